```python
import math
import jax, jax.numpy as jnp
from jax import lax
import numpy as np

D_MODEL = 2048
BATCH = 8
SEQ = 2048
DEPTH = 1

A_HEADS = 8
A_HEAD_DIM = 128
A_WIDTH = A_HEADS * A_HEAD_DIM
MOBA_BLOCK = 256
MOBA_TOPK = 3
MOBA_Q_CHUNK = 16
REL_BUCKETS = 32
REL_MAX_EXACT = 16
REL_MAX_DIST = 128
B_HEADS = 8
B_KEY_DIM = 128
B_VAL_DIM = 128
B_FWIDTH = B_HEADS * B_KEY_DIM
B_IWIDTH = B_HEADS * B_VAL_DIM
HGRN_CHUNK = 64
D_FF = ((8 * D_MODEL // 3 + 255) // 256) * 256
PLE_DIM = 256
NORM_EPS = 1e-6
IN_SIZES = (A_WIDTH, A_WIDTH, A_WIDTH, B_FWIDTH, B_FWIDTH, B_IWIDTH, B_IWIDTH, D_MODEL, D_MODEL)
IN_WIDTH = sum(IN_SIZES)

kernel_name = "hybrid_moba_hgrn2_gated_block"


def rms_norm(x, gain):
    xf = x.astype(jnp.float32)
    y = xf * lax.rsqrt(jnp.mean(xf * xf, axis=-1, keepdims=True) + NORM_EPS)
    return (y * gain.astype(jnp.float32)).astype(x.dtype)


def t5_bucket(rel):
    n = jnp.maximum(rel, 0)
    nf = jnp.maximum(n, REL_MAX_EXACT).astype(jnp.float32)
    large = REL_MAX_EXACT + (jnp.log(nf / REL_MAX_EXACT) / math.log(REL_MAX_DIST / REL_MAX_EXACT)
                             * (REL_BUCKETS - REL_MAX_EXACT)).astype(jnp.int32)
    large = jnp.minimum(large, REL_BUCKETS - 1)
    return jnp.where(n < REL_MAX_EXACT, n, large)


def moba_attention(q, k, v, rel_bias):
    B, S, H, dh = q.shape
    nb = -(-S // MOBA_BLOCK)
    s_pad = nb * MOBA_BLOCK
    pad = ((0, 0), (0, s_pad - S), (0, 0), (0, 0))
    q = q.transpose(0, 2, 1, 3)
    kp = jnp.pad(k, pad).transpose(0, 2, 1, 3).reshape(B, H, nb, MOBA_BLOCK, dh)
    vp = jnp.pad(v, pad).transpose(0, 2, 1, 3).reshape(B, H, nb, MOBA_BLOCK, dh)
    scale = 1.0 / math.sqrt(dh)

    k_mean = jnp.mean(kp.astype(jnp.float32), axis=3)
    scores = jnp.einsum('bhsd,bhnd->bhsn', q.astype(jnp.float32), k_mean)
    q_blk = jnp.arange(S, dtype=jnp.int32) // MOBA_BLOCK
    past = jnp.arange(nb, dtype=jnp.int32)[None, :] < q_blk[:, None]
    scores = jnp.where(past[None, None], scores, -jnp.inf)
    own = jnp.broadcast_to(q_blk[None, None, :, None], (B, H, S, 1))
    n_sel = min(MOBA_TOPK, nb - 1)
    if n_sel > 0:
        _, top_idx = lax.top_k(scores, n_sel)
        top_idx = top_idx.astype(jnp.int32)
        sel = jnp.concatenate([top_idx, own], axis=-1)
        valid = jnp.concatenate([top_idx < own, jnp.ones_like(own, dtype=bool)], axis=-1)
    else:
        sel = own
        valid = jnp.ones_like(own, dtype=bool)
    J = sel.shape[-1]

    nqc = S // MOBA_Q_CHUNK
    def chunks(t):
        return jnp.moveaxis(t.reshape(B, H, nqc, MOBA_Q_CHUNK, *t.shape[3:]), 2, 0)
    xs = (chunks(q), chunks(sel), chunks(valid),
          jnp.arange(S, dtype=jnp.int32).reshape(nqc, MOBA_Q_CHUNK))
    bi = jnp.arange(B)[:, None, None, None]
    hi = jnp.arange(H)[None, :, None, None]
    hi5 = jnp.arange(H)[None, :, None, None, None]
    t_off = jnp.arange(MOBA_BLOCK, dtype=jnp.int32)

    def step(args):
        qc, sel_c, val_c, qpos = args
        kg = kp[bi, hi, sel_c]
        vg = vp[bi, hi, sel_c]
        logits = jnp.einsum('bhqd,bhqjtd->bhqjt', qc, kg).astype(jnp.float32) * scale
        kpos = sel_c[..., None] * MOBA_BLOCK + t_off
        rel = qpos[None, None, :, None, None] - kpos
        bias = rel_bias[t5_bucket(rel), hi5].astype(jnp.float32)
        mask = val_c[..., None] & (rel >= 0)
        logits = jnp.where(mask, logits + bias, -jnp.inf)
        probs = jax.nn.softmax(logits.reshape(B, H, MOBA_Q_CHUNK, J * MOBA_BLOCK), axis=-1)
        probs = probs.reshape(B, H, MOBA_Q_CHUNK, J, MOBA_BLOCK).astype(vg.dtype)
        return jnp.einsum('bhqjt,bhqjtd->bhqd', probs, vg)

    out = lax.map(step, xs)
    return out.transpose(1, 0, 3, 2, 4).reshape(B, S, H * dh)


def hgrn2_mixer(q, f_pre, i_in, g_out, lb, norm_gain):
    B, S, _ = q.shape
    dt = q.dtype
    f = lb.astype(jnp.float32) + (1.0 - lb.astype(jnp.float32)) * jax.nn.sigmoid(f_pre.astype(jnp.float32))
    log_f = jnp.log(f)
    k = 1.0 - f
    nc = S // HGRN_CHUNK
    def chunks(t, d):
        return t.astype(jnp.float32).reshape(B, nc, HGRN_CHUNK, B_HEADS, d).transpose(1, 0, 3, 2, 4)
    qc = chunks(q, B_KEY_DIM)
    kc = chunks(k, B_KEY_DIM)
    gc = chunks(log_f, B_KEY_DIM)
    vc = chunks(i_in, B_VAL_DIM)
    b = jnp.cumsum(gc, axis=3)
    b_last = b[:, :, :, -1:, :]
    q_dec = qc * jnp.exp(b)
    k_dec = kc * jnp.exp(-b)
    k_end = kc * jnp.exp(b_last - b)
    causal = jnp.tril(jnp.ones((HGRN_CHUNK, HGRN_CHUNK), jnp.float32))
    a = jnp.einsum('nbhtd,nbhsd->nbhts', q_dec, k_dec) * causal
    o_intra = jnp.einsum('nbhts,nbhsv->nbhtv', a, vc)

    def scan_fn(state, xs):
        q_d, k_e, v_c, decay = xs
        o_inter = jnp.einsum('bhtd,bhdv->bhtv', q_d, state)
        state = decay[:, :, 0, :, None] * state + jnp.einsum('bhsd,bhsv->bhdv', k_e, v_c)
        return state, o_inter

    s0 = jnp.zeros((B, B_HEADS, B_KEY_DIM, B_VAL_DIM), jnp.float32)
    _, o_inter = lax.scan(scan_fn, s0, (q_dec, k_end, vc, jnp.exp(b_last)))
    o = (o_intra + o_inter).transpose(1, 0, 3, 2, 4).reshape(B, S, B_HEADS, B_VAL_DIM)
    o = rms_norm(o.astype(dt), norm_gain.reshape(B_HEADS, B_VAL_DIM)).reshape(B, S, B_IWIDTH)
    return o * jax.nn.silu(g_out)


def setup_inputs(seed: int = 0) -> dict:
    key = jax.random.key(seed)
    ks = jax.random.split(key, 16)
    def w(k, shape, fan_in):
        return jax.random.normal(k, shape, jnp.float32) * fan_in ** -0.5
    def gain(k, shape):
        return 1.0 + 0.01 * jax.random.normal(k, shape, jnp.float32)
    return {
        "x": jax.random.normal(ks[0], (BATCH, SEQ, D_MODEL), jnp.float32),
        "p": jax.random.normal(ks[1], (DEPTH, BATCH, SEQ, PLE_DIM), jnp.float32),
        "norm_mix": gain(ks[2], (DEPTH, D_MODEL)),
        "w_in": w(ks[3], (DEPTH, D_MODEL, IN_WIDTH), D_MODEL),
        "hgrn_norm": gain(ks[4], (DEPTH, B_IWIDTH)),
        "w_proj_a": w(ks[5], (DEPTH, A_WIDTH, D_MODEL), A_WIDTH),
        "w_proj_b": w(ks[6], (DEPTH, B_IWIDTH, D_MODEL), B_IWIDTH),
        "w_out": w(ks[7], (DEPTH, D_MODEL, D_MODEL), D_MODEL),
        "norm_ffn": gain(ks[8], (DEPTH, D_MODEL)),
        "w_gate_up": w(ks[9], (DEPTH, D_MODEL, 2 * D_FF), D_MODEL),
        "w_down": w(ks[10], (DEPTH, D_FF, D_MODEL), D_FF),
        "w_ple": w(ks[11], (DEPTH, PLE_DIM, D_MODEL), PLE_DIM),
        "w_ple_gate": w(ks[12], (DEPTH, D_MODEL, D_MODEL), D_MODEL),
        "rel_bias": 0.1 * jax.random.normal(ks[13], (REL_BUCKETS, A_HEADS), jnp.float32),
        "hgrn_lb_logits": 0.1 * jax.random.normal(ks[14], (DEPTH + 1, B_FWIDTH), jnp.float32),
        "norm_final": gain(ks[15], (D_MODEL,)),
    }


def reference(x, p, norm_mix, w_in, hgrn_norm, w_proj_a, w_proj_b, w_out, norm_ffn, w_gate_up,
              w_down, w_ple, w_ple_gate, rel_bias, hgrn_lb_logits, norm_final):
    B, S, _ = x.shape
    split_points = list(np.cumsum(IN_SIZES)[:-1])
    lb_all = jnp.cumsum(jax.nn.softmax(hgrn_lb_logits.astype(jnp.float32), axis=0), axis=0)
    for i in range(DEPTH):
        h = rms_norm(x, norm_mix[i])
        proj = h @ w_in[i]
        qa, ka, va, qb, fb, ib, gb, gate_a, gate_b = jnp.split(proj, split_points, axis=-1)
        ya = moba_attention(qa.reshape(B, S, A_HEADS, A_HEAD_DIM),
                            ka.reshape(B, S, A_HEADS, A_HEAD_DIM),
                            va.reshape(B, S, A_HEADS, A_HEAD_DIM), rel_bias)
        yb = hgrn2_mixer(qb, fb, ib, gb, lb_all[i], hgrn_norm[i])
        mixed = jax.nn.sigmoid(gate_a) * (ya @ w_proj_a[i]) + jax.nn.sigmoid(gate_b) * (yb @ w_proj_b[i])
        x = x + mixed @ w_out[i]
        h = rms_norm(x, norm_ffn[i])
        gate, up = jnp.split(h @ w_gate_up[i], 2, axis=-1)
        x = x + (jax.nn.silu(gate) * up) @ w_down[i]
        x = x + jax.nn.sigmoid(x @ w_ple_gate[i]) * (p[i] @ w_ple[i])
    return rms_norm(x, norm_final)
```

```python
import functools
import math

import jax
import jax.numpy as jnp
import numpy as np
from jax import lax
from jax.experimental import pallas as pl
from jax.experimental.pallas import tpu as pltpu

F32 = jnp.float32
BF16 = jnp.bfloat16

D_MODEL = 2048
HEAD_DIM = 128
A_WIDTH = 1024
B_WIDTH = 1024
MOBA_BLOCK = 256
MOBA_TOPK = 3
REL_BUCKETS = 32
REL_MAX_EXACT = 16
REL_MAX_DIST = 128
HGRN_CHUNK = 64
D_FF = 5632
NORM_EPS = 1e-6

COL_QA, COL_KA, COL_VA, COL_QB, COL_FB, COL_IB, COL_GB, COL_GATE_A, COL_GATE_B = 0, 1, 2, 3, 4, 5, 6, 7, 9
IN_WIDTH = 11 * 1024
SEG = 1024

MASK_VALUE = -1e30
VMEM_LIMIT = 56 * 1024 * 1024

NT_DIMS = (((1,), (1,)), ((), ()))


def _bucket_thresholds():
    n = np.arange(0, 4096, dtype=np.int64)
    nf = np.maximum(n, REL_MAX_EXACT).astype(np.float64)
    large = REL_MAX_EXACT + (np.log(nf / REL_MAX_EXACT) / math.log(REL_MAX_DIST / REL_MAX_EXACT)
                             * (REL_BUCKETS - REL_MAX_EXACT)).astype(np.int64)
    large = np.minimum(large, REL_BUCKETS - 1)
    bucket = np.where(n < REL_MAX_EXACT, n, large)
    return [int(np.argmax(bucket >= k)) for k in range(REL_BUCKETS)]


BUCKET_LO = _bucket_thresholds()


def _params(*sem):
    return pltpu.CompilerParams(dimension_semantics=sem, vmem_limit_bytes=VMEM_LIMIT)


def _resident(shape):
    return pl.BlockSpec(shape, lambda *_: (0,) * len(shape), pipeline_mode=pl.Buffered(1))


def _in_proj_kernel(x_ref, gain_ref, w_ref, cscale_ref, o_ref, fb_ref, h_scr, *, rows_per_step):
    j = pl.program_id(1)

    @pl.when(j == 0)
    def _():
        def body(r, _):
            rs = pl.ds(pl.multiple_of(r * rows_per_step, rows_per_step), rows_per_step)
            x = x_ref[rs, :]
            ms = jnp.mean(x * x, axis=-1, keepdims=True)
            h_scr[rs, :] = (x * lax.rsqrt(ms + NORM_EPS) * gain_ref[...]).astype(BF16)
            return 0
        lax.fori_loop(0, x_ref.shape[0] // rows_per_step, body, 0)

    acc = jnp.dot(h_scr[...], w_ref[...], preferred_element_type=F32)
    o_ref[...] = (acc * cscale_ref[...]).astype(BF16)

    @pl.when(j == COL_FB)
    def _():
        fb_ref[...] = acc


def _in_proj(x2d, gain, w_bf16, cscale, *, tm=1024, tn=SEG):
    n, d = x2d.shape
    width = w_bf16.shape[1]
    return pl.pallas_call(
        functools.partial(_in_proj_kernel, rows_per_step=128),
        grid=(n // tm, width // tn),
        in_specs=[
            pl.BlockSpec((tm, d), lambda i, j: (i, 0)),
            pl.BlockSpec((1, d), lambda i, j: (0, 0)),
            pl.BlockSpec((d, tn), lambda i, j: (0, j)),
            pl.BlockSpec((1, tn), lambda i, j: (0, j)),
        ],
        out_specs=[
            pl.BlockSpec((tm, tn), lambda i, j: (i, j)),
            pl.BlockSpec((tm, SEG), lambda i, j: (i, 0)),
        ],
        out_shape=[
            jax.ShapeDtypeStruct((n, width), BF16),
            jax.ShapeDtypeStruct((n, SEG), F32),
        ],
        scratch_shapes=[pltpu.VMEM((tm, d), BF16)],
        compiler_params=_params("arbitrary", "arbitrary"),
        name="in_proj",
    )(x2d, gain, w_bf16, cscale)


def _moba_kernel(rb_ref, q_ref, k_ref, v_ref, o_ref, bias_scr, km_scr, vt_scr, sel_scr, *, n_blocks):
    hd = pl.program_id(0)
    b = pl.program_id(1)
    qi = pl.program_id(2)
    blk = MOBA_BLOCK

    @pl.when((b == 0) & (qi == 0))
    def _build_bias():
        r = lax.broadcasted_iota(jnp.int32, (blk, blk), 1)
        c = lax.broadcasted_iota(jnp.int32, (blk, blk), 0)
        far = rb_ref[REL_BUCKETS - 1, hd]
        for d in range(2):
            rel = d * blk + r - c
            bias = jnp.full((blk, blk), far, F32)
            for kk in range(REL_BUCKETS - 2, -1, -1):
                bias = jnp.where(rel < BUCKET_LO[kk + 1], rb_ref[kk, hd], bias)
            if d == 0:
                bias = jnp.where(rel < 0, MASK_VALUE, bias)
            bias_scr[d] = bias
        bias_scr[2] = jnp.full((blk, blk), far, F32)

    @pl.when(qi == 0)
    def _per_sequence():
        km_scr[...] = jnp.zeros_like(km_scr)
        for jb in range(n_blocks):
            kb = k_ref[0, jb * blk:(jb + 1) * blk, :].astype(F32)
            km_scr[jb:jb + 1, :] = jnp.mean(kb, axis=0, keepdims=True)
            vb = v_ref[0, jb * blk:(jb + 1) * blk, :].astype(F32)
            vt_scr[jb] = vb.T.astype(BF16)

    q = q_ref[0]

    km = km_scr[...]
    km_hi = km.astype(BF16)
    km_lo = (km - km_hi.astype(F32)).astype(BF16)
    sc = (lax.dot_general(km_hi, q, NT_DIMS, preferred_element_type=F32)
          + lax.dot_general(km_lo, q, NT_DIMS, preferred_element_type=F32))
    rows = lax.broadcasted_iota(jnp.int32, sc.shape, 0)
    rows_f = rows.astype(F32)
    past = rows < qi
    s = jnp.where(past, sc, -jnp.inf)
    picked = jnp.zeros(sc.shape, jnp.bool_)
    for _ in range(MOBA_TOPK):
        top = jnp.max(s, axis=0, keepdims=True)
        first = jnp.min(jnp.where(s == top, rows_f, 1e9), axis=0, keepdims=True)
        hit = rows_f == first
        picked = picked | hit
        s = jnp.where(hit, -jnp.inf, s)
    sel_scr[...] = jnp.where(picked & past, 0.0, MASK_VALUE)

    def logits(j, d):
        kj = k_ref[0, pl.ds(pl.multiple_of(j * blk, blk), blk), :]
        return lax.dot_general(kj, q, NT_DIMS, preferred_element_type=F32) + bias_scr[jnp.minimum(d, 2)]

    s0 = logits(qi, 0)
    m0 = jnp.max(s0, axis=0, keepdims=True)
    p0 = jnp.exp(s0 - m0)
    l0 = jnp.sum(p0, axis=0, keepdims=True)
    acc0 = jnp.dot(vt_scr[qi], p0.astype(BF16), preferred_element_type=F32)

    def body(d, carry):
        m, l, acc = carry
        j = qi - d
        sj = logits(j, d) + sel_scr[pl.ds(j, 1), :]
        m_new = jnp.maximum(m, jnp.max(sj, axis=0, keepdims=True))
        alpha = jnp.exp(m - m_new)
        p = jnp.exp(sj - m_new)
        l = alpha * l + jnp.sum(p, axis=0, keepdims=True)
        acc = alpha * acc + jnp.dot(vt_scr[j], p.astype(BF16), preferred_element_type=F32)
        return m_new, l, acc

    _, l, acc = lax.fori_loop(1, qi + 1, body, (m0, l0, acc0))
    o_ref[0] = (acc / l).T.astype(o_ref.dtype)


def _moba(proj3, rel_bias, *, heads):
    bsz, seq, _ = proj3.shape
    nb = seq // MOBA_BLOCK
    dh = HEAD_DIM
    cq, ck, cv = (c * (SEG // dh) for c in (COL_QA, COL_KA, COL_VA))
    return pl.pallas_call(
        functools.partial(_moba_kernel, n_blocks=nb),
        grid=(heads, bsz, nb),
        in_specs=[
            pl.BlockSpec(memory_space=pltpu.SMEM),
            pl.BlockSpec((1, MOBA_BLOCK, dh), lambda h, b, i: (b, i, cq + h)),
            pl.BlockSpec((1, seq, dh), lambda h, b, i: (b, 0, ck + h)),
            pl.BlockSpec((1, seq, dh), lambda h, b, i: (b, 0, cv + h)),
        ],
        out_specs=pl.BlockSpec((1, MOBA_BLOCK, dh), lambda h, b, i: (b, i, h)),
        out_shape=jax.ShapeDtypeStruct((bsz, seq, heads * dh), BF16),
        scratch_shapes=[
            pltpu.VMEM((3, MOBA_BLOCK, MOBA_BLOCK), F32),
            pltpu.VMEM((16, dh), F32),
            pltpu.VMEM((nb, dh, MOBA_BLOCK), BF16),
            pltpu.VMEM((16, MOBA_BLOCK), F32),
        ],
        compiler_params=_params("arbitrary", "arbitrary", "arbitrary"),
        name="moba",
    )(rel_bias, proj3, proj3, proj3)


def _cumsum_rows(x):
    n = x.shape[0]
    rows = lax.broadcasted_iota(jnp.int32, x.shape, 0)
    sh = 1
    while sh < n:
        x = x + jnp.where(rows >= sh, pltpu.roll(x, sh, axis=0), 0.0)
        sh *= 2
    return x


def _hgrn_kernel(lbl_ref, gain_ref, q_ref, f_ref, i_ref, g_ref, o_ref, st_scr, *, heads, layer):
    c = pl.program_id(1)
    ch = HGRN_CHUNK
    dh = HEAD_DIM

    @pl.when(c == 0)
    def _():
        st_scr[...] = jnp.zeros_like(st_scr)

    lg = lbl_ref[...]
    e = jnp.exp(lg - jnp.max(lg, axis=0, keepdims=True))
    lb = jnp.sum(e[:layer + 1], axis=0, keepdims=True) / jnp.sum(e, axis=0, keepdims=True)

    f = lb + (1.0 - lb) * jax.nn.sigmoid(f_ref[0])
    bcum = _cumsum_rows(jnp.log(f))
    b_last = bcum[ch - 1:ch, :]
    kk = 1.0 - f
    q_dec = (q_ref[0].astype(F32) * jnp.exp(bcum)).astype(BF16)
    k_dec = (kk * jnp.exp(-bcum)).astype(BF16)
    k_end = (kk * jnp.exp(b_last - bcum)).astype(BF16)
    decay = jnp.exp(b_last)
    v = i_ref[0]
    gate = g_ref[0].astype(F32)
    gate = gate * jax.nn.sigmoid(gate)
    gain = gain_ref[...]

    tri = (lax.broadcasted_iota(jnp.int32, (ch, ch), 0) >= lax.broadcasted_iota(jnp.int32, (ch, ch), 1))
    eye = (lax.broadcasted_iota(jnp.int32, (dh, dh), 0) == lax.broadcasted_iota(jnp.int32, (dh, dh), 1)).astype(BF16)

    for hd in range(heads):
        sl = slice(hd * dh, (hd + 1) * dh)
        qd, kd, ke, vh = q_dec[:, sl], k_dec[:, sl], k_end[:, sl], v[:, sl]
        st = st_scr[hd]
        a = lax.dot_general(qd, kd, NT_DIMS, preferred_element_type=F32)
        a = jnp.where(tri, a, 0.0).astype(BF16)
        o = (jnp.dot(a, vh, preferred_element_type=F32)
             + lax.dot_general(qd, st.astype(BF16), NT_DIMS, preferred_element_type=F32))
        vt = lax.dot_general(eye, vh, NT_DIMS, preferred_element_type=F32).astype(BF16)
        st_scr[hd] = st * decay[:, sl] + jnp.dot(vt, ke, preferred_element_type=F32)
        ms = jnp.mean(o * o, axis=-1, keepdims=True)
        y = o * lax.rsqrt(ms + NORM_EPS) * gain[:, sl]
        o_ref[0, :, sl] = (y * gate[:, sl]).astype(o_ref.dtype)


def _hgrn(proj3, fb3, lb_logits, gain, *, heads, layer):
    bsz, seq, _ = proj3.shape
    width = heads * HEAD_DIM
    blk = (1, HGRN_CHUNK, width)
    return pl.pallas_call(
        functools.partial(_hgrn_kernel, heads=heads, layer=layer),
        grid=(bsz, seq // HGRN_CHUNK),
        in_specs=[
            pl.BlockSpec(lb_logits.shape, lambda b, c: (0, 0)),
            pl.BlockSpec((1, width), lambda b, c: (0, 0)),
            pl.BlockSpec(blk, lambda b, c: (b, c, COL_QB)),
            pl.BlockSpec(blk, lambda b, c: (b, c, 0)),
            pl.BlockSpec(blk, lambda b, c: (b, c, COL_IB)),
            pl.BlockSpec(blk, lambda b, c: (b, c, COL_GB)),
        ],
        out_specs=pl.BlockSpec(blk, lambda b, c: (b, c, 0)),
        out_shape=jax.ShapeDtypeStruct((bsz, seq, width), BF16),
        scratch_shapes=[pltpu.VMEM((heads, HEAD_DIM, HEAD_DIM), F32)],
        compiler_params=_params("arbitrary", "arbitrary"),
        name="hgrn2",
    )(lb_logits, gain, proj3, fb3, proj3, proj3)


def _mix_kernel(ya_ref, yb_ref, ga0_ref, ga1_ref, gb0_ref, gb1_ref, x_ref, wa_ref, wb_ref, wo_ref, gain_ref,
                x1_ref, h2_ref, mixed_scr, *, tc):
    d = x_ref.shape[1]
    gate_refs = ((ga0_ref, gb0_ref), (ga1_ref, gb1_ref))
    ya = ya_ref[...]
    yb = yb_ref[...]
    for cb in range(d // tc):
        cs = slice(cb * tc, (cb + 1) * tc)
        ga_ref, gb_ref = gate_refs[(cb * tc) // SEG]
        gs = slice((cb * tc) % SEG, (cb * tc) % SEG + tc)
        pa = jnp.dot(ya, wa_ref[:, cs], preferred_element_type=F32)
        pb = jnp.dot(yb, wb_ref[:, cs], preferred_element_type=F32)
        mixed = (jax.nn.sigmoid(ga_ref[:, gs].astype(F32)) * pa
                 + jax.nn.sigmoid(gb_ref[:, gs].astype(F32)) * pb)
        mixed_scr[:, cs] = mixed.astype(BF16)
    ssq = jnp.zeros((x_ref.shape[0], 1), F32)
    for cb in range(d // tc):
        cs = slice(cb * tc, (cb + 1) * tc)
        x1 = x_ref[:, cs] + jnp.dot(mixed_scr[...], wo_ref[:, cs], preferred_element_type=F32)
        x1_ref[:, cs] = x1
        ssq = ssq + jnp.sum(x1 * x1, axis=-1, keepdims=True)
    inv = lax.rsqrt(ssq * (1.0 / d) + NORM_EPS)
    for cb in range(d // tc):
        cs = slice(cb * tc, (cb + 1) * tc)
        h2_ref[:, cs] = (x1_ref[:, cs] * inv * gain_ref[:, cs]).astype(BF16)


def _mix(ya, yb, proj, x2d, wa, wb, wo, gain, *, tm=512, tc=512):
    n, d = x2d.shape
    row = lambda i: (i, 0)
    seg = lambda c: pl.BlockSpec((tm, SEG), lambda i: (i, c))
    return pl.pallas_call(
        functools.partial(_mix_kernel, tc=tc),
        grid=(n // tm,),
        in_specs=[
            pl.BlockSpec((tm, A_WIDTH), row),
            pl.BlockSpec((tm, B_WIDTH), row),
            seg(COL_GATE_A), seg(COL_GATE_A + 1), seg(COL_GATE_B), seg(COL_GATE_B + 1),
            pl.BlockSpec((tm, d), row),
            _resident(wa.shape), _resident(wb.shape), _resident(wo.shape), _resident(gain.shape),
        ],
        out_specs=[pl.BlockSpec((tm, d), row), pl.BlockSpec((tm, d), row)],
        out_shape=[jax.ShapeDtypeStruct((n, d), F32), jax.ShapeDtypeStruct((n, d), BF16)],
        scratch_shapes=[pltpu.VMEM((tm, d), BF16)],
        compiler_params=_params("arbitrary"),
        name="mix",
    )(ya, yb, proj, proj, proj, proj, x2d, wa, wb, wo, gain)


def _ffn_kernel(h_ref, x1_ref, wg_ref, wu_ref, wd_ref, o_ref):
    j = pl.program_id(1)
    h = h_ref[...]
    g = jnp.dot(h, wg_ref[...], preferred_element_type=F32)
    u = jnp.dot(h, wu_ref[...], preferred_element_type=F32)
    act = (g * jax.nn.sigmoid(g) * u).astype(BF16)
    part = jnp.dot(act, wd_ref[...], preferred_element_type=F32)

    @pl.when(j == 0)
    def _():
        o_ref[...] = x1_ref[...] + part

    @pl.when(j > 0)
    def _():
        o_ref[...] += part


def _ffn(h2, x1, w_gu, w_down, *, tm=512, tf=512):
    n, d = x1.shape
    dff = w_down.shape[0]
    nj = dff // tf
    return pl.pallas_call(
        _ffn_kernel,
        grid=(n // tm, nj),
        in_specs=[
            pl.BlockSpec((tm, d), lambda i, j: (i, 0)),
            pl.BlockSpec((tm, d), lambda i, j: (i, 0)),
            pl.BlockSpec((d, tf), lambda i, j: (0, j)),
            pl.BlockSpec((d, tf), lambda i, j: (0, j + nj)),
            pl.BlockSpec((tf, d), lambda i, j: (j, 0)),
        ],
        out_specs=pl.BlockSpec((tm, d), lambda i, j: (i, 0)),
        out_shape=jax.ShapeDtypeStruct((n, d), F32),
        compiler_params=_params("arbitrary", "arbitrary"),
        name="ffn",
    )(h2, x1, w_gu, w_gu, w_down)


def _ple_kernel(x_ref, p_ref, wg_ref, wp_ref, gain_ref, o_ref, *, tc, final_norm):
    d = x_ref.shape[1]
    xb = x_ref[...].astype(BF16)
    pb = p_ref[...].astype(BF16)
    ssq = jnp.zeros((x_ref.shape[0], 1), F32)
    for cb in range(d // tc):
        cs = slice(cb * tc, (cb + 1) * tc)
        gate = jax.nn.sigmoid(jnp.dot(xb, wg_ref[:, cs], preferred_element_type=F32))
        pe = jnp.dot(pb, wp_ref[:, cs], preferred_element_type=F32)
        x3 = x_ref[:, cs] + gate * pe
        o_ref[:, cs] = x3
        ssq = ssq + jnp.sum(x3 * x3, axis=-1, keepdims=True)
    if not final_norm:
        return
    inv = lax.rsqrt(ssq * (1.0 / d) + NORM_EPS)
    for cb in range(d // tc):
        cs = slice(cb * tc, (cb + 1) * tc)
        o_ref[:, cs] = o_ref[:, cs] * inv * gain_ref[:, cs]


def _ple(x2, p2d, wg, wp, gain, *, final_norm, tm=512, tc=512):
    n, d = x2.shape
    row = lambda i: (i, 0)
    return pl.pallas_call(
        functools.partial(_ple_kernel, tc=tc, final_norm=final_norm),
        grid=(n // tm,),
        in_specs=[
            pl.BlockSpec((tm, d), row),
            pl.BlockSpec((tm, p2d.shape[1]), row),
            _resident(wg.shape), _resident(wp.shape), _resident(gain.shape),
        ],
        out_specs=pl.BlockSpec((tm, d), row),
        out_shape=jax.ShapeDtypeStruct((n, d), F32),
        compiler_params=_params("arbitrary"),
        name="ple",
    )(x2, p2d, wg, wp, gain)


def kernel(x, p, norm_mix, w_in, hgrn_norm, w_proj_a, w_proj_b, w_out, norm_ffn, w_gate_up, w_down, w_ple,
           w_ple_gate, rel_bias, hgrn_lb_logits, norm_final):
    bsz, seq, d = x.shape
    n = bsz * seq
    depth = w_in.shape[0]
    a_heads = A_WIDTH // HEAD_DIM
    b_heads = B_WIDTH // HEAD_DIM

    cscale = np.ones((1, IN_WIDTH), np.float32)
    cscale[:, COL_QA * SEG:(COL_QA + 1) * SEG] = 1.0 / math.sqrt(HEAD_DIM)
    cscale = jnp.asarray(cscale)

    xc = x.reshape(n, d)
    for i in range(depth):
        proj, fb = _in_proj(xc, norm_mix[i][None], w_in[i].astype(BF16), cscale)
        proj3 = proj.reshape(bsz, seq, IN_WIDTH)
        ya = _moba(proj3, rel_bias, heads=a_heads)
        yb = _hgrn(proj3, fb.reshape(bsz, seq, SEG), hgrn_lb_logits, hgrn_norm[i][None], heads=b_heads, layer=i)
        x1, h2 = _mix(ya.reshape(n, A_WIDTH), yb.reshape(n, B_WIDTH), proj, xc,
                      w_proj_a[i].astype(BF16), w_proj_b[i].astype(BF16), w_out[i].astype(BF16), norm_ffn[i][None])
        x2 = _ffn(h2, x1, w_gate_up[i].astype(BF16), w_down[i].astype(BF16))
        xc = _ple(x2, p[i].reshape(n, -1), w_ple_gate[i].astype(BF16), w_ple[i].astype(BF16), norm_final[None],
                  final_norm=(i == depth - 1))
    return xc.reshape(bsz, seq, d)
```

```python
import functools
import math

import jax
import jax.numpy as jnp
import numpy as np
from jax import lax
from jax.experimental import pallas as pl
from jax.experimental.pallas import tpu as pltpu

F32 = jnp.float32
BF16 = jnp.bfloat16

D_MODEL = 2048
HEAD_DIM = 128
A_WIDTH = 1024
B_WIDTH = 1024
MOBA_BLOCK = 256
MOBA_TOPK = 3
REL_BUCKETS = 32
REL_MAX_EXACT = 16
REL_MAX_DIST = 128
HGRN_CHUNK = 64
D_FF = 5632
NORM_EPS = 1e-6

COL_QA, COL_KA, COL_VA, COL_QB, COL_FB, COL_IB, COL_GB, COL_GATE_A, COL_GATE_B = 0, 1, 2, 3, 4, 5, 6, 7, 9
IN_WIDTH = 11 * 1024
SEG = 1024

MASK_VALUE = -1e30
VMEM_LIMIT = 56 * 1024 * 1024

NT_DIMS = (((1,), (1,)), ((), ()))
LOG2E = math.log2(math.e)


def _bucket_thresholds():
    n = np.arange(0, 4096, dtype=np.int64)
    nf = np.maximum(n, REL_MAX_EXACT).astype(np.float64)
    large = REL_MAX_EXACT + (np.log(nf / REL_MAX_EXACT) / math.log(REL_MAX_DIST / REL_MAX_EXACT)
                             * (REL_BUCKETS - REL_MAX_EXACT)).astype(np.int64)
    large = np.minimum(large, REL_BUCKETS - 1)
    bucket = np.where(n < REL_MAX_EXACT, n, large)
    return [int(np.argmax(bucket >= k)) for k in range(REL_BUCKETS)]


BUCKET_LO = _bucket_thresholds()


def _params(*sem):
    return pltpu.CompilerParams(dimension_semantics=sem, vmem_limit_bytes=VMEM_LIMIT)


def _resident(shape):
    return pl.BlockSpec(shape, lambda *_: (0,) * len(shape), pipeline_mode=pl.Buffered(1))


def _in_proj_kernel(x_ref, gain_ref, w_ref, cscale_ref, o_ref, fb_ref, h_scr, *, rows_per_step):
    j = pl.program_id(1)

    @pl.when(j == 0)
    def _():
        def body(r, _):
            rs = pl.ds(pl.multiple_of(r * rows_per_step, rows_per_step), rows_per_step)
            x = x_ref[rs, :]
            ms = jnp.mean(x * x, axis=-1, keepdims=True)
            h_scr[rs, :] = (x * lax.rsqrt(ms + NORM_EPS) * gain_ref[...]).astype(BF16)
            return 0
        lax.fori_loop(0, x_ref.shape[0] // rows_per_step, body, 0)

    acc = jnp.dot(h_scr[...], w_ref[...], preferred_element_type=F32)
    o_ref[...] = (acc * cscale_ref[...]).astype(BF16)
    pltpu.store(fb_ref, acc, mask=jnp.broadcast_to(j == COL_FB, acc.shape))


def _in_proj(x2d, gain, w_bf16, cscale, *, tm=1024, tn=SEG):
    n, d = x2d.shape
    width = w_bf16.shape[1]
    return pl.pallas_call(
        functools.partial(_in_proj_kernel, rows_per_step=128),
        grid=(n // tm, width // tn),
        in_specs=[
            pl.BlockSpec((tm, d), lambda i, j: (i, 0)),
            pl.BlockSpec((1, d), lambda i, j: (0, 0)),
            pl.BlockSpec((d, tn), lambda i, j: (0, j)),
            pl.BlockSpec((1, tn), lambda i, j: (0, j)),
        ],
        out_specs=[
            pl.BlockSpec((tm, tn), lambda i, j: (i, j)),
            pl.BlockSpec((tm, SEG), lambda i, j: (i, 0)),
        ],
        out_shape=[
            jax.ShapeDtypeStruct((n, width), BF16),
            jax.ShapeDtypeStruct((n, SEG), F32),
        ],
        scratch_shapes=[pltpu.VMEM((tm, d), BF16)],
        compiler_params=_params("arbitrary", "arbitrary"),
        name="in_proj",
    )(x2d, gain, w_bf16, cscale)


def _moba_kernel(rb_ref, qall_ref, q_ref, k_ref, v_ref, o_ref, bias_scr, vt_scr, sel_scr, s_scr, p_scr, *, n_blocks, group):
    hg = pl.program_id(0)
    b = pl.program_id(1)
    qi = pl.program_id(2)
    blk = MOBA_BLOCK
    dh = HEAD_DIM
    sub = 8

    @pl.when((b == 0) & (qi == 0))
    def _build_bias():
        r = lax.broadcasted_iota(jnp.int32, (blk, blk), 1)
        c = lax.broadcasted_iota(jnp.int32, (blk, blk), 0)
        for g in range(group):
            hd = hg * group + g
            for d in range(2):
                rel = d * blk + r - c
                bias = jnp.full((blk, blk), rb_ref[REL_BUCKETS - 1, hd] * LOG2E, F32)
                for kk in range(REL_BUCKETS - 2, -1, -1):
                    bias = jnp.where(rel < BUCKET_LO[kk + 1], rb_ref[kk, hd] * LOG2E, bias)
                if d == 0:
                    bias = jnp.where(rel < 0, MASK_VALUE, bias)
                bias_scr[g, d] = bias

    @pl.when(qi == 0)
    def _per_sequence():
        for g in range(group):
            hs = slice(g * dh, (g + 1) * dh)
            means = []
            for jb in range(n_blocks):
                rs = slice(jb * blk, (jb + 1) * blk)
                means.append(jnp.mean(k_ref[0, rs, hs].astype(F32), axis=0, keepdims=True))
                vt_scr[g, :, rs] = v_ref[0, rs, hs].astype(F32).T.astype(BF16)
            km = jnp.concatenate(means + [jnp.zeros((16 - n_blocks, dh), F32)], axis=0)

            km_hi = km.astype(BF16)
            km_lo = (km - km_hi.astype(F32)).astype(BF16)
            qall = qall_ref[0, :, hs]
            sc = (lax.dot_general(km_hi, qall, NT_DIMS, preferred_element_type=F32)
                  + lax.dot_general(km_lo, qall, NT_DIMS, preferred_element_type=F32))
            rows = lax.broadcasted_iota(jnp.int32, sc.shape, 0)
            rows_f = rows.astype(F32)
            past = rows < lax.broadcasted_iota(jnp.int32, sc.shape, 1) // blk
            s = jnp.where(past, sc, -jnp.inf)
            picked = jnp.zeros(sc.shape, jnp.bool_)
            for _ in range(MOBA_TOPK):
                top = jnp.max(s, axis=0, keepdims=True)
                first = jnp.min(jnp.where(s == top, rows_f, 1e9), axis=0, keepdims=True)
                hit = rows_f == first
                picked = picked | hit
                s = jnp.where(hit, -jnp.inf, s)
            mask = jnp.where(picked & past, 0.0, MASK_VALUE)
            for qb in range(n_blocks):
                sel_scr[g, qb] = mask[:, qb * blk:(qb + 1) * blk]

    def attend(nt):
        hss = [slice(g * dh, (g + 1) * dh) for g in range(group)]
        ms = []
        for g in range(group):
            far = rb_ref[REL_BUCKETS - 1, hg * group + g] * LOG2E
            s = lax.dot_general(k_ref[0, :nt * blk, hss[g]], q_ref[0, :, hss[g]], NT_DIMS,
                                preferred_element_type=F32)
            m8 = None
            for j in range(nt):
                rs = slice(j * blk, (j + 1) * blk)
                d = nt - 1 - j
                if d == 0:
                    sj = s[rs] + bias_scr[g, 0]
                elif d == 1:
                    sj = s[rs] + bias_scr[g, 1] + sel_scr[g, qi, j:j + 1, :]
                else:
                    sj = s[rs] + (sel_scr[g, qi, j:j + 1, :] + far)
                s_scr[g, rs, :] = sj
                mj = jnp.max(sj.reshape(blk // sub, sub, blk), axis=0)
                m8 = mj if m8 is None else jnp.maximum(m8, mj)
            ms.append(jnp.max(m8, axis=0, keepdims=True))
        ls = []
        for g in range(group):
            l8 = jnp.zeros((sub, blk), F32)
            for j in range(nt):
                rs = slice(j * blk, (j + 1) * blk)
                p = jnp.exp2(s_scr[g, rs, :] - ms[g])
                l8 = l8 + jnp.sum(p.reshape(blk // sub, sub, blk), axis=0)
                p_scr[g, rs, :] = p.astype(BF16)
            ls.append(jnp.sum(l8, axis=0, keepdims=True))
        for g in range(group):
            acc = jnp.dot(vt_scr[g, :, :nt * blk], p_scr[g, :nt * blk, :], preferred_element_type=F32)
            o_ref[0, :, hss[g]] = (acc / ls[g]).T.astype(o_ref.dtype)

    for nt in range(1, n_blocks + 1):
        pl.when(qi == nt - 1)(functools.partial(attend, nt))


def _moba(proj3, rel_bias, *, heads, group=4):
    bsz, seq, _ = proj3.shape
    nb = seq // MOBA_BLOCK
    gw = group * HEAD_DIM
    cq, ck, cv = (c * (SEG // gw) for c in (COL_QA, COL_KA, COL_VA))
    return pl.pallas_call(
        functools.partial(_moba_kernel, n_blocks=nb, group=group),
        grid=(heads // group, bsz, nb),
        in_specs=[
            pl.BlockSpec(memory_space=pltpu.SMEM),
            pl.BlockSpec((1, seq, gw), lambda h, b, i: (b, 0, cq + h)),
            pl.BlockSpec((1, MOBA_BLOCK, gw), lambda h, b, i: (b, i, cq + h)),
            pl.BlockSpec((1, seq, gw), lambda h, b, i: (b, 0, ck + h)),
            pl.BlockSpec((1, seq, gw), lambda h, b, i: (b, 0, cv + h)),
        ],
        out_specs=pl.BlockSpec((1, MOBA_BLOCK, gw), lambda h, b, i: (b, i, h)),
        out_shape=jax.ShapeDtypeStruct((bsz, seq, heads * HEAD_DIM), BF16),
        scratch_shapes=[
            pltpu.VMEM((group, 2, MOBA_BLOCK, MOBA_BLOCK), F32),
            pltpu.VMEM((group, HEAD_DIM, seq), BF16),
            pltpu.VMEM((group, nb, 16, MOBA_BLOCK), F32),
            pltpu.VMEM((group, seq, MOBA_BLOCK), F32),
            pltpu.VMEM((group, seq, MOBA_BLOCK), BF16),
        ],
        compiler_params=_params("arbitrary", "arbitrary", "arbitrary"),
        name="moba",
    )(rel_bias, proj3, proj3, proj3, proj3)


def _cumsum_rows(x):
    n = x.shape[0]
    rows = lax.broadcasted_iota(jnp.int32, x.shape, 0)
    sh = 1
    while sh < n:
        x = x + jnp.where(rows >= sh, pltpu.roll(x, sh, axis=0), 0.0)
        sh *= 2
    return x


def _hgrn_kernel(lbl_ref, gain_ref, q_ref, f_ref, i_ref, g_ref, o_ref, st_scr, *, heads, layer):
    c = pl.program_id(1)
    ch = HGRN_CHUNK
    dh = HEAD_DIM

    @pl.when(c == 0)
    def _():
        st_scr[...] = jnp.zeros_like(st_scr)

    lg = lbl_ref[...]
    e = jnp.exp(lg - jnp.max(lg, axis=0, keepdims=True))
    lb = jnp.sum(e[:layer + 1], axis=0, keepdims=True) / jnp.sum(e, axis=0, keepdims=True)

    f = lb + (1.0 - lb) * jax.nn.sigmoid(f_ref[0])
    bcum = _cumsum_rows(jnp.log(f))
    b_last = bcum[ch - 1:ch, :]
    kk = 1.0 - f
    q_dec = (q_ref[0].astype(F32) * jnp.exp(bcum)).astype(BF16)
    k_dec = (kk * jnp.exp(-bcum)).astype(BF16)
    k_end = (kk * jnp.exp(b_last - bcum)).astype(BF16)
    decay = jnp.exp(b_last)
    v = i_ref[0]
    gate = g_ref[0].astype(F32)
    gate = gate * jax.nn.sigmoid(gate)
    gain = gain_ref[...]

    tri = (lax.broadcasted_iota(jnp.int32, (ch, ch), 0) >= lax.broadcasted_iota(jnp.int32, (ch, ch), 1))
    eye = (lax.broadcasted_iota(jnp.int32, (dh, dh), 0) == lax.broadcasted_iota(jnp.int32, (dh, dh), 1)).astype(BF16)

    for hd in range(heads):
        sl = slice(hd * dh, (hd + 1) * dh)
        qd, kd, ke, vh = q_dec[:, sl], k_dec[:, sl], k_end[:, sl], v[:, sl]
        st = st_scr[hd]
        a = lax.dot_general(qd, kd, NT_DIMS, preferred_element_type=F32)
        a = jnp.where(tri, a, 0.0).astype(BF16)
        o = (jnp.dot(a, vh, preferred_element_type=F32)
             + lax.dot_general(qd, st.astype(BF16), NT_DIMS, preferred_element_type=F32))
        vt = lax.dot_general(eye, vh, NT_DIMS, preferred_element_type=F32).astype(BF16)
        st_scr[hd] = st * decay[:, sl] + jnp.dot(vt, ke, preferred_element_type=F32)
        ms = jnp.mean(o * o, axis=-1, keepdims=True)
        y = o * lax.rsqrt(ms + NORM_EPS) * gain[:, sl]
        o_ref[0, :, sl] = (y * gate[:, sl]).astype(o_ref.dtype)


def _hgrn(proj3, fb3, lb_logits, gain, *, heads, layer):
    bsz, seq, _ = proj3.shape
    width = heads * HEAD_DIM
    blk = (1, HGRN_CHUNK, width)
    return pl.pallas_call(
        functools.partial(_hgrn_kernel, heads=heads, layer=layer),
        grid=(bsz, seq // HGRN_CHUNK),
        in_specs=[
            pl.BlockSpec(lb_logits.shape, lambda b, c: (0, 0)),
            pl.BlockSpec((1, width), lambda b, c: (0, 0)),
            pl.BlockSpec(blk, lambda b, c: (b, c, COL_QB)),
            pl.BlockSpec(blk, lambda b, c: (b, c, 0)),
            pl.BlockSpec(blk, lambda b, c: (b, c, COL_IB)),
            pl.BlockSpec(blk, lambda b, c: (b, c, COL_GB)),
        ],
        out_specs=pl.BlockSpec(blk, lambda b, c: (b, c, 0)),
        out_shape=jax.ShapeDtypeStruct((bsz, seq, width), BF16),
        scratch_shapes=[pltpu.VMEM((heads, HEAD_DIM, HEAD_DIM), F32)],
        compiler_params=_params("arbitrary", "arbitrary"),
        name="hgrn2",
    )(lb_logits, gain, proj3, fb3, proj3, proj3)


def _mix_kernel(ya_ref, yb_ref, ga0_ref, ga1_ref, gb0_ref, gb1_ref, x_ref, wa_ref, wb_ref, wo_ref, gain_ref,
                x1_ref, h2_ref, mixed_scr, *, tc):
    d = x_ref.shape[1]
    gate_refs = ((ga0_ref, gb0_ref), (ga1_ref, gb1_ref))
    ya = ya_ref[...]
    yb = yb_ref[...]
    for cb in range(d // tc):
        cs = slice(cb * tc, (cb + 1) * tc)
        ga_ref, gb_ref = gate_refs[(cb * tc) // SEG]
        gs = slice((cb * tc) % SEG, (cb * tc) % SEG + tc)
        pa = jnp.dot(ya, wa_ref[:, cs], preferred_element_type=F32)
        pb = jnp.dot(yb, wb_ref[:, cs], preferred_element_type=F32)
        mixed = (jax.nn.sigmoid(ga_ref[:, gs].astype(F32)) * pa
                 + jax.nn.sigmoid(gb_ref[:, gs].astype(F32)) * pb)
        mixed_scr[:, cs] = mixed.astype(BF16)
    ssq = jnp.zeros((x_ref.shape[0], 1), F32)
    for cb in range(d // tc):
        cs = slice(cb * tc, (cb + 1) * tc)
        x1 = x_ref[:, cs] + jnp.dot(mixed_scr[...], wo_ref[:, cs], preferred_element_type=F32)
        x1_ref[:, cs] = x1
        ssq = ssq + jnp.sum(x1 * x1, axis=-1, keepdims=True)
    inv = lax.rsqrt(ssq * (1.0 / d) + NORM_EPS)
    for cb in range(d // tc):
        cs = slice(cb * tc, (cb + 1) * tc)
        h2_ref[:, cs] = (x1_ref[:, cs] * inv * gain_ref[:, cs]).astype(BF16)


def _mix(ya, yb, proj, x2d, wa, wb, wo, gain, *, tm=512, tc=512):
    n, d = x2d.shape
    row = lambda i: (i, 0)
    seg = lambda c: pl.BlockSpec((tm, SEG), lambda i: (i, c))
    return pl.pallas_call(
        functools.partial(_mix_kernel, tc=tc),
        grid=(n // tm,),
        in_specs=[
            pl.BlockSpec((tm, A_WIDTH), row),
            pl.BlockSpec((tm, B_WIDTH), row),
            seg(COL_GATE_A), seg(COL_GATE_A + 1), seg(COL_GATE_B), seg(COL_GATE_B + 1),
            pl.BlockSpec((tm, d), row),
            _resident(wa.shape), _resident(wb.shape), _resident(wo.shape), _resident(gain.shape),
        ],
        out_specs=[pl.BlockSpec((tm, d), row), pl.BlockSpec((tm, d), row)],
        out_shape=[jax.ShapeDtypeStruct((n, d), F32), jax.ShapeDtypeStruct((n, d), BF16)],
        scratch_shapes=[pltpu.VMEM((tm, d), BF16)],
        compiler_params=_params("arbitrary"),
        name="mix",
    )(ya, yb, proj, proj, proj, proj, x2d, wa, wb, wo, gain)


def _ffn_kernel(h_ref, x1_ref, wg_ref, wu_ref, wd_ref, o_ref):
    @pl.when(pl.program_id(1) == 0)
    def _():
        o_ref[...] = x1_ref[...]

    h = h_ref[...]
    g = jnp.dot(h, wg_ref[...], preferred_element_type=F32)
    u = jnp.dot(h, wu_ref[...], preferred_element_type=F32)
    act = (g * jax.nn.sigmoid(g) * u).astype(BF16)
    o_ref[...] += jnp.dot(act, wd_ref[...], preferred_element_type=F32)


def _ffn(h2, x1, w_gu, w_down, *, tm=512, tf=512):
    n, d = x1.shape
    dff = w_down.shape[0]
    nj = dff // tf
    return pl.pallas_call(
        _ffn_kernel,
        grid=(n // tm, nj),
        in_specs=[
            pl.BlockSpec((tm, d), lambda i, j: (i, 0)),
            pl.BlockSpec((tm, d), lambda i, j: (i, 0)),
            pl.BlockSpec((d, tf), lambda i, j: (0, j)),
            pl.BlockSpec((d, tf), lambda i, j: (0, j + nj)),
            pl.BlockSpec((tf, d), lambda i, j: (j, 0)),
        ],
        out_specs=pl.BlockSpec((tm, d), lambda i, j: (i, 0)),
        out_shape=jax.ShapeDtypeStruct((n, d), F32),
        compiler_params=_params("arbitrary", "arbitrary"),
        name="ffn",
    )(h2, x1, w_gu, w_gu, w_down)


def _ple_kernel(x_ref, p_ref, wg_ref, wp_ref, gain_ref, o_ref, *, tc, final_norm):
    d = x_ref.shape[1]
    xb = x_ref[...].astype(BF16)
    pb = p_ref[...].astype(BF16)
    ssq = jnp.zeros((x_ref.shape[0], 1), F32)
    for cb in range(d // tc):
        cs = slice(cb * tc, (cb + 1) * tc)
        gate = jax.nn.sigmoid(jnp.dot(xb, wg_ref[:, cs], preferred_element_type=F32))
        pe = jnp.dot(pb, wp_ref[:, cs], preferred_element_type=F32)
        x3 = x_ref[:, cs] + gate * pe
        o_ref[:, cs] = x3
        ssq = ssq + jnp.sum(x3 * x3, axis=-1, keepdims=True)
    if not final_norm:
        return
    inv = lax.rsqrt(ssq * (1.0 / d) + NORM_EPS)
    for cb in range(d // tc):
        cs = slice(cb * tc, (cb + 1) * tc)
        o_ref[:, cs] = o_ref[:, cs] * inv * gain_ref[:, cs]


def _ple(x2, p2d, wg, wp, gain, *, final_norm, tm=512, tc=512):
    n, d = x2.shape
    row = lambda i: (i, 0)
    return pl.pallas_call(
        functools.partial(_ple_kernel, tc=tc, final_norm=final_norm),
        grid=(n // tm,),
        in_specs=[
            pl.BlockSpec((tm, d), row),
            pl.BlockSpec((tm, p2d.shape[1]), row),
            _resident(wg.shape), _resident(wp.shape), _resident(gain.shape),
        ],
        out_specs=pl.BlockSpec((tm, d), row),
        out_shape=jax.ShapeDtypeStruct((n, d), F32),
        compiler_params=_params("arbitrary"),
        name="ple",
    )(x2, p2d, wg, wp, gain)


def kernel(x, p, norm_mix, w_in, hgrn_norm, w_proj_a, w_proj_b, w_out, norm_ffn, w_gate_up, w_down, w_ple,
           w_ple_gate, rel_bias, hgrn_lb_logits, norm_final):
    bsz, seq, d = x.shape
    n = bsz * seq
    depth = w_in.shape[0]
    a_heads = A_WIDTH // HEAD_DIM
    b_heads = B_WIDTH // HEAD_DIM

    cscale = np.ones((1, IN_WIDTH), np.float32)
    cscale[:, COL_QA * SEG:(COL_QA + 1) * SEG] = LOG2E / math.sqrt(HEAD_DIM)
    cscale = jnp.asarray(cscale)

    xc = x.reshape(n, d)
    for i in range(depth):
        proj, fb = _in_proj(xc, norm_mix[i][None], w_in[i].astype(BF16), cscale)
        proj3 = proj.reshape(bsz, seq, IN_WIDTH)
        ya = _moba(proj3, rel_bias, heads=a_heads)
        yb = _hgrn(proj3, fb.reshape(bsz, seq, SEG), hgrn_lb_logits, hgrn_norm[i][None], heads=b_heads, layer=i)
        x1, h2 = _mix(ya.reshape(n, A_WIDTH), yb.reshape(n, B_WIDTH), proj, xc,
                      w_proj_a[i].astype(BF16), w_proj_b[i].astype(BF16), w_out[i].astype(BF16), norm_ffn[i][None])
        x2 = _ffn(h2, x1, w_gate_up[i].astype(BF16), w_down[i].astype(BF16))
        xc = _ple(x2, p[i].reshape(n, -1), w_ple_gate[i].astype(BF16), w_ple[i].astype(BF16), norm_final[None],
                  final_norm=(i == depth - 1))
    return xc.reshape(bsz, seq, d)
```

```python
import functools
import math

import jax
import jax.numpy as jnp
import numpy as np
from jax import lax
from jax.experimental import pallas as pl
from jax.experimental.pallas import tpu as pltpu

F32 = jnp.float32
BF16 = jnp.bfloat16

D_MODEL = 2048
HEAD_DIM = 128
A_WIDTH = 1024
B_WIDTH = 1024
MOBA_BLOCK = 256
MOBA_TOPK = 3
REL_BUCKETS = 32
REL_MAX_EXACT = 16
REL_MAX_DIST = 128
HGRN_CHUNK = 64
D_FF = 5632
NORM_EPS = 1e-6

COL_QA, COL_KA, COL_VA, COL_QB, COL_FB, COL_IB, COL_GB, COL_GATE_A, COL_GATE_B = 0, 1, 2, 3, 4, 5, 6, 7, 9
IN_WIDTH = 11 * 1024
SEG = 1024

MASK_VALUE = -1e30
VMEM_LIMIT = 56 * 1024 * 1024

NT_DIMS = (((1,), (1,)), ((), ()))
LOG2E = math.log2(math.e)


def _bucket_thresholds():
    n = np.arange(0, 4096, dtype=np.int64)
    nf = np.maximum(n, REL_MAX_EXACT).astype(np.float64)
    large = REL_MAX_EXACT + (np.log(nf / REL_MAX_EXACT) / math.log(REL_MAX_DIST / REL_MAX_EXACT)
                             * (REL_BUCKETS - REL_MAX_EXACT)).astype(np.int64)
    large = np.minimum(large, REL_BUCKETS - 1)
    bucket = np.where(n < REL_MAX_EXACT, n, large)
    return [int(np.argmax(bucket >= k)) for k in range(REL_BUCKETS)]


BUCKET_LO = _bucket_thresholds()


def _params(*sem):
    return pltpu.CompilerParams(dimension_semantics=sem, vmem_limit_bytes=VMEM_LIMIT)


def _resident(shape):
    return pl.BlockSpec(shape, lambda *_: (0,) * len(shape), pipeline_mode=pl.Buffered(1))


def _in_proj_kernel(x_ref, gain_ref, w_ref, cscale_ref, o_ref, fb_ref, h_scr, *, rows_per_step):
    j = pl.program_id(1)

    @pl.when(j == 0)
    def _():
        def body(r, _):
            rs = pl.ds(pl.multiple_of(r * rows_per_step, rows_per_step), rows_per_step)
            x = x_ref[rs, :]
            ms = jnp.mean(x * x, axis=-1, keepdims=True)
            h_scr[rs, :] = (x * lax.rsqrt(ms + NORM_EPS) * gain_ref[...]).astype(BF16)
            return 0
        lax.fori_loop(0, x_ref.shape[0] // rows_per_step, body, 0)

    acc = jnp.dot(h_scr[...], w_ref[...], preferred_element_type=F32)
    o_ref[...] = (acc * cscale_ref[...]).astype(BF16)
    pltpu.store(fb_ref, acc, mask=jnp.broadcast_to(j == COL_FB, acc.shape))


def _in_proj(x2d, gain, w_bf16, cscale, *, tm=1024, tn=SEG):
    n, d = x2d.shape
    width = w_bf16.shape[1]
    return pl.pallas_call(
        functools.partial(_in_proj_kernel, rows_per_step=128),
        grid=(n // tm, width // tn),
        in_specs=[
            pl.BlockSpec((tm, d), lambda i, j: (i, 0)),
            pl.BlockSpec((1, d), lambda i, j: (0, 0)),
            pl.BlockSpec((d, tn), lambda i, j: (0, j)),
            pl.BlockSpec((1, tn), lambda i, j: (0, j)),
        ],
        out_specs=[
            pl.BlockSpec((tm, tn), lambda i, j: (i, j)),
            pl.BlockSpec((tm, SEG), lambda i, j: (i, 0)),
        ],
        out_shape=[
            jax.ShapeDtypeStruct((n, width), BF16),
            jax.ShapeDtypeStruct((n, SEG), F32),
        ],
        scratch_shapes=[pltpu.VMEM((tm, d), BF16)],
        compiler_params=_params("arbitrary", "arbitrary"),
        name="in_proj",
    )(x2d, gain, w_bf16, cscale)


def _moba_kernel(rb_ref, qall_ref, q_ref, k_ref, v_ref, o_ref, bias_scr, vt_scr, sel_scr, s_scr, p_scr, *, n_blocks, group):
    hg = pl.program_id(0)
    b = pl.program_id(1)
    qi = pl.program_id(2)
    blk = MOBA_BLOCK
    dh = HEAD_DIM
    sub = 8

    @pl.when((b == 0) & (qi == 0))
    def _build_bias():
        r = lax.broadcasted_iota(jnp.int32, (blk, blk), 1)
        c = lax.broadcasted_iota(jnp.int32, (blk, blk), 0)
        for g in range(group):
            hd = hg * group + g
            for d in range(2):
                rel = d * blk + r - c
                bias = jnp.full((blk, blk), rb_ref[REL_BUCKETS - 1, hd] * LOG2E, F32)
                for kk in range(REL_BUCKETS - 2, -1, -1):
                    bias = jnp.where(rel < BUCKET_LO[kk + 1], rb_ref[kk, hd] * LOG2E, bias)
                if d == 0:
                    bias = jnp.where(rel < 0, MASK_VALUE, bias)
                bias_scr[g, d] = bias

    @pl.when(qi == 0)
    def _per_sequence():
        for g in range(group):
            hs = slice(g * dh, (g + 1) * dh)
            means = []
            for jb in range(n_blocks):
                rs = slice(jb * blk, (jb + 1) * blk)
                means.append(jnp.mean(k_ref[0, rs, hs].astype(F32), axis=0, keepdims=True))
                vt_scr[g, :, rs] = v_ref[0, rs, hs].astype(F32).T.astype(BF16)
            km = jnp.concatenate(means + [jnp.zeros((16 - n_blocks, dh), F32)], axis=0)

            km_hi = km.astype(BF16)
            km_lo = (km - km_hi.astype(F32)).astype(BF16)
            qall = qall_ref[0, :, hs]
            sc = (lax.dot_general(km_hi, qall, NT_DIMS, preferred_element_type=F32)
                  + lax.dot_general(km_lo, qall, NT_DIMS, preferred_element_type=F32))
            rows = lax.broadcasted_iota(jnp.int32, sc.shape, 0)
            rows_f = rows.astype(F32)
            past = rows < lax.broadcasted_iota(jnp.int32, sc.shape, 1) // blk
            s = jnp.where(past, sc, -jnp.inf)
            picked = jnp.zeros(sc.shape, jnp.bool_)
            for _ in range(MOBA_TOPK):
                top = jnp.max(s, axis=0, keepdims=True)
                first = jnp.min(jnp.where(s == top, rows_f, 1e9), axis=0, keepdims=True)
                hit = rows_f == first
                picked = picked | hit
                s = jnp.where(hit, -jnp.inf, s)
            mask = jnp.where(picked & past, 0.0, MASK_VALUE)
            for qb in range(n_blocks):
                sel_scr[g, qb] = mask[:, qb * blk:(qb + 1) * blk]

    def attend(nt):
        hss = [slice(g * dh, (g + 1) * dh) for g in range(group)]
        ms = []
        for g in range(group):
            far = rb_ref[REL_BUCKETS - 1, hg * group + g] * LOG2E
            s = lax.dot_general(k_ref[0, :nt * blk, hss[g]], q_ref[0, :, hss[g]], NT_DIMS,
                                preferred_element_type=F32)
            m8 = None
            for j in range(nt):
                rs = slice(j * blk, (j + 1) * blk)
                d = nt - 1 - j
                if d == 0:
                    sj = s[rs] + bias_scr[g, 0]
                elif d == 1:
                    sj = s[rs] + bias_scr[g, 1] + sel_scr[g, qi, j:j + 1, :]
                else:
                    sj = s[rs] + (sel_scr[g, qi, j:j + 1, :] + far)
                s_scr[g, rs, :] = sj
                mj = jnp.max(sj.reshape(blk // sub, sub, blk), axis=0)
                m8 = mj if m8 is None else jnp.maximum(m8, mj)
            ms.append(jnp.max(m8, axis=0, keepdims=True))
        ls = []
        for g in range(group):
            l8 = jnp.zeros((sub, blk), F32)
            for j in range(nt):
                rs = slice(j * blk, (j + 1) * blk)
                p = jnp.exp2(s_scr[g, rs, :] - ms[g])
                l8 = l8 + jnp.sum(p.reshape(blk // sub, sub, blk), axis=0)
                p_scr[g, rs, :] = p.astype(BF16)
            ls.append(jnp.sum(l8, axis=0, keepdims=True))
        for g in range(group):
            acc = jnp.dot(vt_scr[g, :, :nt * blk], p_scr[g, :nt * blk, :], preferred_element_type=F32)
            o_ref[0, :, hss[g]] = (acc / ls[g]).T.astype(o_ref.dtype)

    for nt in range(1, n_blocks + 1):
        pl.when(qi == nt - 1)(functools.partial(attend, nt))


def _moba(proj3, rel_bias, *, heads, group=4):
    bsz, seq, _ = proj3.shape
    nb = seq // MOBA_BLOCK
    gw = group * HEAD_DIM
    cq, ck, cv = (c * (SEG // gw) for c in (COL_QA, COL_KA, COL_VA))
    return pl.pallas_call(
        functools.partial(_moba_kernel, n_blocks=nb, group=group),
        grid=(heads // group, bsz, nb),
        in_specs=[
            pl.BlockSpec(memory_space=pltpu.SMEM),
            pl.BlockSpec((1, seq, gw), lambda h, b, i: (b, 0, cq + h)),
            pl.BlockSpec((1, MOBA_BLOCK, gw), lambda h, b, i: (b, i, cq + h)),
            pl.BlockSpec((1, seq, gw), lambda h, b, i: (b, 0, ck + h)),
            pl.BlockSpec((1, seq, gw), lambda h, b, i: (b, 0, cv + h)),
        ],
        out_specs=pl.BlockSpec((1, MOBA_BLOCK, gw), lambda h, b, i: (b, i, h)),
        out_shape=jax.ShapeDtypeStruct((bsz, seq, heads * HEAD_DIM), BF16),
        scratch_shapes=[
            pltpu.VMEM((group, 2, MOBA_BLOCK, MOBA_BLOCK), F32),
            pltpu.VMEM((group, HEAD_DIM, seq), BF16),
            pltpu.VMEM((group, nb, 16, MOBA_BLOCK), F32),
            pltpu.VMEM((group, seq, MOBA_BLOCK), F32),
            pltpu.VMEM((group, seq, MOBA_BLOCK), BF16),
        ],
        compiler_params=_params("arbitrary", "arbitrary", "arbitrary"),
        name="moba",
    )(rel_bias, proj3, proj3, proj3, proj3)


TN_DIMS = (((0,), (0,)), ((), ()))


def _cumsum_rows(x, period):
    pos = lax.broadcasted_iota(jnp.int32, x.shape, 0) % period
    sh = 1
    while sh < period:
        x = x + jnp.where(pos >= sh, pltpu.roll(x, sh, axis=0), 0.0)
        sh *= 2
    return x


def _hgrn_kernel(lbl_ref, gain_ref, q_ref, f_ref, i_ref, g_ref, o_ref, st_scr, *, heads, layer, chunks):
    ch = HGRN_CHUNK
    dh = HEAD_DIM
    rows = chunks * ch

    @pl.when(pl.program_id(1) == 0)
    def _():
        st_scr[...] = jnp.zeros_like(st_scr)

    lg = lbl_ref[...]
    e = jnp.exp(lg - jnp.max(lg, axis=0, keepdims=True))
    lb = jnp.sum(e[:layer + 1], axis=0, keepdims=True) / jnp.sum(e, axis=0, keepdims=True)

    f = lb + (1.0 - lb) * jax.nn.sigmoid(f_ref[0])
    bcum = _cumsum_rows(jnp.log(f), ch)
    q_dec = (q_ref[0].astype(F32) * jnp.exp(bcum)).astype(BF16)
    k_dec32 = (1.0 - f) * jnp.exp(-bcum)
    k_dec = k_dec32.astype(BF16)
    decays = [jnp.exp(bcum[(c + 1) * ch - 1:(c + 1) * ch, :]) for c in range(chunks)]
    k_end = [(k_dec32[c * ch:(c + 1) * ch, :] * decays[c]).astype(BF16) for c in range(chunks)]
    v = i_ref[0]
    gate = g_ref[0].astype(F32)
    gate = gate * jax.nn.sigmoid(gate)
    gain = gain_ref[...]

    r = lax.broadcasted_iota(jnp.int32, (rows, rows), 0)
    c_ = lax.broadcasted_iota(jnp.int32, (rows, rows), 1)
    causal_in_chunk = (r >= c_) & (r // ch == c_ // ch)

    for hd in range(heads):
        sl = slice(hd * dh, (hd + 1) * dh)
        qd, vh = q_dec[:, sl], v[:, sl]
        a = lax.dot_general(qd, k_dec[:, sl], NT_DIMS, preferred_element_type=F32)
        a = jnp.where(causal_in_chunk, a, 0.0).astype(BF16)
        o_intra = jnp.dot(a, vh, preferred_element_type=F32)
        st = st_scr[hd]
        outs = []
        for c in range(chunks):
            rs = slice(c * ch, (c + 1) * ch)
            outs.append(o_intra[rs] + lax.dot_general(qd[rs], st.astype(BF16), NT_DIMS, preferred_element_type=F32))
            st = st * decays[c][:, sl] + lax.dot_general(vh[rs], k_end[c][:, sl], TN_DIMS,
                                                         preferred_element_type=F32)
        st_scr[hd] = st
        o = jnp.concatenate(outs, axis=0)
        ms = jnp.mean(o * o, axis=-1, keepdims=True)
        y = o * lax.rsqrt(ms + NORM_EPS) * gain[:, sl]
        o_ref[0, :, sl] = (y * gate[:, sl]).astype(o_ref.dtype)


def _hgrn(proj3, fb3, lb_logits, gain, *, heads, layer, chunks=4):
    bsz, seq, _ = proj3.shape
    width = heads * HEAD_DIM
    rows = chunks * HGRN_CHUNK
    blk = (1, rows, width)
    return pl.pallas_call(
        functools.partial(_hgrn_kernel, heads=heads, layer=layer, chunks=chunks),
        grid=(bsz, seq // rows),
        in_specs=[
            pl.BlockSpec(lb_logits.shape, lambda b, c: (0, 0)),
            pl.BlockSpec((1, width), lambda b, c: (0, 0)),
            pl.BlockSpec(blk, lambda b, c: (b, c, COL_QB)),
            pl.BlockSpec(blk, lambda b, c: (b, c, 0)),
            pl.BlockSpec(blk, lambda b, c: (b, c, COL_IB)),
            pl.BlockSpec(blk, lambda b, c: (b, c, COL_GB)),
        ],
        out_specs=pl.BlockSpec(blk, lambda b, c: (b, c, 0)),
        out_shape=jax.ShapeDtypeStruct((bsz, seq, width), BF16),
        scratch_shapes=[pltpu.VMEM((heads, HEAD_DIM, HEAD_DIM), F32)],
        compiler_params=_params("arbitrary", "arbitrary"),
        name="hgrn2",
    )(lb_logits, gain, proj3, fb3, proj3, proj3)


def _mix_kernel(ya_ref, yb_ref, ga0_ref, ga1_ref, gb0_ref, gb1_ref, x_ref, wa_ref, wb_ref, wo_ref, gain_ref,
                x1_ref, h2_ref, mixed_scr, *, tc):
    d = x_ref.shape[1]
    gate_refs = ((ga0_ref, gb0_ref), (ga1_ref, gb1_ref))
    ya = ya_ref[...]
    yb = yb_ref[...]
    for cb in range(d // tc):
        cs = slice(cb * tc, (cb + 1) * tc)
        ga_ref, gb_ref = gate_refs[(cb * tc) // SEG]
        gs = slice((cb * tc) % SEG, (cb * tc) % SEG + tc)
        pa = jnp.dot(ya, wa_ref[:, cs], preferred_element_type=F32)
        pb = jnp.dot(yb, wb_ref[:, cs], preferred_element_type=F32)
        mixed = (jax.nn.sigmoid(ga_ref[:, gs].astype(F32)) * pa
                 + jax.nn.sigmoid(gb_ref[:, gs].astype(F32)) * pb)
        mixed_scr[:, cs] = mixed.astype(BF16)
    ssq = jnp.zeros((x_ref.shape[0], 1), F32)
    for cb in range(d // tc):
        cs = slice(cb * tc, (cb + 1) * tc)
        x1 = x_ref[:, cs] + jnp.dot(mixed_scr[...], wo_ref[:, cs], preferred_element_type=F32)
        x1_ref[:, cs] = x1
        ssq = ssq + jnp.sum(x1 * x1, axis=-1, keepdims=True)
    inv = lax.rsqrt(ssq * (1.0 / d) + NORM_EPS)
    for cb in range(d // tc):
        cs = slice(cb * tc, (cb + 1) * tc)
        h2_ref[:, cs] = (x1_ref[:, cs] * inv * gain_ref[:, cs]).astype(BF16)


def _mix(ya, yb, proj, x2d, wa, wb, wo, gain, *, tm=512, tc=512):
    n, d = x2d.shape
    row = lambda i: (i, 0)
    seg = lambda c: pl.BlockSpec((tm, SEG), lambda i: (i, c))
    return pl.pallas_call(
        functools.partial(_mix_kernel, tc=tc),
        grid=(n // tm,),
        in_specs=[
            pl.BlockSpec((tm, A_WIDTH), row),
            pl.BlockSpec((tm, B_WIDTH), row),
            seg(COL_GATE_A), seg(COL_GATE_A + 1), seg(COL_GATE_B), seg(COL_GATE_B + 1),
            pl.BlockSpec((tm, d), row),
            _resident(wa.shape), _resident(wb.shape), _resident(wo.shape), _resident(gain.shape),
        ],
        out_specs=[pl.BlockSpec((tm, d), row), pl.BlockSpec((tm, d), row)],
        out_shape=[jax.ShapeDtypeStruct((n, d), F32), jax.ShapeDtypeStruct((n, d), BF16)],
        scratch_shapes=[pltpu.VMEM((tm, d), BF16)],
        compiler_params=_params("arbitrary"),
        name="mix",
    )(ya, yb, proj, proj, proj, proj, x2d, wa, wb, wo, gain)


def _ffn_kernel(h_ref, wg_ref, wu_ref, wd_ref, o_ref):
    @pl.when(pl.program_id(1) == 0)
    def _():
        o_ref[...] = jnp.zeros_like(o_ref)

    h = h_ref[...]
    g = jnp.dot(h, wg_ref[...], preferred_element_type=F32)
    u = jnp.dot(h, wu_ref[...], preferred_element_type=F32)
    act = (g * jax.nn.sigmoid(g) * u).astype(BF16)
    o_ref[...] += jnp.dot(act, wd_ref[...], preferred_element_type=F32)


def _ffn(h2, w_gu, w_down, *, tm=1024, tf=512):
    n, d = h2.shape
    dff = w_down.shape[0]
    nj = dff // tf
    return pl.pallas_call(
        _ffn_kernel,
        grid=(n // tm, nj),
        in_specs=[
            pl.BlockSpec((tm, d), lambda i, j: (i, 0)),
            pl.BlockSpec((d, tf), lambda i, j: (0, j)),
            pl.BlockSpec((d, tf), lambda i, j: (0, j + nj)),
            pl.BlockSpec((tf, d), lambda i, j: (j, 0)),
        ],
        out_specs=pl.BlockSpec((tm, d), lambda i, j: (i, 0)),
        out_shape=jax.ShapeDtypeStruct((n, d), F32),
        compiler_params=_params("arbitrary", "arbitrary"),
        name="ffn",
    )(h2, w_gu, w_gu, w_down)


def _ple_kernel(x1_ref, y_ref, p_ref, wg_ref, wp_ref, gain_ref, o_ref, x2_scr, *, tc, final_norm):
    d = x1_ref.shape[1]
    x2_scr[...] = x1_ref[...] + y_ref[...]
    xb = x2_scr[...].astype(BF16)
    pb = p_ref[...].astype(BF16)
    ssq = jnp.zeros((x1_ref.shape[0], 1), F32)
    for cb in range(d // tc):
        cs = slice(cb * tc, (cb + 1) * tc)
        gate = jax.nn.sigmoid(jnp.dot(xb, wg_ref[:, cs], preferred_element_type=F32))
        pe = jnp.dot(pb, wp_ref[:, cs], preferred_element_type=F32)
        x3 = x2_scr[:, cs] + gate * pe
        o_ref[:, cs] = x3
        ssq = ssq + jnp.sum(x3 * x3, axis=-1, keepdims=True)
    if not final_norm:
        return
    inv = lax.rsqrt(ssq * (1.0 / d) + NORM_EPS)
    for cb in range(d // tc):
        cs = slice(cb * tc, (cb + 1) * tc)
        o_ref[:, cs] = o_ref[:, cs] * inv * gain_ref[:, cs]


def _ple(x1, y, p2d, wg, wp, gain, *, final_norm, tm=512, tc=512):
    n, d = x1.shape
    row = lambda i: (i, 0)
    return pl.pallas_call(
        functools.partial(_ple_kernel, tc=tc, final_norm=final_norm),
        grid=(n // tm,),
        in_specs=[
            pl.BlockSpec((tm, d), row),
            pl.BlockSpec((tm, d), row),
            pl.BlockSpec((tm, p2d.shape[1]), row),
            _resident(wg.shape), _resident(wp.shape), _resident(gain.shape),
        ],
        out_specs=pl.BlockSpec((tm, d), row),
        out_shape=jax.ShapeDtypeStruct((n, d), F32),
        scratch_shapes=[pltpu.VMEM((tm, d), F32)],
        compiler_params=_params("arbitrary"),
        name="ple",
    )(x1, y, p2d, wg, wp, gain)


def kernel(x, p, norm_mix, w_in, hgrn_norm, w_proj_a, w_proj_b, w_out, norm_ffn, w_gate_up, w_down, w_ple,
           w_ple_gate, rel_bias, hgrn_lb_logits, norm_final):
    bsz, seq, d = x.shape
    n = bsz * seq
    depth = w_in.shape[0]
    a_heads = A_WIDTH // HEAD_DIM
    b_heads = B_WIDTH // HEAD_DIM

    cscale = np.ones((1, IN_WIDTH), np.float32)
    cscale[:, COL_QA * SEG:(COL_QA + 1) * SEG] = LOG2E / math.sqrt(HEAD_DIM)
    cscale = jnp.asarray(cscale)

    xc = x.reshape(n, d)
    for i in range(depth):
        proj, fb = _in_proj(xc, norm_mix[i][None], w_in[i].astype(BF16), cscale)
        proj3 = proj.reshape(bsz, seq, IN_WIDTH)
        ya = _moba(proj3, rel_bias, heads=a_heads)
        yb = _hgrn(proj3, fb.reshape(bsz, seq, SEG), hgrn_lb_logits, hgrn_norm[i][None], heads=b_heads, layer=i)
        x1, h2 = _mix(ya.reshape(n, A_WIDTH), yb.reshape(n, B_WIDTH), proj, xc,
                      w_proj_a[i].astype(BF16), w_proj_b[i].astype(BF16), w_out[i].astype(BF16), norm_ffn[i][None])
        y = _ffn(h2, w_gate_up[i].astype(BF16), w_down[i].astype(BF16))
        xc = _ple(x1, y, p[i].reshape(n, -1), w_ple_gate[i].astype(BF16), w_ple[i].astype(BF16), norm_final[None],
                  final_norm=(i == depth - 1))
    return xc.reshape(bsz, seq, d)
```

```python
import functools
import math

import jax
import jax.numpy as jnp
import numpy as np
from jax import lax
from jax.experimental import pallas as pl
from jax.experimental.pallas import tpu as pltpu

F32 = jnp.float32
BF16 = jnp.bfloat16

D_MODEL = 2048
HEAD_DIM = 128
A_WIDTH = 1024
B_WIDTH = 1024
MOBA_BLOCK = 256
MOBA_TOPK = 3
REL_BUCKETS = 32
REL_MAX_EXACT = 16
REL_MAX_DIST = 128
HGRN_CHUNK = 64
D_FF = 5632
NORM_EPS = 1e-6

COL_QA, COL_KA, COL_VA, COL_QB, COL_FB, COL_IB, COL_GB, COL_GATE_A, COL_GATE_B = 0, 1, 2, 3, 4, 5, 6, 7, 9
IN_WIDTH = 11 * 1024
SEG = 1024

BF16_SUBLANES = 16
MASK_VALUE = -1e30
VMEM_LIMIT = 56 * 1024 * 1024

NT_DIMS = (((1,), (1,)), ((), ()))
LOG2E = math.log2(math.e)


def _bucket_thresholds():
    n = np.arange(0, 4096, dtype=np.int64)
    nf = np.maximum(n, REL_MAX_EXACT).astype(np.float64)
    large = REL_MAX_EXACT + (np.log(nf / REL_MAX_EXACT) / math.log(REL_MAX_DIST / REL_MAX_EXACT)
                             * (REL_BUCKETS - REL_MAX_EXACT)).astype(np.int64)
    large = np.minimum(large, REL_BUCKETS - 1)
    bucket = np.where(n < REL_MAX_EXACT, n, large)
    return [int(np.argmax(bucket >= k)) for k in range(REL_BUCKETS)]


BUCKET_LO = _bucket_thresholds()


def _params(*sem):
    return pltpu.CompilerParams(dimension_semantics=sem, vmem_limit_bytes=VMEM_LIMIT)


def _resident(shape):
    return pl.BlockSpec(shape, lambda *_: (0,) * len(shape), pipeline_mode=pl.Buffered(1))


def _in_proj_kernel(x_ref, gain_ref, w_ref, cscale_ref, o_ref, fb_ref, h_scr, *, rows_per_step):
    j = pl.program_id(1)

    @pl.when(j == 0)
    def _():
        def body(r, _):
            rs = pl.ds(pl.multiple_of(r * rows_per_step, rows_per_step), rows_per_step)
            x = x_ref[rs, :]
            ms = jnp.mean(x * x, axis=-1, keepdims=True)
            h_scr[rs, :] = (x * lax.rsqrt(ms + NORM_EPS) * gain_ref[...]).astype(BF16)
            return 0
        lax.fori_loop(0, x_ref.shape[0] // rows_per_step, body, 0)

    acc = jnp.dot(h_scr[...], w_ref[...], preferred_element_type=F32)
    o_ref[...] = (acc * cscale_ref[...]).astype(BF16)
    pltpu.store(fb_ref, acc, mask=jnp.broadcast_to(j == COL_FB, acc.shape))


def _in_proj(x2d, gain, w_bf16, cscale, *, tm=1024, tn=SEG):
    n, d = x2d.shape
    width = w_bf16.shape[1]
    return pl.pallas_call(
        functools.partial(_in_proj_kernel, rows_per_step=128),
        grid=(n // tm, width // tn),
        in_specs=[
            pl.BlockSpec((tm, d), lambda i, j: (i, 0)),
            pl.BlockSpec((1, d), lambda i, j: (0, 0)),
            pl.BlockSpec((d, tn), lambda i, j: (0, j)),
            pl.BlockSpec((1, tn), lambda i, j: (0, j)),
        ],
        out_specs=[
            pl.BlockSpec((tm, tn), lambda i, j: (i, j)),
            pl.BlockSpec((tm, SEG), lambda i, j: (i, 0)),
        ],
        out_shape=[
            jax.ShapeDtypeStruct((n, width), BF16),
            jax.ShapeDtypeStruct((n, SEG), F32),
        ],
        scratch_shapes=[pltpu.VMEM((tm, d), BF16)],
        compiler_params=_params("arbitrary", "arbitrary"),
        name="in_proj",
    )(x2d, gain, w_bf16, cscale)


def _moba_kernel(rb_ref, qall_ref, q_ref, k_ref, v_ref, *rest, n_blocks, group, n_cast):
    w32_refs = rest[:n_cast]
    o_ref = rest[n_cast]
    w16_refs = rest[n_cast + 1:2 * n_cast + 1]
    bias_scr, vt_scr, sel_scr, s_scr, p_scr = rest[2 * n_cast + 1:]
    hg = pl.program_id(0)
    b = pl.program_id(1)
    qi = pl.program_id(2)
    blk = MOBA_BLOCK
    dh = HEAD_DIM
    sub = 8

    @pl.when((b == 0) & (qi == 0))
    def _build_bias():
        r = lax.broadcasted_iota(jnp.int32, (blk, blk), 1)
        c = lax.broadcasted_iota(jnp.int32, (blk, blk), 0)
        for g in range(group):
            hd = hg * group + g
            for d in range(2):
                rel = d * blk + r - c
                bias = jnp.full((blk, blk), rb_ref[REL_BUCKETS - 1, hd] * LOG2E, F32)
                for kk in range(REL_BUCKETS - 2, -1, -1):
                    bias = jnp.where(rel < BUCKET_LO[kk + 1], rb_ref[kk, hd] * LOG2E, bias)
                if d == 0:
                    bias = jnp.where(rel < 0, MASK_VALUE, bias)
                bias_scr[g, d] = bias

    @pl.when(qi == 0)
    def _per_sequence():
        for g in range(group):
            hs = slice(g * dh, (g + 1) * dh)
            means = []
            for jb in range(n_blocks):
                rs = slice(jb * blk, (jb + 1) * blk)
                means.append(jnp.mean(k_ref[0, rs, hs].astype(F32), axis=0, keepdims=True))
                vt_scr[g, :, rs] = v_ref[0, rs, hs].astype(F32).T.astype(BF16)
            km = jnp.concatenate(means + [jnp.zeros((16 - n_blocks, dh), F32)], axis=0)

            km_hi = km.astype(BF16)
            km_lo = (km - km_hi.astype(F32)).astype(BF16)
            qall = qall_ref[0, :, hs]
            sc = (lax.dot_general(km_hi, qall, NT_DIMS, preferred_element_type=F32)
                  + lax.dot_general(km_lo, qall, NT_DIMS, preferred_element_type=F32))
            rows = lax.broadcasted_iota(jnp.int32, sc.shape, 0)
            rows_f = rows.astype(F32)
            past = rows < lax.broadcasted_iota(jnp.int32, sc.shape, 1) // blk
            s = jnp.where(past, sc, -jnp.inf)
            picked = jnp.zeros(sc.shape, jnp.bool_)
            for _ in range(MOBA_TOPK):
                top = jnp.max(s, axis=0, keepdims=True)
                first = jnp.min(jnp.where(s == top, rows_f, 1e9), axis=0, keepdims=True)
                hit = rows_f == first
                picked = picked | hit
                s = jnp.where(hit, -jnp.inf, s)
            mask = jnp.where(picked & past, 0.0, MASK_VALUE)
            for qb in range(n_blocks):
                sel_scr[g, qb] = mask[:, qb * blk:(qb + 1) * blk]

    def attend(nt):
        for w32, w16 in zip(w32_refs, w16_refs):
            w16[...] = w32[...].astype(BF16)
        hss = [slice(g * dh, (g + 1) * dh) for g in range(group)]
        ms = []
        for g in range(group):
            far = rb_ref[REL_BUCKETS - 1, hg * group + g] * LOG2E
            s = lax.dot_general(k_ref[0, :nt * blk, hss[g]], q_ref[0, :, hss[g]], NT_DIMS,
                                preferred_element_type=F32)
            m8 = None
            for j in range(nt):
                rs = slice(j * blk, (j + 1) * blk)
                d = nt - 1 - j
                if d == 0:
                    sj = s[rs] + bias_scr[g, 0]
                elif d == 1:
                    sj = s[rs] + bias_scr[g, 1] + sel_scr[g, qi, j:j + 1, :]
                else:
                    sj = s[rs] + (sel_scr[g, qi, j:j + 1, :] + far)
                s_scr[g, rs, :] = sj
                mj = jnp.max(sj.reshape(blk // sub, sub, blk), axis=0)
                m8 = mj if m8 is None else jnp.maximum(m8, mj)
            ms.append(jnp.max(m8, axis=0, keepdims=True))
        ls = []
        for g in range(group):
            l8 = jnp.zeros((sub, blk), F32)
            for j in range(nt):
                rs = slice(j * blk, (j + 1) * blk)
                p = jnp.exp2(s_scr[g, rs, :] - ms[g])
                l8 = l8 + jnp.sum(p.reshape(blk // sub, sub, blk), axis=0)
                p_scr[g, rs, :] = p.astype(BF16)
            ls.append(jnp.sum(l8, axis=0, keepdims=True))
        for g in range(group):
            acc = jnp.dot(vt_scr[g, :, :nt * blk], p_scr[g, :nt * blk, :], preferred_element_type=F32)
            o_ref[0, :, hss[g]] = (acc / ls[g]).T.astype(o_ref.dtype)

    for nt in range(1, n_blocks + 1):
        pl.when(qi == nt - 1)(functools.partial(attend, nt))


def _cast_rows(n_rows, n_steps):
    r = BF16_SUBLANES
    while n_rows % r or n_rows // r > n_steps:
        r += BF16_SUBLANES
    return r


def _moba(proj3, rel_bias, cast_weights, *, heads, group=4):
    bsz, seq, _ = proj3.shape
    nb = seq // MOBA_BLOCK
    gw = group * HEAD_DIM
    cq, ck, cv = (c * (SEG // gw) for c in (COL_QA, COL_KA, COL_VA))
    n_steps = (heads // group) * bsz * nb
    cast_specs = []
    for w in cast_weights:
        r = _cast_rows(w.shape[0], n_steps)
        last = w.shape[0] // r - 1
        cast_specs.append(pl.BlockSpec((r, w.shape[1]),
                                       lambda h, b, i, last=last: (jnp.minimum((h * bsz + b) * nb + i, last), 0)))
    outs = pl.pallas_call(
        functools.partial(_moba_kernel, n_blocks=nb, group=group, n_cast=len(cast_weights)),
        grid=(heads // group, bsz, nb),
        in_specs=[
            pl.BlockSpec(memory_space=pltpu.SMEM),
            pl.BlockSpec((1, seq, gw), lambda h, b, i: (b, 0, cq + h)),
            pl.BlockSpec((1, MOBA_BLOCK, gw), lambda h, b, i: (b, i, cq + h)),
            pl.BlockSpec((1, seq, gw), lambda h, b, i: (b, 0, ck + h)),
            pl.BlockSpec((1, seq, gw), lambda h, b, i: (b, 0, cv + h)),
            *cast_specs,
        ],
        out_specs=[pl.BlockSpec((1, MOBA_BLOCK, gw), lambda h, b, i: (b, i, h)), *cast_specs],
        out_shape=[jax.ShapeDtypeStruct((bsz, seq, heads * HEAD_DIM), BF16),
                   *(jax.ShapeDtypeStruct(w.shape, BF16) for w in cast_weights)],
        scratch_shapes=[
            pltpu.VMEM((group, 2, MOBA_BLOCK, MOBA_BLOCK), F32),
            pltpu.VMEM((group, HEAD_DIM, seq), BF16),
            pltpu.VMEM((group, nb, 16, MOBA_BLOCK), F32),
            pltpu.VMEM((group, seq, MOBA_BLOCK), F32),
            pltpu.VMEM((group, seq, MOBA_BLOCK), BF16),
        ],
        compiler_params=_params("arbitrary", "arbitrary", "arbitrary"),
        name="moba",
    )(rel_bias, proj3, proj3, proj3, proj3, *cast_weights)
    return outs[0], outs[1:]


TN_DIMS = (((0,), (0,)), ((), ()))


def _cumsum_rows(x, period):
    pos = lax.broadcasted_iota(jnp.int32, x.shape, 0) % period
    sh = 1
    while sh < period:
        x = x + jnp.where(pos >= sh, pltpu.roll(x, sh, axis=0), 0.0)
        sh *= 2
    return x


def _hgrn_kernel(lbl_ref, gain_ref, q_ref, f_ref, i_ref, g_ref, o_ref, st_scr, *, heads, layer, chunks):
    ch = HGRN_CHUNK
    dh = HEAD_DIM
    rows = chunks * ch

    @pl.when(pl.program_id(1) == 0)
    def _():
        st_scr[...] = jnp.zeros_like(st_scr)

    lg = lbl_ref[...]
    e = jnp.exp(lg - jnp.max(lg, axis=0, keepdims=True))
    lb = jnp.sum(e[:layer + 1], axis=0, keepdims=True) / jnp.sum(e, axis=0, keepdims=True)

    f = lb + (1.0 - lb) * jax.nn.sigmoid(f_ref[0])
    bcum = _cumsum_rows(jnp.log(f), ch)
    q_dec = (q_ref[0].astype(F32) * jnp.exp(bcum)).astype(BF16)
    k_dec32 = (1.0 - f) * jnp.exp(-bcum)
    k_dec = k_dec32.astype(BF16)
    decays = [jnp.exp(bcum[(c + 1) * ch - 1:(c + 1) * ch, :]) for c in range(chunks)]
    k_end = [(k_dec32[c * ch:(c + 1) * ch, :] * decays[c]).astype(BF16) for c in range(chunks)]
    v = i_ref[0]
    gate = g_ref[0].astype(F32)
    gate = gate * jax.nn.sigmoid(gate)
    gain = gain_ref[...]

    r = lax.broadcasted_iota(jnp.int32, (rows, rows), 0)
    c_ = lax.broadcasted_iota(jnp.int32, (rows, rows), 1)
    causal_in_chunk = (r >= c_) & (r // ch == c_ // ch)

    for hd in range(heads):
        sl = slice(hd * dh, (hd + 1) * dh)
        qd, vh = q_dec[:, sl], v[:, sl]
        a = lax.dot_general(qd, k_dec[:, sl], NT_DIMS, preferred_element_type=F32)
        a = jnp.where(causal_in_chunk, a, 0.0).astype(BF16)
        o_intra = jnp.dot(a, vh, preferred_element_type=F32)
        st = st_scr[hd]
        outs = []
        for c in range(chunks):
            rs = slice(c * ch, (c + 1) * ch)
            outs.append(o_intra[rs] + lax.dot_general(qd[rs], st.astype(BF16), NT_DIMS, preferred_element_type=F32))
            st = st * decays[c][:, sl] + lax.dot_general(vh[rs], k_end[c][:, sl], TN_DIMS,
                                                         preferred_element_type=F32)
        st_scr[hd] = st
        o = jnp.concatenate(outs, axis=0)
        ms = jnp.mean(o * o, axis=-1, keepdims=True)
        y = o * lax.rsqrt(ms + NORM_EPS) * gain[:, sl]
        o_ref[0, :, sl] = (y * gate[:, sl]).astype(o_ref.dtype)


def _hgrn(proj3, fb3, lb_logits, gain, *, heads, layer, chunks=4):
    bsz, seq, _ = proj3.shape
    width = heads * HEAD_DIM
    rows = chunks * HGRN_CHUNK
    blk = (1, rows, width)
    return pl.pallas_call(
        functools.partial(_hgrn_kernel, heads=heads, layer=layer, chunks=chunks),
        grid=(bsz, seq // rows),
        in_specs=[
            pl.BlockSpec(lb_logits.shape, lambda b, c: (0, 0)),
            pl.BlockSpec((1, width), lambda b, c: (0, 0)),
            pl.BlockSpec(blk, lambda b, c: (b, c, COL_QB)),
            pl.BlockSpec(blk, lambda b, c: (b, c, 0)),
            pl.BlockSpec(blk, lambda b, c: (b, c, COL_IB)),
            pl.BlockSpec(blk, lambda b, c: (b, c, COL_GB)),
        ],
        out_specs=pl.BlockSpec(blk, lambda b, c: (b, c, 0)),
        out_shape=jax.ShapeDtypeStruct((bsz, seq, width), BF16),
        scratch_shapes=[pltpu.VMEM((heads, HEAD_DIM, HEAD_DIM), F32)],
        compiler_params=_params("arbitrary", "arbitrary"),
        name="hgrn2",
    )(lb_logits, gain, proj3, fb3, proj3, proj3)


def _mix_kernel(ya_ref, yb_ref, ga0_ref, ga1_ref, gb0_ref, gb1_ref, x_ref, wa_ref, wb_ref, wo_ref, gain_ref,
                x1_ref, h2_ref, mixed_scr, *, tc):
    d = x_ref.shape[1]
    gate_refs = ((ga0_ref, gb0_ref), (ga1_ref, gb1_ref))
    ya = ya_ref[...]
    yb = yb_ref[...]
    for cb in range(d // tc):
        cs = slice(cb * tc, (cb + 1) * tc)
        ga_ref, gb_ref = gate_refs[(cb * tc) // SEG]
        gs = slice((cb * tc) % SEG, (cb * tc) % SEG + tc)
        pa = jnp.dot(ya, wa_ref[:, cs], preferred_element_type=F32)
        pb = jnp.dot(yb, wb_ref[:, cs], preferred_element_type=F32)
        mixed = (jax.nn.sigmoid(ga_ref[:, gs].astype(F32)) * pa
                 + jax.nn.sigmoid(gb_ref[:, gs].astype(F32)) * pb)
        mixed_scr[:, cs] = mixed.astype(BF16)
    ssq = jnp.zeros((x_ref.shape[0], 1), F32)
    for cb in range(d // tc):
        cs = slice(cb * tc, (cb + 1) * tc)
        x1 = x_ref[:, cs] + jnp.dot(mixed_scr[...], wo_ref[:, cs], preferred_element_type=F32)
        x1_ref[:, cs] = x1
        ssq = ssq + jnp.sum(x1 * x1, axis=-1, keepdims=True)
    inv = lax.rsqrt(ssq * (1.0 / d) + NORM_EPS)
    for cb in range(d // tc):
        cs = slice(cb * tc, (cb + 1) * tc)
        h2_ref[:, cs] = (x1_ref[:, cs] * inv * gain_ref[:, cs]).astype(BF16)


def _mix(ya, yb, proj, x2d, wa, wb, wo, gain, *, tm=512, tc=512):
    n, d = x2d.shape
    row = lambda i: (i, 0)
    seg = lambda c: pl.BlockSpec((tm, SEG), lambda i: (i, c))
    return pl.pallas_call(
        functools.partial(_mix_kernel, tc=tc),
        grid=(n // tm,),
        in_specs=[
            pl.BlockSpec((tm, A_WIDTH), row),
            pl.BlockSpec((tm, B_WIDTH), row),
            seg(COL_GATE_A), seg(COL_GATE_A + 1), seg(COL_GATE_B), seg(COL_GATE_B + 1),
            pl.BlockSpec((tm, d), row),
            _resident(wa.shape), _resident(wb.shape), _resident(wo.shape), _resident(gain.shape),
        ],
        out_specs=[pl.BlockSpec((tm, d), row), pl.BlockSpec((tm, d), row)],
        out_shape=[jax.ShapeDtypeStruct((n, d), F32), jax.ShapeDtypeStruct((n, d), BF16)],
        scratch_shapes=[pltpu.VMEM((tm, d), BF16)],
        compiler_params=_params("arbitrary"),
        name="mix",
    )(ya, yb, proj, proj, proj, proj, x2d, wa, wb, wo, gain)


def _ffn_kernel(h_ref, wg_ref, wu_ref, wd_ref, o_ref):
    @pl.when(pl.program_id(1) == 0)
    def _():
        o_ref[...] = jnp.zeros_like(o_ref)

    h = h_ref[...]
    g = jnp.dot(h, wg_ref[...], preferred_element_type=F32)
    u = jnp.dot(h, wu_ref[...], preferred_element_type=F32)
    act = (g * jax.nn.sigmoid(g) * u).astype(BF16)
    o_ref[...] += jnp.dot(act, wd_ref[...], preferred_element_type=F32)


def _ffn(h2, w_gu, w_down, *, tm=1024, tf=512):
    n, d = h2.shape
    dff = w_down.shape[0]
    nj = dff // tf
    return pl.pallas_call(
        _ffn_kernel,
        grid=(n // tm, nj),
        in_specs=[
            pl.BlockSpec((tm, d), lambda i, j: (i, 0)),
            pl.BlockSpec((d, tf), lambda i, j: (0, j)),
            pl.BlockSpec((d, tf), lambda i, j: (0, j + nj)),
            pl.BlockSpec((tf, d), lambda i, j: (j, 0)),
        ],
        out_specs=pl.BlockSpec((tm, d), lambda i, j: (i, 0)),
        out_shape=jax.ShapeDtypeStruct((n, d), F32),
        compiler_params=_params("arbitrary", "arbitrary"),
        name="ffn",
    )(h2, w_gu, w_gu, w_down)


def _ple_kernel(x1_ref, y_ref, p_ref, wg_ref, wp_ref, gain_ref, o_ref, x2_scr, *, tc, final_norm):
    d = x1_ref.shape[1]
    x2_scr[...] = x1_ref[...] + y_ref[...]
    xb = x2_scr[...].astype(BF16)
    pb = p_ref[...].astype(BF16)
    ssq = jnp.zeros((x1_ref.shape[0], 1), F32)
    for cb in range(d // tc):
        cs = slice(cb * tc, (cb + 1) * tc)
        gate = jax.nn.sigmoid(jnp.dot(xb, wg_ref[:, cs], preferred_element_type=F32))
        pe = jnp.dot(pb, wp_ref[:, cs], preferred_element_type=F32)
        x3 = x2_scr[:, cs] + gate * pe
        o_ref[:, cs] = x3
        ssq = ssq + jnp.sum(x3 * x3, axis=-1, keepdims=True)
    if not final_norm:
        return
    inv = lax.rsqrt(ssq * (1.0 / d) + NORM_EPS)
    for cb in range(d // tc):
        cs = slice(cb * tc, (cb + 1) * tc)
        o_ref[:, cs] = o_ref[:, cs] * inv * gain_ref[:, cs]


def _ple(x1, y, p2d, wg, wp, gain, *, final_norm, tm=512, tc=512):
    n, d = x1.shape
    row = lambda i: (i, 0)
    return pl.pallas_call(
        functools.partial(_ple_kernel, tc=tc, final_norm=final_norm),
        grid=(n // tm,),
        in_specs=[
            pl.BlockSpec((tm, d), row),
            pl.BlockSpec((tm, d), row),
            pl.BlockSpec((tm, p2d.shape[1]), row),
            _resident(wg.shape), _resident(wp.shape), _resident(gain.shape),
        ],
        out_specs=pl.BlockSpec((tm, d), row),
        out_shape=jax.ShapeDtypeStruct((n, d), F32),
        scratch_shapes=[pltpu.VMEM((tm, d), F32)],
        compiler_params=_params("arbitrary"),
        name="ple",
    )(x1, y, p2d, wg, wp, gain)


def kernel(x, p, norm_mix, w_in, hgrn_norm, w_proj_a, w_proj_b, w_out, norm_ffn, w_gate_up, w_down, w_ple,
           w_ple_gate, rel_bias, hgrn_lb_logits, norm_final):
    bsz, seq, d = x.shape
    n = bsz * seq
    depth = w_in.shape[0]
    a_heads = A_WIDTH // HEAD_DIM
    b_heads = B_WIDTH // HEAD_DIM

    cscale = np.ones((1, IN_WIDTH), np.float32)
    cscale[:, COL_QA * SEG:(COL_QA + 1) * SEG] = LOG2E / math.sqrt(HEAD_DIM)
    cscale = jnp.asarray(cscale)

    xc = x.reshape(n, d)
    for i in range(depth):
        proj, fb = _in_proj(xc, norm_mix[i][None], w_in[i].astype(BF16), cscale)
        proj3 = proj.reshape(bsz, seq, IN_WIDTH)
        later_weights = (w_proj_a[i], w_proj_b[i], w_out[i], w_gate_up[i], w_down[i], w_ple_gate[i], w_ple[i])
        ya, (wa, wb, wo, wgu, wdn, wpg, wpl) = _moba(proj3, rel_bias, later_weights, heads=a_heads)
        yb = _hgrn(proj3, fb.reshape(bsz, seq, SEG), hgrn_lb_logits, hgrn_norm[i][None], heads=b_heads, layer=i)
        x1, h2 = _mix(ya.reshape(n, A_WIDTH), yb.reshape(n, B_WIDTH), proj, xc, wa, wb, wo, norm_ffn[i][None])
        y = _ffn(h2, wgu, wdn)
        xc = _ple(x1, y, p[i].reshape(n, -1), wpg, wpl, norm_final[None], final_norm=(i == depth - 1))
    return xc.reshape(bsz, seq, d)
```

```python
import functools
import math

import jax
import jax.numpy as jnp
import numpy as np
from jax import lax
from jax.experimental import pallas as pl
from jax.experimental.pallas import tpu as pltpu

F32 = jnp.float32
BF16 = jnp.bfloat16

D_MODEL = 2048
HEAD_DIM = 128
A_WIDTH = 1024
B_WIDTH = 1024
MOBA_BLOCK = 256
MOBA_TOPK = 3
REL_BUCKETS = 32
REL_MAX_EXACT = 16
REL_MAX_DIST = 128
HGRN_CHUNK = 64
D_FF = 5632
NORM_EPS = 1e-6

COL_QA, COL_KA, COL_VA, COL_QB, COL_FB, COL_IB, COL_GB, COL_GATE_A, COL_GATE_B = 0, 1, 2, 3, 4, 5, 6, 7, 9
IN_WIDTH = 11 * 1024
SEG = 1024

BF16_SUBLANES = 16
MASK_VALUE = -1e30
VMEM_LIMIT = 56 * 1024 * 1024

NT_DIMS = (((1,), (1,)), ((), ()))
LOG2E = math.log2(math.e)


def _bucket_thresholds():
    n = np.arange(0, 4096, dtype=np.int64)
    nf = np.maximum(n, REL_MAX_EXACT).astype(np.float64)
    large = REL_MAX_EXACT + (np.log(nf / REL_MAX_EXACT) / math.log(REL_MAX_DIST / REL_MAX_EXACT)
                             * (REL_BUCKETS - REL_MAX_EXACT)).astype(np.int64)
    large = np.minimum(large, REL_BUCKETS - 1)
    bucket = np.where(n < REL_MAX_EXACT, n, large)
    return [int(np.argmax(bucket >= k)) for k in range(REL_BUCKETS)]


BUCKET_LO = _bucket_thresholds()


def _params(*sem):
    return pltpu.CompilerParams(dimension_semantics=sem, vmem_limit_bytes=VMEM_LIMIT)


def _resident(shape):
    return pl.BlockSpec(shape, lambda *_: (0,) * len(shape), pipeline_mode=pl.Buffered(1))


def _cast_specs(weights, n_steps, step_index):
    specs = []
    for w in weights:
        r = BF16_SUBLANES
        while w.shape[0] % r or w.shape[0] // r > n_steps:
            r += BF16_SUBLANES
        last = w.shape[0] // r - 1
        specs.append(pl.BlockSpec((r, w.shape[1]), lambda *ids, last=last: (jnp.minimum(step_index(*ids), last), 0)))
    return specs


def _cast_slabs(w32_refs, w16_refs):
    for w32, w16 in zip(w32_refs, w16_refs):
        w16[...] = w32[...].astype(BF16)


def _in_proj_kernel(x_ref, gain_ref, w_ref, cscale_ref, o_ref, fb_ref, h_scr, *, rows_per_step):
    j = pl.program_id(1)

    @pl.when(j == 0)
    def _():
        def body(r, _):
            rs = pl.ds(pl.multiple_of(r * rows_per_step, rows_per_step), rows_per_step)
            x = x_ref[rs, :]
            ms = jnp.mean(x * x, axis=-1, keepdims=True)
            h_scr[rs, :] = (x * lax.rsqrt(ms + NORM_EPS) * gain_ref[...]).astype(BF16)
            return 0
        lax.fori_loop(0, x_ref.shape[0] // rows_per_step, body, 0)

    acc = jnp.dot(h_scr[...], w_ref[...], preferred_element_type=F32)
    o_ref[...] = (acc * cscale_ref[...]).astype(BF16)
    pltpu.store(fb_ref, acc, mask=jnp.broadcast_to(j == COL_FB, acc.shape))


def _in_proj(x2d, gain, w_bf16, cscale, *, tm=1024, tn=SEG):
    n, d = x2d.shape
    width = w_bf16.shape[1]
    return pl.pallas_call(
        functools.partial(_in_proj_kernel, rows_per_step=128),
        grid=(n // tm, width // tn),
        in_specs=[
            pl.BlockSpec((tm, d), lambda i, j: (i, 0)),
            pl.BlockSpec((1, d), lambda i, j: (0, 0)),
            pl.BlockSpec((d, tn), lambda i, j: (0, j)),
            pl.BlockSpec((1, tn), lambda i, j: (0, j)),
        ],
        out_specs=[
            pl.BlockSpec((tm, tn), lambda i, j: (i, j)),
            pl.BlockSpec((tm, SEG), lambda i, j: (i, 0)),
        ],
        out_shape=[
            jax.ShapeDtypeStruct((n, width), BF16),
            jax.ShapeDtypeStruct((n, SEG), F32),
        ],
        scratch_shapes=[pltpu.VMEM((tm, d), BF16)],
        compiler_params=_params("arbitrary", "arbitrary"),
        name="in_proj",
    )(x2d, gain, w_bf16, cscale)


def _moba_kernel(rb_ref, qall_ref, q_ref, k_ref, v_ref, *rest, n_blocks, group, n_cast):
    w32_refs = rest[:n_cast]
    o_ref = rest[n_cast]
    w16_refs = rest[n_cast + 1:2 * n_cast + 1]
    bias_scr, vt_scr, sel_scr, s_scr, p_scr = rest[2 * n_cast + 1:]
    hg = pl.program_id(0)
    b = pl.program_id(1)
    qi = pl.program_id(2)
    blk = MOBA_BLOCK
    dh = HEAD_DIM
    sub = 8

    @pl.when((b == 0) & (qi == 0))
    def _build_bias():
        r = lax.broadcasted_iota(jnp.int32, (blk, blk), 1)
        c = lax.broadcasted_iota(jnp.int32, (blk, blk), 0)
        for g in range(group):
            hd = hg * group + g
            for d in range(2):
                rel = d * blk + r - c
                bias = jnp.full((blk, blk), rb_ref[REL_BUCKETS - 1, hd] * LOG2E, F32)
                for kk in range(REL_BUCKETS - 2, -1, -1):
                    bias = jnp.where(rel < BUCKET_LO[kk + 1], rb_ref[kk, hd] * LOG2E, bias)
                if d == 0:
                    bias = jnp.where(rel < 0, MASK_VALUE, bias)
                bias_scr[g, d] = bias

    @pl.when(qi == 0)
    def _per_sequence():
        for g in range(group):
            hs = slice(g * dh, (g + 1) * dh)
            means = []
            for jb in range(n_blocks):
                rs = slice(jb * blk, (jb + 1) * blk)
                means.append(jnp.mean(k_ref[0, rs, hs].astype(F32), axis=0, keepdims=True))
                vt_scr[g, :, rs] = v_ref[0, rs, hs].astype(F32).T.astype(BF16)
            km = jnp.concatenate(means + [jnp.zeros((16 - n_blocks, dh), F32)], axis=0)

            km_hi = km.astype(BF16)
            km_lo = (km - km_hi.astype(F32)).astype(BF16)
            qall = qall_ref[0, :, hs]
            sc = (lax.dot_general(km_hi, qall, NT_DIMS, preferred_element_type=F32)
                  + lax.dot_general(km_lo, qall, NT_DIMS, preferred_element_type=F32))
            rows = lax.broadcasted_iota(jnp.int32, sc.shape, 0)
            rows_f = rows.astype(F32)
            past = rows < lax.broadcasted_iota(jnp.int32, sc.shape, 1) // blk
            s = jnp.where(past, sc, -jnp.inf)
            picked = jnp.zeros(sc.shape, jnp.bool_)
            for _ in range(MOBA_TOPK):
                top = jnp.max(s, axis=0, keepdims=True)
                first = jnp.min(jnp.where(s == top, rows_f, 1e9), axis=0, keepdims=True)
                hit = rows_f == first
                picked = picked | hit
                s = jnp.where(hit, -jnp.inf, s)
            mask = jnp.where(picked & past, 0.0, MASK_VALUE)
            for qb in range(n_blocks):
                sel_scr[g, qb] = mask[:, qb * blk:(qb + 1) * blk]

    def attend(nt):
        _cast_slabs(w32_refs, w16_refs)
        hss =[slice(g * dh, (g + 1) * dh) for g in range(group)]
        ms = []
        for g in range(group):
            far = rb_ref[REL_BUCKETS - 1, hg * group + g] * LOG2E
            s = lax.dot_general(k_ref[0, :nt * blk, hss[g]], q_ref[0, :, hss[g]], NT_DIMS,
                                preferred_element_type=F32)
            m8 = None
            for j in range(nt):
                rs = slice(j * blk, (j + 1) * blk)
                d = nt - 1 - j
                if d == 0:
                    sj = s[rs] + bias_scr[g, 0]
                elif d == 1:
                    sj = s[rs] + bias_scr[g, 1] + sel_scr[g, qi, j:j + 1, :]
                else:
                    sj = s[rs] + (sel_scr[g, qi, j:j + 1, :] + far)
                s_scr[g, rs, :] = sj
                mj = jnp.max(sj.reshape(blk // sub, sub, blk), axis=0)
                m8 = mj if m8 is None else jnp.maximum(m8, mj)
            ms.append(jnp.max(m8, axis=0, keepdims=True))
        ls = []
        for g in range(group):
            l8 = jnp.zeros((sub, blk), F32)
            for j in range(nt):
                rs = slice(j * blk, (j + 1) * blk)
                p = jnp.exp2(s_scr[g, rs, :] - ms[g])
                l8 = l8 + jnp.sum(p.reshape(blk // sub, sub, blk), axis=0)
                p_scr[g, rs, :] = p.astype(BF16)
            ls.append(jnp.sum(l8, axis=0, keepdims=True))
        for g in range(group):
            acc = jnp.dot(vt_scr[g, :, :nt * blk], p_scr[g, :nt * blk, :], preferred_element_type=F32)
            o_ref[0, :, hss[g]] = (acc / ls[g]).T.astype(o_ref.dtype)

    for nt in range(1, n_blocks + 1):
        pl.when(qi == nt - 1)(functools.partial(attend, nt))


def _moba(proj3, rel_bias, cast_weights, *, heads, group=4):
    bsz, seq, _ = proj3.shape
    nb = seq // MOBA_BLOCK
    gw = group * HEAD_DIM
    cq, ck, cv = (c * (SEG // gw) for c in (COL_QA, COL_KA, COL_VA))
    cast_specs = _cast_specs(cast_weights, (heads // group) * bsz * nb, lambda h, b, i: (h * bsz + b) * nb + i)
    outs = pl.pallas_call(
        functools.partial(_moba_kernel, n_blocks=nb, group=group, n_cast=len(cast_weights)),
        grid=(heads // group, bsz, nb),
        in_specs=[
            pl.BlockSpec(memory_space=pltpu.SMEM),
            pl.BlockSpec((1, seq, gw), lambda h, b, i: (b, 0, cq + h)),
            pl.BlockSpec((1, MOBA_BLOCK, gw), lambda h, b, i: (b, i, cq + h)),
            pl.BlockSpec((1, seq, gw), lambda h, b, i: (b, 0, ck + h)),
            pl.BlockSpec((1, seq, gw), lambda h, b, i: (b, 0, cv + h)),
            *cast_specs,
        ],
        out_specs=[pl.BlockSpec((1, MOBA_BLOCK, gw), lambda h, b, i: (b, i, h)), *cast_specs],
        out_shape=[jax.ShapeDtypeStruct((bsz, seq, heads * HEAD_DIM), BF16),
                   *(jax.ShapeDtypeStruct(w.shape, BF16) for w in cast_weights)],
        scratch_shapes=[
            pltpu.VMEM((group, 2, MOBA_BLOCK, MOBA_BLOCK), F32),
            pltpu.VMEM((group, HEAD_DIM, seq), BF16),
            pltpu.VMEM((group, nb, 16, MOBA_BLOCK), F32),
            pltpu.VMEM((group, seq, MOBA_BLOCK), F32),
            pltpu.VMEM((group, seq, MOBA_BLOCK), BF16),
        ],
        compiler_params=_params("arbitrary", "arbitrary", "arbitrary"),
        name="moba",
    )(rel_bias, proj3, proj3, proj3, proj3, *cast_weights)
    return outs[0], outs[1:]


TN_DIMS = (((0,), (0,)), ((), ()))


def _cumsum_rows(x, period):
    pos = lax.broadcasted_iota(jnp.int32, x.shape, 0) % period
    sh = 1
    while sh < period:
        x = x + jnp.where(pos >= sh, pltpu.roll(x, sh, axis=0), 0.0)
        sh *= 2
    return x


def _hgrn_kernel(lbl_ref, gain_ref, q_ref, f_ref, i_ref, g_ref, *rest, heads, layer, chunks, n_cast):
    w32_refs = rest[:n_cast]
    o_ref = rest[n_cast]
    w16_refs = rest[n_cast + 1:2 * n_cast + 1]
    st_scr, = rest[2 * n_cast + 1:]
    _cast_slabs(w32_refs, w16_refs)
    ch = HGRN_CHUNK
    dh = HEAD_DIM
    rows = chunks * ch

    @pl.when(pl.program_id(1) == 0)
    def _():
        st_scr[...] = jnp.zeros_like(st_scr)

    lg = lbl_ref[...]
    e = jnp.exp(lg - jnp.max(lg, axis=0, keepdims=True))
    lb = jnp.sum(e[:layer + 1], axis=0, keepdims=True) / jnp.sum(e, axis=0, keepdims=True)

    f = lb + (1.0 - lb) * jax.nn.sigmoid(f_ref[0])
    bcum = _cumsum_rows(jnp.log(f), ch)
    q_dec = (q_ref[0].astype(F32) * jnp.exp(bcum)).astype(BF16)
    k_dec32 = (1.0 - f) * jnp.exp(-bcum)
    k_dec = k_dec32.astype(BF16)
    decays = [jnp.exp(bcum[(c + 1) * ch - 1:(c + 1) * ch, :]) for c in range(chunks)]
    k_end = [(k_dec32[c * ch:(c + 1) * ch, :] * decays[c]).astype(BF16) for c in range(chunks)]
    v = i_ref[0]
    gate = g_ref[0].astype(F32)
    gate = gate * jax.nn.sigmoid(gate)
    gain = gain_ref[...]

    r = lax.broadcasted_iota(jnp.int32, (rows, rows), 0)
    c_ = lax.broadcasted_iota(jnp.int32, (rows, rows), 1)
    causal_in_chunk = (r >= c_) & (r // ch == c_ // ch)

    for hd in range(heads):
        sl = slice(hd * dh, (hd + 1) * dh)
        qd, vh = q_dec[:, sl], v[:, sl]
        a = lax.dot_general(qd, k_dec[:, sl], NT_DIMS, preferred_element_type=F32)
        a = jnp.where(causal_in_chunk, a, 0.0).astype(BF16)
        o_intra = jnp.dot(a, vh, preferred_element_type=F32)
        st = st_scr[hd]
        outs = []
        for c in range(chunks):
            rs = slice(c * ch, (c + 1) * ch)
            outs.append(o_intra[rs] + lax.dot_general(qd[rs], st.astype(BF16), NT_DIMS, preferred_element_type=F32))
            st = st * decays[c][:, sl] + lax.dot_general(vh[rs], k_end[c][:, sl], TN_DIMS,
                                                         preferred_element_type=F32)
        st_scr[hd] = st
        o = jnp.concatenate(outs, axis=0)
        ms = jnp.mean(o * o, axis=-1, keepdims=True)
        y = o * lax.rsqrt(ms + NORM_EPS) * gain[:, sl]
        o_ref[0, :, sl] = (y * gate[:, sl]).astype(o_ref.dtype)


def _hgrn(proj3, fb3, lb_logits, gain, cast_weights, *, heads, layer, chunks=4):
    bsz, seq, _ = proj3.shape
    width = heads * HEAD_DIM
    rows = chunks * HGRN_CHUNK
    blk = (1, rows, width)
    nc = seq // rows
    cast_specs = _cast_specs(cast_weights, bsz * nc, lambda b, c: b * nc + c)
    outs = pl.pallas_call(
        functools.partial(_hgrn_kernel, heads=heads, layer=layer, chunks=chunks, n_cast=len(cast_weights)),
        grid=(bsz, nc),
        in_specs=[
            pl.BlockSpec(lb_logits.shape, lambda b, c: (0, 0)),
            pl.BlockSpec((1, width), lambda b, c: (0, 0)),
            pl.BlockSpec(blk, lambda b, c: (b, c, COL_QB)),
            pl.BlockSpec(blk, lambda b, c: (b, c, 0)),
            pl.BlockSpec(blk, lambda b, c: (b, c, COL_IB)),
            pl.BlockSpec(blk, lambda b, c: (b, c, COL_GB)),
            *cast_specs,
        ],
        out_specs=[pl.BlockSpec(blk, lambda b, c: (b, c, 0)), *cast_specs],
        out_shape=[jax.ShapeDtypeStruct((bsz, seq, width), BF16),
                   *(jax.ShapeDtypeStruct(w.shape, BF16) for w in cast_weights)],
        scratch_shapes=[pltpu.VMEM((heads, HEAD_DIM, HEAD_DIM), F32)],
        compiler_params=_params("arbitrary", "arbitrary"),
        name="hgrn2",
    )(lb_logits, gain, proj3, fb3, proj3, proj3, *cast_weights)
    return outs[0], outs[1:]


def _mix_kernel(ya_ref, yb_ref, ga0_ref, ga1_ref, gb0_ref, gb1_ref, x_ref, wa_ref, wb_ref, wo_ref, gain_ref,
                *rest, tc, n_cast):
    w32_refs = rest[:n_cast]
    x1_ref, h2_ref = rest[n_cast:n_cast + 2]
    w16_refs = rest[n_cast + 2:2 * n_cast + 2]
    mixed_scr, = rest[2 * n_cast + 2:]
    _cast_slabs(w32_refs, w16_refs)
    d = x_ref.shape[1]
    gate_refs = ((ga0_ref, gb0_ref), (ga1_ref, gb1_ref))
    ya = ya_ref[...]
    yb = yb_ref[...]
    for cb in range(d // tc):
        cs = slice(cb * tc, (cb + 1) * tc)
        ga_ref, gb_ref = gate_refs[(cb * tc) // SEG]
        gs = slice((cb * tc) % SEG, (cb * tc) % SEG + tc)
        pa = jnp.dot(ya, wa_ref[:, cs], preferred_element_type=F32)
        pb = jnp.dot(yb, wb_ref[:, cs], preferred_element_type=F32)
        mixed = (jax.nn.sigmoid(ga_ref[:, gs].astype(F32)) * pa
                 + jax.nn.sigmoid(gb_ref[:, gs].astype(F32)) * pb)
        mixed_scr[:, cs] = mixed.astype(BF16)
    ssq = jnp.zeros((x_ref.shape[0], 1), F32)
    for cb in range(d // tc):
        cs = slice(cb * tc, (cb + 1) * tc)
        x1 = x_ref[:, cs] + jnp.dot(mixed_scr[...], wo_ref[:, cs], preferred_element_type=F32)
        x1_ref[:, cs] = x1
        ssq = ssq + jnp.sum(x1 * x1, axis=-1, keepdims=True)
    inv = lax.rsqrt(ssq * (1.0 / d) + NORM_EPS)
    for cb in range(d // tc):
        cs = slice(cb * tc, (cb + 1) * tc)
        h2_ref[:, cs] = (x1_ref[:, cs] * inv * gain_ref[:, cs]).astype(BF16)


def _mix(ya, yb, proj, x2d, wa, wb, wo, gain, cast_weights, *, tm=512, tc=512):
    n, d = x2d.shape
    row = lambda i: (i, 0)
    seg = lambda c: pl.BlockSpec((tm, SEG), lambda i: (i, c))
    cast_specs = _cast_specs(cast_weights, n // tm, lambda i: i)
    outs = pl.pallas_call(
        functools.partial(_mix_kernel, tc=tc, n_cast=len(cast_weights)),
        grid=(n // tm,),
        in_specs=[
            pl.BlockSpec((tm, A_WIDTH), row),
            pl.BlockSpec((tm, B_WIDTH), row),
            seg(COL_GATE_A), seg(COL_GATE_A + 1), seg(COL_GATE_B), seg(COL_GATE_B + 1),
            pl.BlockSpec((tm, d), row),
            _resident(wa.shape), _resident(wb.shape), _resident(wo.shape), _resident(gain.shape),
            *cast_specs,
        ],
        out_specs=[pl.BlockSpec((tm, d), row), pl.BlockSpec((tm, d), row), *cast_specs],
        out_shape=[jax.ShapeDtypeStruct((n, d), F32), jax.ShapeDtypeStruct((n, d), BF16),
                   *(jax.ShapeDtypeStruct(w.shape, BF16) for w in cast_weights)],
        scratch_shapes=[pltpu.VMEM((tm, d), BF16)],
        compiler_params=_params("arbitrary"),
        name="mix",
    )(ya, yb, proj, proj, proj, proj, x2d, wa, wb, wo, gain, *cast_weights)
    return outs[0], outs[1], outs[2:]


def _ffn_kernel(h_ref, wg_ref, wu_ref, wd_ref, o_ref):
    @pl.when(pl.program_id(1) == 0)
    def _():
        o_ref[...] = jnp.zeros_like(o_ref)

    h = h_ref[...]
    g = jnp.dot(h, wg_ref[...], preferred_element_type=F32)
    u = jnp.dot(h, wu_ref[...], preferred_element_type=F32)
    act = (g * jax.nn.sigmoid(g) * u).astype(BF16)
    o_ref[...] += jnp.dot(act, wd_ref[...], preferred_element_type=F32)


def _ffn(h2, w_gu, w_down, *, tm=1024, tf=512):
    n, d = h2.shape
    dff = w_down.shape[0]
    nj = dff // tf
    return pl.pallas_call(
        _ffn_kernel,
        grid=(n // tm, nj),
        in_specs=[
            pl.BlockSpec((tm, d), lambda i, j: (i, 0)),
            pl.BlockSpec((d, tf), lambda i, j: (0, j)),
            pl.BlockSpec((d, tf), lambda i, j: (0, j + nj)),
            pl.BlockSpec((tf, d), lambda i, j: (j, 0)),
        ],
        out_specs=pl.BlockSpec((tm, d), lambda i, j: (i, 0)),
        out_shape=jax.ShapeDtypeStruct((n, d), F32),
        compiler_params=_params("arbitrary", "arbitrary"),
        name="ffn",
    )(h2, w_gu, w_gu, w_down)


def _ple_kernel(x1_ref, y_ref, p_ref, wg_ref, wp_ref, gain_ref, o_ref, x2_scr, *, tc, final_norm):
    d = x1_ref.shape[1]
    x2_scr[...] = x1_ref[...] + y_ref[...]
    xb = x2_scr[...].astype(BF16)
    pb = p_ref[...].astype(BF16)
    ssq = jnp.zeros((x1_ref.shape[0], 1), F32)
    for cb in range(d // tc):
        cs = slice(cb * tc, (cb + 1) * tc)
        gate = jax.nn.sigmoid(jnp.dot(xb, wg_ref[:, cs], preferred_element_type=F32))
        pe = jnp.dot(pb, wp_ref[:, cs], preferred_element_type=F32)
        x3 = x2_scr[:, cs] + gate * pe
        o_ref[:, cs] = x3
        ssq = ssq + jnp.sum(x3 * x3, axis=-1, keepdims=True)
    if not final_norm:
        return
    inv = lax.rsqrt(ssq * (1.0 / d) + NORM_EPS)
    for cb in range(d // tc):
        cs = slice(cb * tc, (cb + 1) * tc)
        o_ref[:, cs] = o_ref[:, cs] * inv * gain_ref[:, cs]


def _ple(x1, y, p2d, wg, wp, gain, *, final_norm, tm=512, tc=512):
    n, d = x1.shape
    row = lambda i: (i, 0)
    return pl.pallas_call(
        functools.partial(_ple_kernel, tc=tc, final_norm=final_norm),
        grid=(n // tm,),
        in_specs=[
            pl.BlockSpec((tm, d), row),
            pl.BlockSpec((tm, d), row),
            pl.BlockSpec((tm, p2d.shape[1]), row),
            _resident(wg.shape), _resident(wp.shape), _resident(gain.shape),
        ],
        out_specs=pl.BlockSpec((tm, d), row),
        out_shape=jax.ShapeDtypeStruct((n, d), F32),
        scratch_shapes=[pltpu.VMEM((tm, d), F32)],
        compiler_params=_params("arbitrary"),
        name="ple",
    )(x1, y, p2d, wg, wp, gain)


def kernel(x, p, norm_mix, w_in, hgrn_norm, w_proj_a, w_proj_b, w_out, norm_ffn, w_gate_up, w_down, w_ple,
           w_ple_gate, rel_bias, hgrn_lb_logits, norm_final):
    bsz, seq, d = x.shape
    n = bsz * seq
    depth = w_in.shape[0]
    a_heads = A_WIDTH // HEAD_DIM
    b_heads = B_WIDTH // HEAD_DIM

    cscale = np.ones((1, IN_WIDTH), np.float32)
    cscale[:, COL_QA * SEG:(COL_QA + 1) * SEG] = LOG2E / math.sqrt(HEAD_DIM)
    cscale = jnp.asarray(cscale)

    xc = x.reshape(n, d)
    for i in range(depth):
        proj, fb = _in_proj(xc, norm_mix[i][None], w_in[i].astype(BF16), cscale)
        proj3 = proj.reshape(bsz, seq, IN_WIDTH)
        ya, (wa, wb, wo) = _moba(proj3, rel_bias, (w_proj_a[i], w_proj_b[i], w_out[i]), heads=a_heads)
        yb, (wgu, wdn) = _hgrn(proj3, fb.reshape(bsz, seq, SEG), hgrn_lb_logits, hgrn_norm[i][None],
                               (w_gate_up[i], w_down[i]), heads=b_heads, layer=i)
        x1, h2, (wpg, wpl) = _mix(ya.reshape(n, A_WIDTH), yb.reshape(n, B_WIDTH), proj, xc, wa, wb, wo,
                                  norm_ffn[i][None], (w_ple_gate[i], w_ple[i]))
        y = _ffn(h2, wgu, wdn)
        xc = _ple(x1, y, p[i].reshape(n, -1), wpg, wpl, norm_final[None], final_norm=(i == depth - 1))
    return xc.reshape(bsz, seq, d)
```

```python
import functools
import math

import jax
import jax.numpy as jnp
import numpy as np
from jax import lax
from jax.experimental import pallas as pl
from jax.experimental.pallas import tpu as pltpu

F32 = jnp.float32
BF16 = jnp.bfloat16

D_MODEL = 2048
HEAD_DIM = 128
A_WIDTH = 1024
B_WIDTH = 1024
MOBA_BLOCK = 256
MOBA_TOPK = 3
REL_BUCKETS = 32
REL_MAX_EXACT = 16
REL_MAX_DIST = 128
HGRN_CHUNK = 64
D_FF = 5632
FFN_TILE = 512
NORM_EPS = 1e-6

COL_QA, COL_KA, COL_VA, COL_QB, COL_FB, COL_IB, COL_GB, COL_GATE_A, COL_GATE_B = 0, 1, 2, 3, 4, 5, 6, 7, 9
IN_WIDTH = 11 * 1024
SEG = 1024

BF16_SUBLANES = 16
MASK_VALUE = -1e30
VMEM_LIMIT = 56 * 1024 * 1024

NT_DIMS = (((1,), (1,)), ((), ()))
LOG2E = math.log2(math.e)


def _bucket_thresholds():
    n = np.arange(0, 4096, dtype=np.int64)
    nf = np.maximum(n, REL_MAX_EXACT).astype(np.float64)
    large = REL_MAX_EXACT + (np.log(nf / REL_MAX_EXACT) / math.log(REL_MAX_DIST / REL_MAX_EXACT)
                             * (REL_BUCKETS - REL_MAX_EXACT)).astype(np.int64)
    large = np.minimum(large, REL_BUCKETS - 1)
    bucket = np.where(n < REL_MAX_EXACT, n, large)
    return [int(np.argmax(bucket >= k)) for k in range(REL_BUCKETS)]


BUCKET_LO = _bucket_thresholds()


def _params(*sem):
    return pltpu.CompilerParams(dimension_semantics=sem, vmem_limit_bytes=VMEM_LIMIT)


def _resident(shape):
    return pl.BlockSpec(shape, lambda *_: (0,) * len(shape), pipeline_mode=pl.Buffered(1))


def _cast_specs(weights, col_tiles, n_steps, step_index):
    in_specs, out_specs, out_shapes = [], [], []
    for w, t in zip(weights, col_tiles):
        rows, cols = w.shape
        r = BF16_SUBLANES
        while rows % r or rows // r > n_steps:
            r += BF16_SUBLANES
        last = rows // r - 1
        slab = lambda *ids, last=last: jnp.minimum(step_index(*ids), last)
        in_specs.append(pl.BlockSpec((r, cols), lambda *ids, slab=slab: (slab(*ids), 0)))
        if t == 1:
            out_specs.append(pl.BlockSpec((r, cols), lambda *ids, slab=slab: (slab(*ids), 0)))
            out_shapes.append(jax.ShapeDtypeStruct((rows, cols), BF16))
        else:
            out_specs.append(pl.BlockSpec((t, r, cols // t), lambda *ids, slab=slab: (0, slab(*ids), 0)))
            out_shapes.append(jax.ShapeDtypeStruct((t, rows, cols // t), BF16))
    return in_specs, out_specs, out_shapes


def _cast_slabs(w32_refs, w16_refs):
    for w32, w16 in zip(w32_refs, w16_refs):
        if len(w16.shape) == 2:
            w16[...] = w32[...].astype(BF16)
        else:
            tw = w16.shape[2]
            for t in range(w16.shape[0]):
                w16[t] = w32[:, t * tw:(t + 1) * tw].astype(BF16)


def _in_proj_kernel(x_ref, gain_ref, w_ref, cscale_ref, o_ref, fb_ref, h_scr, *, rows_per_step):
    j = pl.program_id(1)

    @pl.when(j == 0)
    def _():
        def body(r, _):
            rs = pl.ds(pl.multiple_of(r * rows_per_step, rows_per_step), rows_per_step)
            x = x_ref[rs, :]
            ms = jnp.mean(x * x, axis=-1, keepdims=True)
            h_scr[rs, :] = (x * lax.rsqrt(ms + NORM_EPS) * gain_ref[...]).astype(BF16)
            return 0
        lax.fori_loop(0, x_ref.shape[0] // rows_per_step, body, 0)

    acc = jnp.dot(h_scr[...], w_ref[...], preferred_element_type=F32)
    o_ref[...] = (acc * cscale_ref[...]).astype(BF16)
    pltpu.store(fb_ref, acc, mask=jnp.broadcast_to(j == COL_FB, acc.shape))


def _in_proj(x2d, gain, w_tiles, cscale, *, tm=1024):
    n, d = x2d.shape
    nj, _, tn = w_tiles.shape
    width = nj * tn
    return pl.pallas_call(
        functools.partial(_in_proj_kernel, rows_per_step=128),
        grid=(n // tm, nj),
        in_specs=[
            pl.BlockSpec((tm, d), lambda i, j: (i, 0)),
            pl.BlockSpec((1, d), lambda i, j: (0, 0)),
            pl.BlockSpec((None, d, tn), lambda i, j: (j, 0, 0)),
            pl.BlockSpec((1, tn), lambda i, j: (0, j)),
        ],
        out_specs=[
            pl.BlockSpec((tm, tn), lambda i, j: (i, j)),
            pl.BlockSpec((tm, SEG), lambda i, j: (i, 0)),
        ],
        out_shape=[
            jax.ShapeDtypeStruct((n, width), BF16),
            jax.ShapeDtypeStruct((n, SEG), F32),
        ],
        scratch_shapes=[pltpu.VMEM((tm, d), BF16)],
        compiler_params=_params("arbitrary", "arbitrary"),
        name="in_proj",
    )(x2d, gain, w_tiles, cscale)


def _moba_kernel(rb_ref, qall_ref, q_ref, k_ref, v_ref, *rest, n_blocks, group, n_cast):
    w32_refs = rest[:n_cast]
    o_ref = rest[n_cast]
    w16_refs = rest[n_cast + 1:2 * n_cast + 1]
    bias_scr, vt_scr, sel_scr, s_scr, p_scr = rest[2 * n_cast + 1:]
    hg = pl.program_id(0)
    b = pl.program_id(1)
    qi = pl.program_id(2)
    blk = MOBA_BLOCK
    dh = HEAD_DIM
    sub = 8

    @pl.when((b == 0) & (qi == 0))
    def _build_bias():
        r = lax.broadcasted_iota(jnp.int32, (blk, blk), 1)
        c = lax.broadcasted_iota(jnp.int32, (blk, blk), 0)
        for g in range(group):
            hd = hg * group + g
            for d in range(2):
                rel = d * blk + r - c
                bias = jnp.full((blk, blk), rb_ref[REL_BUCKETS - 1, hd] * LOG2E, F32)
                for kk in range(REL_BUCKETS - 2, -1, -1):
                    bias = jnp.where(rel < BUCKET_LO[kk + 1], rb_ref[kk, hd] * LOG2E, bias)
                if d == 0:
                    bias = jnp.where(rel < 0, MASK_VALUE, bias)
                bias_scr[g, d] = bias

    @pl.when(qi == 0)
    def _per_sequence():
        for g in range(group):
            hs = slice(g * dh, (g + 1) * dh)
            means = []
            for jb in range(n_blocks):
                rs = slice(jb * blk, (jb + 1) * blk)
                means.append(jnp.mean(k_ref[0, rs, hs].astype(F32), axis=0, keepdims=True))
                vt_scr[g, :, rs] = v_ref[0, rs, hs].astype(F32).T.astype(BF16)
            km = jnp.concatenate(means + [jnp.zeros((16 - n_blocks, dh), F32)], axis=0)

            km_hi = km.astype(BF16)
            km_lo = (km - km_hi.astype(F32)).astype(BF16)
            qall = qall_ref[0, :, hs]
            sc = (lax.dot_general(km_hi, qall, NT_DIMS, preferred_element_type=F32)
                  + lax.dot_general(km_lo, qall, NT_DIMS, preferred_element_type=F32))
            rows = lax.broadcasted_iota(jnp.int32, sc.shape, 0)
            rows_f = rows.astype(F32)
            past = rows < lax.broadcasted_iota(jnp.int32, sc.shape, 1) // blk
            s = jnp.where(past, sc, -jnp.inf)
            picked = jnp.zeros(sc.shape, jnp.bool_)
            for _ in range(MOBA_TOPK):
                top = jnp.max(s, axis=0, keepdims=True)
                first = jnp.min(jnp.where(s == top, rows_f, 1e9), axis=0, keepdims=True)
                hit = rows_f == first
                picked = picked | hit
                s = jnp.where(hit, -jnp.inf, s)
            mask = jnp.where(picked & past, 0.0, MASK_VALUE)
            for qb in range(n_blocks):
                sel_scr[g, qb] = mask[:, qb * blk:(qb + 1) * blk]

    def attend(nt):
        _cast_slabs(w32_refs, w16_refs)
        hss = [slice(g * dh, (g + 1) * dh) for g in range(group)]
        ms = []
        for g in range(group):
            far = rb_ref[REL_BUCKETS - 1, hg * group + g] * LOG2E
            s = lax.dot_general(k_ref[0, :nt * blk, hss[g]], q_ref[0, :, hss[g]], NT_DIMS,
                                preferred_element_type=F32)
            m8 = None
            for j in range(nt):
                rs = slice(j * blk, (j + 1) * blk)
                d = nt - 1 - j
                if d == 0:
                    sj = s[rs] + bias_scr[g, 0]
                elif d == 1:
                    sj = s[rs] + bias_scr[g, 1] + sel_scr[g, qi, j:j + 1, :]
                else:
                    sj = s[rs] + (sel_scr[g, qi, j:j + 1, :] + far)
                s_scr[g, rs, :] = sj
                mj = jnp.max(sj.reshape(blk // sub, sub, blk), axis=0)
                m8 = mj if m8 is None else jnp.maximum(m8, mj)
            ms.append(jnp.max(m8, axis=0, keepdims=True))
        ls = []
        for g in range(group):
            l8 = jnp.zeros((sub, blk), F32)
            for j in range(nt):
                rs = slice(j * blk, (j + 1) * blk)
                p = jnp.exp2(s_scr[g, rs, :] - ms[g])
                l8 = l8 + jnp.sum(p.reshape(blk // sub, sub, blk), axis=0)
                p_scr[g, rs, :] = p.astype(BF16)
            ls.append(jnp.sum(l8, axis=0, keepdims=True))
        for g in range(group):
            acc = jnp.dot(vt_scr[g, :, :nt * blk], p_scr[g, :nt * blk, :], preferred_element_type=F32)
            o_ref[0, :, hss[g]] = (acc / ls[g]).T.astype(o_ref.dtype)

    for nt in range(1, n_blocks + 1):
        pl.when(qi == nt - 1)(functools.partial(attend, nt))


def _moba(proj3, rel_bias, cast_weights, *, heads, group=4):
    bsz, seq, _ = proj3.shape
    nb = seq // MOBA_BLOCK
    gw = group * HEAD_DIM
    cq, ck, cv = (c * (SEG // gw) for c in (COL_QA, COL_KA, COL_VA))
    cast_in, cast_out, cast_shapes = _cast_specs(cast_weights, (1,) * len(cast_weights), (heads // group) * bsz * nb,
                                                 lambda h, b, i: (h * bsz + b) * nb + i)
    outs = pl.pallas_call(
        functools.partial(_moba_kernel, n_blocks=nb, group=group, n_cast=len(cast_weights)),
        grid=(heads // group, bsz, nb),
        in_specs=[
            pl.BlockSpec(memory_space=pltpu.SMEM),
            pl.BlockSpec((1, seq, gw), lambda h, b, i: (b, 0, cq + h)),
            pl.BlockSpec((1, MOBA_BLOCK, gw), lambda h, b, i: (b, i, cq + h)),
            pl.BlockSpec((1, seq, gw), lambda h, b, i: (b, 0, ck + h)),
            pl.BlockSpec((1, seq, gw), lambda h, b, i: (b, 0, cv + h)),
            *cast_in,
        ],
        out_specs=[pl.BlockSpec((1, MOBA_BLOCK, gw), lambda h, b, i: (b, i, h)), *cast_out],
        out_shape=[jax.ShapeDtypeStruct((bsz, seq, heads * HEAD_DIM), BF16), *cast_shapes],
        scratch_shapes=[
            pltpu.VMEM((group, 2, MOBA_BLOCK, MOBA_BLOCK), F32),
            pltpu.VMEM((group, HEAD_DIM, seq), BF16),
            pltpu.VMEM((group, nb, 16, MOBA_BLOCK), F32),
            pltpu.VMEM((group, seq, MOBA_BLOCK), F32),
            pltpu.VMEM((group, seq, MOBA_BLOCK), BF16),
        ],
        compiler_params=_params("arbitrary", "arbitrary", "arbitrary"),
        name="moba",
    )(rel_bias, proj3, proj3, proj3, proj3, *cast_weights)
    return outs[0], outs[1:]


TN_DIMS = (((0,), (0,)), ((), ()))


def _cumsum_rows(x, period):
    pos = lax.broadcasted_iota(jnp.int32, x.shape, 0) % period
    sh = 1
    while sh < period:
        x = x + jnp.where(pos >= sh, pltpu.roll(x, sh, axis=0), 0.0)
        sh *= 2
    return x


def _hgrn_kernel(lbl_ref, gain_ref, q_ref, f_ref, i_ref, g_ref, *rest, heads, layer, chunks, n_cast):
    w32_refs = rest[:n_cast]
    o_ref = rest[n_cast]
    w16_refs = rest[n_cast + 1:2 * n_cast + 1]
    st_scr, = rest[2 * n_cast + 1:]
    _cast_slabs(w32_refs, w16_refs)
    ch = HGRN_CHUNK
    dh = HEAD_DIM
    rows = chunks * ch

    @pl.when(pl.program_id(1) == 0)
    def _():
        st_scr[...] = jnp.zeros_like(st_scr)

    lg = lbl_ref[...]
    e = jnp.exp(lg - jnp.max(lg, axis=0, keepdims=True))
    lb = jnp.sum(e[:layer + 1], axis=0, keepdims=True) / jnp.sum(e, axis=0, keepdims=True)

    f = lb + (1.0 - lb) * jax.nn.sigmoid(f_ref[0])
    bcum = _cumsum_rows(jnp.log(f), ch)
    q_dec = (q_ref[0].astype(F32) * jnp.exp(bcum)).astype(BF16)
    k_dec32 = (1.0 - f) * jnp.exp(-bcum)
    k_dec = k_dec32.astype(BF16)
    decays = [jnp.exp(bcum[(c + 1) * ch - 1:(c + 1) * ch, :]) for c in range(chunks)]
    k_end = [(k_dec32[c * ch:(c + 1) * ch, :] * decays[c]).astype(BF16) for c in range(chunks)]
    v = i_ref[0]
    gate = g_ref[0].astype(F32)
    gate = gate * jax.nn.sigmoid(gate)
    gain = gain_ref[...]

    r = lax.broadcasted_iota(jnp.int32, (rows, rows), 0)
    c_ = lax.broadcasted_iota(jnp.int32, (rows, rows), 1)
    causal_in_chunk = (r >= c_) & (r // ch == c_ // ch)

    for hd in range(heads):
        sl = slice(hd * dh, (hd + 1) * dh)
        qd, vh = q_dec[:, sl], v[:, sl]
        a = lax.dot_general(qd, k_dec[:, sl], NT_DIMS, preferred_element_type=F32)
        a = jnp.where(causal_in_chunk, a, 0.0).astype(BF16)
        o_intra = jnp.dot(a, vh, preferred_element_type=F32)
        st = st_scr[hd]
        outs = []
        for c in range(chunks):
            rs = slice(c * ch, (c + 1) * ch)
            outs.append(o_intra[rs] + lax.dot_general(qd[rs], st.astype(BF16), NT_DIMS, preferred_element_type=F32))
            st = st * decays[c][:, sl] + lax.dot_general(vh[rs], k_end[c][:, sl], TN_DIMS,
                                                         preferred_element_type=F32)
        st_scr[hd] = st
        o = jnp.concatenate(outs, axis=0)
        ms = jnp.mean(o * o, axis=-1, keepdims=True)
        y = o * lax.rsqrt(ms + NORM_EPS) * gain[:, sl]
        o_ref[0, :, sl] = (y * gate[:, sl]).astype(o_ref.dtype)


def _hgrn(proj3, fb3, lb_logits, gain, cast_weights, cast_col_tiles, *, heads, layer, chunks=4):
    bsz, seq, _ = proj3.shape
    width = heads * HEAD_DIM
    rows = chunks * HGRN_CHUNK
    blk = (1, rows, width)
    nc = seq // rows
    cast_in, cast_out, cast_shapes = _cast_specs(cast_weights, cast_col_tiles, bsz * nc, lambda b, c: b * nc + c)
    outs = pl.pallas_call(
        functools.partial(_hgrn_kernel, heads=heads, layer=layer, chunks=chunks, n_cast=len(cast_weights)),
        grid=(bsz, nc),
        in_specs=[
            pl.BlockSpec(lb_logits.shape, lambda b, c: (0, 0)),
            pl.BlockSpec((1, width), lambda b, c: (0, 0)),
            pl.BlockSpec(blk, lambda b, c: (b, c, COL_QB)),
            pl.BlockSpec(blk, lambda b, c: (b, c, 0)),
            pl.BlockSpec(blk, lambda b, c: (b, c, COL_IB)),
            pl.BlockSpec(blk, lambda b, c: (b, c, COL_GB)),
            *cast_in,
        ],
        out_specs=[pl.BlockSpec(blk, lambda b, c: (b, c, 0)), *cast_out],
        out_shape=[jax.ShapeDtypeStruct((bsz, seq, width), BF16), *cast_shapes],
        scratch_shapes=[pltpu.VMEM((heads, HEAD_DIM, HEAD_DIM), F32)],
        compiler_params=_params("arbitrary", "arbitrary"),
        name="hgrn2",
    )(lb_logits, gain, proj3, fb3, proj3, proj3, *cast_weights)
    return outs[0], outs[1:]


def _mix_kernel(ya_ref, yb_ref, ga0_ref, ga1_ref, gb0_ref, gb1_ref, x_ref, wa_ref, wb_ref, wo_ref, gain_ref,
                *rest, tc, n_cast):
    w32_refs = rest[:n_cast]
    x1_ref, h2_ref = rest[n_cast:n_cast + 2]
    w16_refs = rest[n_cast + 2:2 * n_cast + 2]
    mixed_scr, = rest[2 * n_cast + 2:]
    _cast_slabs(w32_refs, w16_refs)
    d = x_ref.shape[1]
    gate_refs = ((ga0_ref, gb0_ref), (ga1_ref, gb1_ref))
    ya = ya_ref[...]
    yb = yb_ref[...]
    for cb in range(d // tc):
        cs = slice(cb * tc, (cb + 1) * tc)
        ga_ref, gb_ref = gate_refs[(cb * tc) // SEG]
        gs = slice((cb * tc) % SEG, (cb * tc) % SEG + tc)
        pa = jnp.dot(ya, wa_ref[:, cs], preferred_element_type=F32)
        pb = jnp.dot(yb, wb_ref[:, cs], preferred_element_type=F32)
        mixed = (jax.nn.sigmoid(ga_ref[:, gs].astype(F32)) * pa
                 + jax.nn.sigmoid(gb_ref[:, gs].astype(F32)) * pb)
        mixed_scr[:, cs] = mixed.astype(BF16)
    ssq = jnp.zeros((x_ref.shape[0], 1), F32)
    for cb in range(d // tc):
        cs = slice(cb * tc, (cb + 1) * tc)
        x1 = x_ref[:, cs] + jnp.dot(mixed_scr[...], wo_ref[:, cs], preferred_element_type=F32)
        x1_ref[:, cs] = x1
        ssq = ssq + jnp.sum(x1 * x1, axis=-1, keepdims=True)
    inv = lax.rsqrt(ssq * (1.0 / d) + NORM_EPS)
    for cb in range(d // tc):
        cs = slice(cb * tc, (cb + 1) * tc)
        h2_ref[:, cs] = (x1_ref[:, cs] * inv * gain_ref[:, cs]).astype(BF16)


def _mix(ya, yb, proj, x2d, wa, wb, wo, gain, cast_weights, *, tm=512, tc=512):
    n, d = x2d.shape
    row = lambda i: (i, 0)
    seg = lambda c: pl.BlockSpec((tm, SEG), lambda i: (i, c))
    cast_in, cast_out, cast_shapes = _cast_specs(cast_weights, (1,) * len(cast_weights), n // tm, lambda i: i)
    outs = pl.pallas_call(
        functools.partial(_mix_kernel, tc=tc, n_cast=len(cast_weights)),
        grid=(n // tm,),
        in_specs=[
            pl.BlockSpec((tm, A_WIDTH), row),
            pl.BlockSpec((tm, B_WIDTH), row),
            seg(COL_GATE_A), seg(COL_GATE_A + 1), seg(COL_GATE_B), seg(COL_GATE_B + 1),
            pl.BlockSpec((tm, d), row),
            _resident(wa.shape), _resident(wb.shape), _resident(wo.shape), _resident(gain.shape),
            *cast_in,
        ],
        out_specs=[pl.BlockSpec((tm, d), row), pl.BlockSpec((tm, d), row), *cast_out],
        out_shape=[jax.ShapeDtypeStruct((n, d), F32), jax.ShapeDtypeStruct((n, d), BF16), *cast_shapes],
        scratch_shapes=[pltpu.VMEM((tm, d), BF16)],
        compiler_params=_params("arbitrary"),
        name="mix",
    )(ya, yb, proj, proj, proj, proj, x2d, wa, wb, wo, gain, *cast_weights)
    return outs[0], outs[1], outs[2:]


def _ffn_kernel(h_ref, wg_ref, wu_ref, wd_ref, o_ref):
    @pl.when(pl.program_id(1) == 0)
    def _():
        o_ref[...] = jnp.zeros_like(o_ref)

    h = h_ref[...]
    g = jnp.dot(h, wg_ref[...], preferred_element_type=F32)
    u = jnp.dot(h, wu_ref[...], preferred_element_type=F32)
    act = (g * jax.nn.sigmoid(g) * u).astype(BF16)
    o_ref[...] += jnp.dot(act, wd_ref[...], preferred_element_type=F32)


def _ffn(h2, w_gu_tiles, w_down, *, tm=1024):
    n, d = h2.shape
    nj, tf = w_gu_tiles.shape[0] // 2, w_gu_tiles.shape[2]
    return pl.pallas_call(
        _ffn_kernel,
        grid=(n // tm, nj),
        in_specs=[
            pl.BlockSpec((tm, d), lambda i, j: (i, 0)),
            pl.BlockSpec((None, d, tf), lambda i, j: (j, 0, 0)),
            pl.BlockSpec((None, d, tf), lambda i, j: (j + nj, 0, 0)),
            pl.BlockSpec((tf, d), lambda i, j: (j, 0)),
        ],
        out_specs=pl.BlockSpec((tm, d), lambda i, j: (i, 0)),
        out_shape=jax.ShapeDtypeStruct((n, d), F32),
        compiler_params=_params("arbitrary", "arbitrary"),
        name="ffn",
    )(h2, w_gu_tiles, w_gu_tiles, w_down)


def _ple_kernel(x1_ref, y_ref, p_ref, wg_ref, wp_ref, gain_ref, o_ref, x2_scr, *, tc, final_norm):
    d = x1_ref.shape[1]
    x2_scr[...] = x1_ref[...] + y_ref[...]
    xb = x2_scr[...].astype(BF16)
    pb = p_ref[...].astype(BF16)
    ssq = jnp.zeros((x1_ref.shape[0], 1), F32)
    for cb in range(d // tc):
        cs = slice(cb * tc, (cb + 1) * tc)
        gate = jax.nn.sigmoid(jnp.dot(xb, wg_ref[:, cs], preferred_element_type=F32))
        pe = jnp.dot(pb, wp_ref[:, cs], preferred_element_type=F32)
        x3 = x2_scr[:, cs] + gate * pe
        o_ref[:, cs] = x3
        ssq = ssq + jnp.sum(x3 * x3, axis=-1, keepdims=True)
    if not final_norm:
        return
    inv = lax.rsqrt(ssq * (1.0 / d) + NORM_EPS)
    for cb in range(d // tc):
        cs = slice(cb * tc, (cb + 1) * tc)
        o_ref[:, cs] = o_ref[:, cs] * inv * gain_ref[:, cs]


def _ple(x1, y, p2d, wg, wp, gain, *, final_norm, tm=512, tc=512):
    n, d = x1.shape
    row = lambda i: (i, 0)
    return pl.pallas_call(
        functools.partial(_ple_kernel, tc=tc, final_norm=final_norm),
        grid=(n // tm,),
        in_specs=[
            pl.BlockSpec((tm, d), row),
            pl.BlockSpec((tm, d), row),
            pl.BlockSpec((tm, p2d.shape[1]), row),
            _resident(wg.shape), _resident(wp.shape), _resident(gain.shape),
        ],
        out_specs=pl.BlockSpec((tm, d), row),
        out_shape=jax.ShapeDtypeStruct((n, d), F32),
        scratch_shapes=[pltpu.VMEM((tm, d), F32)],
        compiler_params=_params("arbitrary"),
        name="ple",
    )(x1, y, p2d, wg, wp, gain)


def kernel(x, p, norm_mix, w_in, hgrn_norm, w_proj_a, w_proj_b, w_out, norm_ffn, w_gate_up, w_down, w_ple,
           w_ple_gate, rel_bias, hgrn_lb_logits, norm_final):
    bsz, seq, d = x.shape
    n = bsz * seq
    depth = w_in.shape[0]
    a_heads = A_WIDTH // HEAD_DIM
    b_heads = B_WIDTH // HEAD_DIM

    cscale = np.ones((1, IN_WIDTH), np.float32)
    cscale[:, COL_QA * SEG:(COL_QA + 1) * SEG] = LOG2E / math.sqrt(HEAD_DIM)
    cscale = jnp.asarray(cscale)

    xc = x.reshape(n, d)
    for i in range(depth):
        w_in_tiles = w_in[i].astype(BF16).reshape(d, IN_WIDTH // SEG, SEG).transpose(1, 0, 2)
        proj, fb = _in_proj(xc, norm_mix[i][None], w_in_tiles, cscale)
        proj3 = proj.reshape(bsz, seq, IN_WIDTH)
        ya, (wa, wb, wo) = _moba(proj3, rel_bias, (w_proj_a[i], w_proj_b[i], w_out[i]), heads=a_heads)
        yb, (wgu, wdn) = _hgrn(proj3, fb.reshape(bsz, seq, SEG), hgrn_lb_logits, hgrn_norm[i][None],
                               (w_gate_up[i], w_down[i]), (2 * D_FF // FFN_TILE, 1), heads=b_heads, layer=i)
        x1, h2, (wpg, wpl) = _mix(ya.reshape(n, A_WIDTH), yb.reshape(n, B_WIDTH), proj, xc, wa, wb, wo,
                                  norm_ffn[i][None], (w_ple_gate[i], w_ple[i]))
        y = _ffn(h2, wgu, wdn)
        xc = _ple(x1, y, p[i].reshape(n, -1), wpg, wpl, norm_final[None], final_norm=(i == depth - 1))
    return xc.reshape(bsz, seq, d)
```

```python
import functools
import math

import jax
import jax.numpy as jnp
import numpy as np
from jax import lax
from jax.experimental import pallas as pl
from jax.experimental.pallas import tpu as pltpu

F32 = jnp.float32
BF16 = jnp.bfloat16

D_MODEL = 2048
HEAD_DIM = 128
A_WIDTH = 1024
B_WIDTH = 1024
MOBA_BLOCK = 256
MOBA_TOPK = 3
REL_BUCKETS = 32
REL_MAX_EXACT = 16
REL_MAX_DIST = 128
HGRN_CHUNK = 64
D_FF = 5632
NORM_EPS = 1e-6

COL_QA, COL_KA, COL_VA, COL_QB, COL_FB, COL_IB, COL_GB, COL_GATE_A, COL_GATE_B = 0, 1, 2, 3, 4, 5, 6, 7, 9
IN_WIDTH = 11 * 1024
SEG = 1024

BF16_SUBLANES = 16
MASK_VALUE = -1e30
VMEM_LIMIT = 56 * 1024 * 1024

NT_DIMS = (((1,), (1,)), ((), ()))
LOG2E = math.log2(math.e)


def _bucket_thresholds():
    n = np.arange(0, 4096, dtype=np.int64)
    nf = np.maximum(n, REL_MAX_EXACT).astype(np.float64)
    large = REL_MAX_EXACT + (np.log(nf / REL_MAX_EXACT) / math.log(REL_MAX_DIST / REL_MAX_EXACT)
                             * (REL_BUCKETS - REL_MAX_EXACT)).astype(np.int64)
    large = np.minimum(large, REL_BUCKETS - 1)
    bucket = np.where(n < REL_MAX_EXACT, n, large)
    return [int(np.argmax(bucket >= k)) for k in range(REL_BUCKETS)]


BUCKET_LO = _bucket_thresholds()


def _params(*sem):
    return pltpu.CompilerParams(dimension_semantics=sem, vmem_limit_bytes=VMEM_LIMIT)


def _resident(shape):
    return pl.BlockSpec(shape, lambda *_: (0,) * len(shape), pipeline_mode=pl.Buffered(1))


def _cast_specs(weights, n_steps, step_index):
    specs = []
    for w in weights:
        r = BF16_SUBLANES
        while w.shape[0] % r or w.shape[0] // r > n_steps:
            r += BF16_SUBLANES
        last = w.shape[0] // r - 1
        specs.append(pl.BlockSpec((r, w.shape[1]), lambda *ids, last=last: (jnp.minimum(step_index(*ids), last), 0)))
    return specs


def _cast_slabs(w32_refs, w16_refs):
    for w32, w16 in zip(w32_refs, w16_refs):
        w16[...] = w32[...].astype(BF16)


def _in_proj_kernel(x_ref, gain_ref, w_ref, cscale_ref, *rest, rows_per_step, first_tile):
    if first_tile:
        o_ref, fb_ref, w16_ref, h_scr = rest
    else:
        _, _, o_ref, fb_ref, h_scr = rest
    j = pl.program_id(1)

    @pl.when(j == 0)
    def _():
        def body(r, _):
            rs = pl.ds(pl.multiple_of(r * rows_per_step, rows_per_step), rows_per_step)
            x = x_ref[rs, :]
            ms = jnp.mean(x * x, axis=-1, keepdims=True)
            h_scr[rs, :] = (x * lax.rsqrt(ms + NORM_EPS) * gain_ref[...]).astype(BF16)
            return 0
        lax.fori_loop(0, x_ref.shape[0] // rows_per_step, body, 0)

    if first_tile:
        w = w_ref[...].astype(BF16)
        w16_ref[...] = w
    else:
        w = w_ref[...]
    acc = jnp.dot(h_scr[...], w, preferred_element_type=F32)
    o_ref[...] = (acc * cscale_ref[...]).astype(BF16)
    pltpu.store(fb_ref, acc, mask=jnp.broadcast_to(j == COL_FB, acc.shape))


def _in_proj(x2d, gain, w_f32, cscale, *, tm=1024, tn=SEG):
    n, d = x2d.shape
    width = w_f32.shape[1]
    nj = width // tn
    out_shape = [jax.ShapeDtypeStruct((n, width), BF16), jax.ShapeDtypeStruct((n, SEG), F32)]
    scratch = [pltpu.VMEM((tm, d), BF16)]

    def specs(row0, x_mode):
        ins = [
            pl.BlockSpec((tm, d), lambda i, j: (i + row0, 0), **x_mode),
            pl.BlockSpec((1, d), lambda i, j: (0, 0)),
            pl.BlockSpec((d, tn), lambda i, j: (0, j)),
            pl.BlockSpec((1, tn), lambda i, j: (0, j)),
        ]
        outs = [
            pl.BlockSpec((tm, tn), lambda i, j: (i + row0, j)),
            pl.BlockSpec((tm, SEG), lambda i, j: (i + row0, 0)),
        ]
        return ins, outs

    ins, outs = specs(0, dict(pipeline_mode=pl.Buffered(1)))
    proj, fb, w16 = pl.pallas_call(
        functools.partial(_in_proj_kernel, rows_per_step=128, first_tile=True),
        grid=(1, nj),
        in_specs=ins,
        out_specs=outs + [pl.BlockSpec((d, tn), lambda i, j: (0, j))],
        out_shape=out_shape + [jax.ShapeDtypeStruct((d, width), BF16)],
        scratch_shapes=scratch,
        compiler_params=_params("arbitrary", "arbitrary"),
        name="in_proj_first",
    )(x2d, gain, w_f32, cscale)

    ins, outs = specs(1, {})
    any_spec = pl.BlockSpec(memory_space=pl.ANY)
    return pl.pallas_call(
        functools.partial(_in_proj_kernel, rows_per_step=128, first_tile=False),
        grid=(n // tm - 1, nj),
        in_specs=ins + [any_spec, any_spec],
        out_specs=outs,
        out_shape=out_shape,
        input_output_aliases={4: 0, 5: 1},
        scratch_shapes=scratch,
        compiler_params=_params("arbitrary", "arbitrary"),
        name="in_proj",
    )(x2d, gain, w16, cscale, proj, fb)


def _moba_kernel(rb_ref, qall_ref, q_ref, k_ref, v_ref, *rest, n_blocks, group, n_cast):
    w32_refs = rest[:n_cast]
    o_ref = rest[n_cast]
    w16_refs = rest[n_cast + 1:2 * n_cast + 1]
    bias_scr, vt_scr, sel_scr, s_scr, p_scr = rest[2 * n_cast + 1:]
    hg = pl.program_id(0)
    b = pl.program_id(1)
    qi = pl.program_id(2)
    blk = MOBA_BLOCK
    dh = HEAD_DIM
    sub = 8

    @pl.when((b == 0) & (qi == 0))
    def _build_bias():
        r = lax.broadcasted_iota(jnp.int32, (blk, blk), 1)
        c = lax.broadcasted_iota(jnp.int32, (blk, blk), 0)
        for g in range(group):
            hd = hg * group + g
            for d in range(2):
                rel = d * blk + r - c
                bias = jnp.full((blk, blk), rb_ref[REL_BUCKETS - 1, hd] * LOG2E, F32)
                for kk in range(REL_BUCKETS - 2, -1, -1):
                    bias = jnp.where(rel < BUCKET_LO[kk + 1], rb_ref[kk, hd] * LOG2E, bias)
                if d == 0:
                    bias = jnp.where(rel < 0, MASK_VALUE, bias)
                bias_scr[g, d] = bias

    @pl.when(qi == 0)
    def _per_sequence():
        for g in range(group):
            hs = slice(g * dh, (g + 1) * dh)
            means = []
            for jb in range(n_blocks):
                rs = slice(jb * blk, (jb + 1) * blk)
                means.append(jnp.mean(k_ref[0, rs, hs].astype(F32), axis=0, keepdims=True))
                vt_scr[g, :, rs] = v_ref[0, rs, hs].astype(F32).T.astype(BF16)
            km = jnp.concatenate(means + [jnp.zeros((16 - n_blocks, dh), F32)], axis=0)

            km_hi = km.astype(BF16)
            km_lo = (km - km_hi.astype(F32)).astype(BF16)
            qall = qall_ref[0, :, hs]
            sc = (lax.dot_general(km_hi, qall, NT_DIMS, preferred_element_type=F32)
                  + lax.dot_general(km_lo, qall, NT_DIMS, preferred_element_type=F32))
            rows = lax.broadcasted_iota(jnp.int32, sc.shape, 0)
            rows_f = rows.astype(F32)
            past = rows < lax.broadcasted_iota(jnp.int32, sc.shape, 1) // blk
            s = jnp.where(past, sc, -jnp.inf)
            picked = jnp.zeros(sc.shape, jnp.bool_)
            for _ in range(MOBA_TOPK):
                top = jnp.max(s, axis=0, keepdims=True)
                first = jnp.min(jnp.where(s == top, rows_f, 1e9), axis=0, keepdims=True)
                hit = rows_f == first
                picked = picked | hit
                s = jnp.where(hit, -jnp.inf, s)
            mask = jnp.where(picked & past, 0.0, MASK_VALUE)
            for qb in range(n_blocks):
                sel_scr[g, qb] = mask[:, qb * blk:(qb + 1) * blk]

    def attend(nt):
        _cast_slabs(w32_refs, w16_refs)
        hss = [slice(g * dh, (g + 1) * dh) for g in range(group)]
        ms = []
        for g in range(group):
            far = rb_ref[REL_BUCKETS - 1, hg * group + g] * LOG2E
            s = lax.dot_general(k_ref[0, :nt * blk, hss[g]], q_ref[0, :, hss[g]], NT_DIMS,
                                preferred_element_type=F32)
            m8 = None
            for j in range(nt):
                rs = slice(j * blk, (j + 1) * blk)
                d = nt - 1 - j
                if d == 0:
                    sj = s[rs] + bias_scr[g, 0]
                elif d == 1:
                    sj = s[rs] + bias_scr[g, 1] + sel_scr[g, qi, j:j + 1, :]
                else:
                    sj = s[rs] + (sel_scr[g, qi, j:j + 1, :] + far)
                s_scr[g, rs, :] = sj
                mj = jnp.max(sj.reshape(blk // sub, sub, blk), axis=0)
                m8 = mj if m8 is None else jnp.maximum(m8, mj)
            ms.append(jnp.max(m8, axis=0, keepdims=True))
        ls = []
        for g in range(group):
            l8 = jnp.zeros((sub, blk), F32)
            for j in range(nt):
                rs = slice(j * blk, (j + 1) * blk)
                p = jnp.exp2(s_scr[g, rs, :] - ms[g])
                l8 = l8 + jnp.sum(p.reshape(blk // sub, sub, blk), axis=0)
                p_scr[g, rs, :] = p.astype(BF16)
            ls.append(jnp.sum(l8, axis=0, keepdims=True))
        for g in range(group):
            acc = jnp.dot(vt_scr[g, :, :nt * blk], p_scr[g, :nt * blk, :], preferred_element_type=F32)
            o_ref[0, :, hss[g]] = (acc / ls[g]).T.astype(o_ref.dtype)

    for nt in range(1, n_blocks + 1):
        pl.when(qi == nt - 1)(functools.partial(attend, nt))


def _moba(proj3, rel_bias, cast_weights, *, heads, group=4):
    bsz, seq, _ = proj3.shape
    nb = seq // MOBA_BLOCK
    gw = group * HEAD_DIM
    cq, ck, cv = (c * (SEG // gw) for c in (COL_QA, COL_KA, COL_VA))
    cast_specs = _cast_specs(cast_weights, (heads // group) * bsz * nb, lambda h, b, i: (h * bsz + b) * nb + i)
    outs = pl.pallas_call(
        functools.partial(_moba_kernel, n_blocks=nb, group=group, n_cast=len(cast_weights)),
        grid=(heads // group, bsz, nb),
        in_specs=[
            pl.BlockSpec(memory_space=pltpu.SMEM),
            pl.BlockSpec((1, seq, gw), lambda h, b, i: (b, 0, cq + h)),
            pl.BlockSpec((1, MOBA_BLOCK, gw), lambda h, b, i: (b, i, cq + h)),
            pl.BlockSpec((1, seq, gw), lambda h, b, i: (b, 0, ck + h)),
            pl.BlockSpec((1, seq, gw), lambda h, b, i: (b, 0, cv + h)),
            *cast_specs,
        ],
        out_specs=[pl.BlockSpec((1, MOBA_BLOCK, gw), lambda h, b, i: (b, i, h)), *cast_specs],
        out_shape=[jax.ShapeDtypeStruct((bsz, seq, heads * HEAD_DIM), BF16),
                   *(jax.ShapeDtypeStruct(w.shape, BF16) for w in cast_weights)],
        scratch_shapes=[
            pltpu.VMEM((group, 2, MOBA_BLOCK, MOBA_BLOCK), F32),
            pltpu.VMEM((group, HEAD_DIM, seq), BF16),
            pltpu.VMEM((group, nb, 16, MOBA_BLOCK), F32),
            pltpu.VMEM((group, seq, MOBA_BLOCK), F32),
            pltpu.VMEM((group, seq, MOBA_BLOCK), BF16),
        ],
        compiler_params=_params("arbitrary", "arbitrary", "arbitrary"),
        name="moba",
    )(rel_bias, proj3, proj3, proj3, proj3, *cast_weights)
    return outs[0], outs[1:]


TN_DIMS = (((0,), (0,)), ((), ()))


def _cumsum_rows(x, period):
    pos = lax.broadcasted_iota(jnp.int32, x.shape, 0) % period
    sh = 1
    while sh < period:
        x = x + jnp.where(pos >= sh, pltpu.roll(x, sh, axis=0), 0.0)
        sh *= 2
    return x


def _hgrn_kernel(lbl_ref, gain_ref, q_ref, f_ref, i_ref, g_ref, *rest, heads, layer, chunks, n_cast):
    w32_refs = rest[:n_cast]
    o_ref = rest[n_cast]
    w16_refs = rest[n_cast + 1:2 * n_cast + 1]
    st_scr, = rest[2 * n_cast + 1:]
    _cast_slabs(w32_refs, w16_refs)
    ch = HGRN_CHUNK
    dh = HEAD_DIM
    rows = chunks * ch

    @pl.when(pl.program_id(1) == 0)
    def _():
        st_scr[...] = jnp.zeros_like(st_scr)

    lg = lbl_ref[...]
    e = jnp.exp(lg - jnp.max(lg, axis=0, keepdims=True))
    lb = jnp.sum(e[:layer + 1], axis=0, keepdims=True) / jnp.sum(e, axis=0, keepdims=True)

    f = lb + (1.0 - lb) * jax.nn.sigmoid(f_ref[0])
    bcum = _cumsum_rows(jnp.log(f), ch)
    q_dec = (q_ref[0].astype(F32) * jnp.exp(bcum)).astype(BF16)
    k_dec32 = (1.0 - f) * jnp.exp(-bcum)
    k_dec = k_dec32.astype(BF16)
    decays = [jnp.exp(bcum[(c + 1) * ch - 1:(c + 1) * ch, :]) for c in range(chunks)]
    k_end = [(k_dec32[c * ch:(c + 1) * ch, :] * decays[c]).astype(BF16) for c in range(chunks)]
    v = i_ref[0]
    gate = g_ref[0].astype(F32)
    gate = gate * jax.nn.sigmoid(gate)
    gain = gain_ref[...]

    r = lax.broadcasted_iota(jnp.int32, (rows, rows), 0)
    c_ = lax.broadcasted_iota(jnp.int32, (rows, rows), 1)
    causal_in_chunk = (r >= c_) & (r // ch == c_ // ch)

    for hd in range(heads):
        sl = slice(hd * dh, (hd + 1) * dh)
        qd, vh = q_dec[:, sl], v[:, sl]
        a = lax.dot_general(qd, k_dec[:, sl], NT_DIMS, preferred_element_type=F32)
        a = jnp.where(causal_in_chunk, a, 0.0).astype(BF16)
        o_intra = jnp.dot(a, vh, preferred_element_type=F32)
        st = st_scr[hd]
        outs = []
        for c in range(chunks):
            rs = slice(c * ch, (c + 1) * ch)
            outs.append(o_intra[rs] + lax.dot_general(qd[rs], st.astype(BF16), NT_DIMS, preferred_element_type=F32))
            st = st * decays[c][:, sl] + lax.dot_general(vh[rs], k_end[c][:, sl], TN_DIMS,
                                                         preferred_element_type=F32)
        st_scr[hd] = st
        o = jnp.concatenate(outs, axis=0)
        ms = jnp.mean(o * o, axis=-1, keepdims=True)
        y = o * lax.rsqrt(ms + NORM_EPS) * gain[:, sl]
        o_ref[0, :, sl] = (y * gate[:, sl]).astype(o_ref.dtype)


def _hgrn(proj3, fb3, lb_logits, gain, cast_weights, *, heads, layer, chunks=4):
    bsz, seq, _ = proj3.shape
    width = heads * HEAD_DIM
    rows = chunks * HGRN_CHUNK
    blk = (1, rows, width)
    nc = seq // rows
    cast_specs = _cast_specs(cast_weights, bsz * nc, lambda b, c: b * nc + c)
    outs = pl.pallas_call(
        functools.partial(_hgrn_kernel, heads=heads, layer=layer, chunks=chunks, n_cast=len(cast_weights)),
        grid=(bsz, nc),
        in_specs=[
            pl.BlockSpec(lb_logits.shape, lambda b, c: (0, 0)),
            pl.BlockSpec((1, width), lambda b, c: (0, 0)),
            pl.BlockSpec(blk, lambda b, c: (b, c, COL_QB)),
            pl.BlockSpec(blk, lambda b, c: (b, c, 0)),
            pl.BlockSpec(blk, lambda b, c: (b, c, COL_IB)),
            pl.BlockSpec(blk, lambda b, c: (b, c, COL_GB)),
            *cast_specs,
        ],
        out_specs=[pl.BlockSpec(blk, lambda b, c: (b, c, 0)), *cast_specs],
        out_shape=[jax.ShapeDtypeStruct((bsz, seq, width), BF16),
                   *(jax.ShapeDtypeStruct(w.shape, BF16) for w in cast_weights)],
        scratch_shapes=[pltpu.VMEM((heads, HEAD_DIM, HEAD_DIM), F32)],
        compiler_params=_params("arbitrary", "arbitrary"),
        name="hgrn2",
    )(lb_logits, gain, proj3, fb3, proj3, proj3, *cast_weights)
    return outs[0], outs[1:]


def _mix_kernel(ya_ref, yb_ref, ga0_ref, ga1_ref, gb0_ref, gb1_ref, x_ref, wa_ref, wb_ref, wo_ref, gain_ref,
                *rest, tc, n_cast):
    w32_refs = rest[:n_cast]
    x1_ref, h2_ref = rest[n_cast:n_cast + 2]
    w16_refs = rest[n_cast + 2:2 * n_cast + 2]
    mixed_scr, = rest[2 * n_cast + 2:]
    _cast_slabs(w32_refs, w16_refs)
    d = x_ref.shape[1]
    gate_refs = ((ga0_ref, gb0_ref), (ga1_ref, gb1_ref))
    ya = ya_ref[...]
    yb = yb_ref[...]
    for cb in range(d // tc):
        cs = slice(cb * tc, (cb + 1) * tc)
        ga_ref, gb_ref = gate_refs[(cb * tc) // SEG]
        gs = slice((cb * tc) % SEG, (cb * tc) % SEG + tc)
        pa = jnp.dot(ya, wa_ref[:, cs], preferred_element_type=F32)
        pb = jnp.dot(yb, wb_ref[:, cs], preferred_element_type=F32)
        mixed = (jax.nn.sigmoid(ga_ref[:, gs].astype(F32)) * pa
                 + jax.nn.sigmoid(gb_ref[:, gs].astype(F32)) * pb)
        mixed_scr[:, cs] = mixed.astype(BF16)
    ssq = jnp.zeros((x_ref.shape[0], 1), F32)
    for cb in range(d // tc):
        cs = slice(cb * tc, (cb + 1) * tc)
        x1 = x_ref[:, cs] + jnp.dot(mixed_scr[...], wo_ref[:, cs], preferred_element_type=F32)
        x1_ref[:, cs] = x1
        ssq = ssq + jnp.sum(x1 * x1, axis=-1, keepdims=True)
    inv = lax.rsqrt(ssq * (1.0 / d) + NORM_EPS)
    for cb in range(d // tc):
        cs = slice(cb * tc, (cb + 1) * tc)
        h2_ref[:, cs] = (x1_ref[:, cs] * inv * gain_ref[:, cs]).astype(BF16)


def _mix(ya, yb, proj, x2d, wa, wb, wo, gain, cast_weights, *, tm=512, tc=512):
    n, d = x2d.shape
    row = lambda i: (i, 0)
    seg = lambda c: pl.BlockSpec((tm, SEG), lambda i: (i, c))
    cast_specs = _cast_specs(cast_weights, n // tm, lambda i: i)
    outs = pl.pallas_call(
        functools.partial(_mix_kernel, tc=tc, n_cast=len(cast_weights)),
        grid=(n // tm,),
        in_specs=[
            pl.BlockSpec((tm, A_WIDTH), row),
            pl.BlockSpec((tm, B_WIDTH), row),
            seg(COL_GATE_A), seg(COL_GATE_A + 1), seg(COL_GATE_B), seg(COL_GATE_B + 1),
            pl.BlockSpec((tm, d), row),
            _resident(wa.shape), _resident(wb.shape), _resident(wo.shape), _resident(gain.shape),
            *cast_specs,
        ],
        out_specs=[pl.BlockSpec((tm, d), row), pl.BlockSpec((tm, d), row), *cast_specs],
        out_shape=[jax.ShapeDtypeStruct((n, d), F32), jax.ShapeDtypeStruct((n, d), BF16),
                   *(jax.ShapeDtypeStruct(w.shape, BF16) for w in cast_weights)],
        scratch_shapes=[pltpu.VMEM((tm, d), BF16)],
        compiler_params=_params("arbitrary"),
        name="mix",
    )(ya, yb, proj, proj, proj, proj, x2d, wa, wb, wo, gain, *cast_weights)
    return outs[0], outs[1], outs[2:]


def _ffn_kernel(h_ref, wg_ref, wu_ref, wd_ref, o_ref):
    @pl.when(pl.program_id(1) == 0)
    def _():
        o_ref[...] = jnp.zeros_like(o_ref)

    h = h_ref[...]
    g = jnp.dot(h, wg_ref[...], preferred_element_type=F32)
    u = jnp.dot(h, wu_ref[...], preferred_element_type=F32)
    act = (g * jax.nn.sigmoid(g) * u).astype(BF16)
    o_ref[...] += jnp.dot(act, wd_ref[...], preferred_element_type=F32)


def _ffn(h2, w_gu, w_down, *, tm=1024, tf=512):
    n, d = h2.shape
    dff = w_down.shape[0]
    nj = dff // tf
    return pl.pallas_call(
        _ffn_kernel,
        grid=(n // tm, nj),
        in_specs=[
            pl.BlockSpec((tm, d), lambda i, j: (i, 0)),
            pl.BlockSpec((d, tf), lambda i, j: (0, j)),
            pl.BlockSpec((d, tf), lambda i, j: (0, j + nj)),
            pl.BlockSpec((tf, d), lambda i, j: (j, 0)),
        ],
        out_specs=pl.BlockSpec((tm, d), lambda i, j: (i, 0)),
        out_shape=jax.ShapeDtypeStruct((n, d), F32),
        compiler_params=_params("arbitrary", "arbitrary"),
        name="ffn",
    )(h2, w_gu, w_gu, w_down)


def _ple_kernel(x1_ref, y_ref, p_ref, wg_ref, wp_ref, gain_ref, o_ref, x2_scr, *, tc, final_norm):
    d = x1_ref.shape[1]
    x2_scr[...] = x1_ref[...] + y_ref[...]
    xb = x2_scr[...].astype(BF16)
    pb = p_ref[...].astype(BF16)
    ssq = jnp.zeros((x1_ref.shape[0], 1), F32)
    for cb in range(d // tc):
        cs = slice(cb * tc, (cb + 1) * tc)
        gate = jax.nn.sigmoid(jnp.dot(xb, wg_ref[:, cs], preferred_element_type=F32))
        pe = jnp.dot(pb, wp_ref[:, cs], preferred_element_type=F32)
        x3 = x2_scr[:, cs] + gate * pe
        o_ref[:, cs] = x3
        ssq = ssq + jnp.sum(x3 * x3, axis=-1, keepdims=True)
    if not final_norm:
        return
    inv = lax.rsqrt(ssq * (1.0 / d) + NORM_EPS)
    for cb in range(d // tc):
        cs = slice(cb * tc, (cb + 1) * tc)
        o_ref[:, cs] = o_ref[:, cs] * inv * gain_ref[:, cs]


def _ple(x1, y, p2d, wg, wp, gain, *, final_norm, tm=512, tc=512):
    n, d = x1.shape
    row = lambda i: (i, 0)
    return pl.pallas_call(
        functools.partial(_ple_kernel, tc=tc, final_norm=final_norm),
        grid=(n // tm,),
        in_specs=[
            pl.BlockSpec((tm, d), row),
            pl.BlockSpec((tm, d), row),
            pl.BlockSpec((tm, p2d.shape[1]), row),
            _resident(wg.shape), _resident(wp.shape), _resident(gain.shape),
        ],
        out_specs=pl.BlockSpec((tm, d), row),
        out_shape=jax.ShapeDtypeStruct((n, d), F32),
        scratch_shapes=[pltpu.VMEM((tm, d), F32)],
        compiler_params=_params("arbitrary"),
        name="ple",
    )(x1, y, p2d, wg, wp, gain)


def kernel(x, p, norm_mix, w_in, hgrn_norm, w_proj_a, w_proj_b, w_out, norm_ffn, w_gate_up, w_down, w_ple,
           w_ple_gate, rel_bias, hgrn_lb_logits, norm_final):
    bsz, seq, d = x.shape
    n = bsz * seq
    depth = w_in.shape[0]
    a_heads = A_WIDTH // HEAD_DIM
    b_heads = B_WIDTH // HEAD_DIM

    cscale = np.ones((1, IN_WIDTH), np.float32)
    cscale[:, COL_QA * SEG:(COL_QA + 1) * SEG] = LOG2E / math.sqrt(HEAD_DIM)
    cscale = jnp.asarray(cscale)

    xc = x.reshape(n, d)
    for i in range(depth):
        proj, fb = _in_proj(xc, norm_mix[i][None], w_in[i], cscale)
        proj3 = proj.reshape(bsz, seq, IN_WIDTH)
        ya, (wa, wb, wo) = _moba(proj3, rel_bias, (w_proj_a[i], w_proj_b[i], w_out[i]), heads=a_heads)
        yb, (wgu, wdn) = _hgrn(proj3, fb.reshape(bsz, seq, SEG), hgrn_lb_logits, hgrn_norm[i][None],
                               (w_gate_up[i], w_down[i]), heads=b_heads, layer=i)
        x1, h2, (wpg, wpl) = _mix(ya.reshape(n, A_WIDTH), yb.reshape(n, B_WIDTH), proj, xc, wa, wb, wo,
                                  norm_ffn[i][None], (w_ple_gate[i], w_ple[i]))
        y = _ffn(h2, wgu, wdn)
        xc = _ple(x1, y, p[i].reshape(n, -1), wpg, wpl, norm_final[None], final_norm=(i == depth - 1))
    return xc.reshape(bsz, seq, d)
```

```python
import functools
import math

import jax
import jax.numpy as jnp
import numpy as np
from jax import lax
from jax.experimental import pallas as pl
from jax.experimental.pallas import tpu as pltpu

F32 = jnp.float32
BF16 = jnp.bfloat16

D_MODEL = 2048
HEAD_DIM = 128
A_WIDTH = 1024
B_WIDTH = 1024
MOBA_BLOCK = 256
MOBA_TOPK = 3
REL_BUCKETS = 32
REL_MAX_EXACT = 16
REL_MAX_DIST = 128
HGRN_CHUNK = 64
D_FF = 5632
NORM_EPS = 1e-6

COL_QA, COL_KA, COL_VA, COL_QB, COL_FB, COL_IB, COL_GB, COL_GATE_A, COL_GATE_B = 0, 1, 2, 3, 4, 5, 6, 7, 9
IN_WIDTH = 11 * 1024
SEG = 1024

BF16_SUBLANES = 16
MASK_VALUE = -1e30
VMEM_LIMIT = 56 * 1024 * 1024

NT_DIMS = (((1,), (1,)), ((), ()))
LOG2E = math.log2(math.e)


def _bucket_thresholds():
    n = np.arange(0, 4096, dtype=np.int64)
    nf = np.maximum(n, REL_MAX_EXACT).astype(np.float64)
    large = REL_MAX_EXACT + (np.log(nf / REL_MAX_EXACT) / math.log(REL_MAX_DIST / REL_MAX_EXACT)
                             * (REL_BUCKETS - REL_MAX_EXACT)).astype(np.int64)
    large = np.minimum(large, REL_BUCKETS - 1)
    bucket = np.where(n < REL_MAX_EXACT, n, large)
    return [int(np.argmax(bucket >= k)) for k in range(REL_BUCKETS)]


BUCKET_LO = _bucket_thresholds()


def _params(*sem):
    return pltpu.CompilerParams(dimension_semantics=sem, vmem_limit_bytes=VMEM_LIMIT)


def _resident(shape):
    return pl.BlockSpec(shape, lambda *_: (0,) * len(shape), pipeline_mode=pl.Buffered(1))


def _cast_specs(weights, n_steps, step_index):
    specs = []
    for w in weights:
        r = BF16_SUBLANES
        while w.shape[0] % r or w.shape[0] // r > n_steps:
            r += BF16_SUBLANES
        last = w.shape[0] // r - 1
        specs.append(pl.BlockSpec((r, w.shape[1]), lambda *ids, last=last: (jnp.minimum(step_index(*ids), last), 0)))
    return specs


def _cast_slabs(w32_refs, w16_refs):
    for w32, w16 in zip(w32_refs, w16_refs):
        w16[...] = w32[...].astype(BF16)


def _in_proj_kernel(x_ref, gain_ref, w_ref, cscale_ref, *rest, rows_per_step, first_tile):
    if first_tile:
        o_ref, fb_ref, w16_ref, h_scr = rest
    else:
        _, _, o_ref, fb_ref, h_scr = rest
    j = pl.program_id(1)

    @pl.when(j == 0)
    def _():
        def body(r, _):
            rs = pl.ds(pl.multiple_of(r * rows_per_step, rows_per_step), rows_per_step)
            x = x_ref[rs, :]
            ms = jnp.mean(x * x, axis=-1, keepdims=True)
            h_scr[rs, :] = (x * lax.rsqrt(ms + NORM_EPS) * gain_ref[...]).astype(BF16)
            return 0
        lax.fori_loop(0, x_ref.shape[0] // rows_per_step, body, 0)

    if first_tile:
        w = w_ref[...].astype(BF16)
        w16_ref[...] = w
    else:
        w = w_ref[...]
    acc = jnp.dot(h_scr[...], w, preferred_element_type=F32)
    o_ref[...] = (acc * cscale_ref[...]).astype(BF16)
    pltpu.store(fb_ref, acc, mask=jnp.broadcast_to(j == COL_FB, acc.shape))


def _in_proj(x2d, gain, w_f32, cscale, *, tm=1024, tn=SEG):
    n, d = x2d.shape
    width = w_f32.shape[1]
    nj = width // tn
    out_shape = [jax.ShapeDtypeStruct((n, width), BF16), jax.ShapeDtypeStruct((n, SEG), F32)]
    scratch = [pltpu.VMEM((tm, d), BF16)]

    def specs(row0, x_mode):
        ins = [
            pl.BlockSpec((tm, d), lambda i, j: (i + row0, 0), **x_mode),
            pl.BlockSpec((1, d), lambda i, j: (0, 0)),
            pl.BlockSpec((d, tn), lambda i, j: (0, j)),
            pl.BlockSpec((1, tn), lambda i, j: (0, j)),
        ]
        outs = [
            pl.BlockSpec((tm, tn), lambda i, j: (i + row0, j)),
            pl.BlockSpec((tm, SEG), lambda i, j: (i + row0, 0)),
        ]
        return ins, outs

    ins, outs = specs(0, dict(pipeline_mode=pl.Buffered(1)))
    proj, fb, w16 = pl.pallas_call(
        functools.partial(_in_proj_kernel, rows_per_step=128, first_tile=True),
        grid=(1, nj),
        in_specs=ins,
        out_specs=outs + [pl.BlockSpec((d, tn), lambda i, j: (0, j))],
        out_shape=out_shape + [jax.ShapeDtypeStruct((d, width), BF16)],
        scratch_shapes=scratch,
        compiler_params=_params("arbitrary", "arbitrary"),
        name="in_proj_first",
    )(x2d, gain, w_f32, cscale)

    ins, outs = specs(1, {})
    any_spec = pl.BlockSpec(memory_space=pl.ANY)
    return pl.pallas_call(
        functools.partial(_in_proj_kernel, rows_per_step=128, first_tile=False),
        grid=(n // tm - 1, nj),
        in_specs=ins + [any_spec, any_spec],
        out_specs=outs,
        out_shape=out_shape,
        input_output_aliases={4: 0, 5: 1},
        scratch_shapes=scratch,
        compiler_params=_params("arbitrary", "arbitrary"),
        name="in_proj",
    )(x2d, gain, w16, cscale, proj, fb)


def _moba_kernel(rb_ref, qall_ref, q_ref, k_ref, v_ref, *rest, n_blocks, group, n_cast):
    w32_refs = rest[:n_cast]
    o_ref = rest[n_cast]
    w16_refs = rest[n_cast + 1:2 * n_cast + 1]
    bias_scr, vt_scr, sel_scr, s_scr = rest[2 * n_cast + 1:]
    hg = pl.program_id(0)
    b = pl.program_id(1)
    qi = pl.program_id(2)
    blk = MOBA_BLOCK
    dh = HEAD_DIM
    sub = 8

    @pl.when((b == 0) & (qi == 0))
    def _build_bias():
        r = lax.broadcasted_iota(jnp.int32, (blk, blk), 1)
        c = lax.broadcasted_iota(jnp.int32, (blk, blk), 0)
        for g in range(group):
            hd = hg * group + g
            for d in range(2):
                rel = d * blk + r - c
                bias = jnp.full((blk, blk), rb_ref[REL_BUCKETS - 1, hd] * LOG2E, F32)
                for kk in range(REL_BUCKETS - 2, -1, -1):
                    bias = jnp.where(rel < BUCKET_LO[kk + 1], rb_ref[kk, hd] * LOG2E, bias)
                if d == 0:
                    bias = jnp.where(rel < 0, MASK_VALUE, bias)
                bias_scr[g, d] = bias

    @pl.when(qi == 0)
    def _per_sequence():
        for g in range(group):
            hs = slice(g * dh, (g + 1) * dh)
            means = []
            for jb in range(n_blocks):
                rs = slice(jb * blk, (jb + 1) * blk)
                means.append(jnp.mean(k_ref[0, rs, hs].astype(F32), axis=0, keepdims=True))
                vt_scr[g, :, rs] = v_ref[0, rs, hs].T
            km = jnp.concatenate(means + [jnp.zeros((16 - n_blocks, dh), F32)], axis=0)

            km_hi = km.astype(BF16)
            km_lo = (km - km_hi.astype(F32)).astype(BF16)
            qall = qall_ref[0, :, hs]
            sc = (lax.dot_general(km_hi, qall, NT_DIMS, preferred_element_type=F32)
                  + lax.dot_general(km_lo, qall, NT_DIMS, preferred_element_type=F32))
            rows = lax.broadcasted_iota(jnp.int32, sc.shape, 0)
            rows_f = rows.astype(F32)
            past = rows < lax.broadcasted_iota(jnp.int32, sc.shape, 1) // blk
            s = jnp.where(past, sc, -jnp.inf)
            picked = jnp.zeros(sc.shape, jnp.bool_)
            for _ in range(MOBA_TOPK):
                top = jnp.max(s, axis=0, keepdims=True)
                first = jnp.min(jnp.where(s == top, rows_f, 1e9), axis=0, keepdims=True)
                hit = rows_f == first
                picked = picked | hit
                s = jnp.where(hit, -jnp.inf, s)
            mask = jnp.where(picked & past, 0.0, MASK_VALUE)
            for qb in range(n_blocks):
                sel_scr[g, qb] = mask[:, qb * blk:(qb + 1) * blk]

    def attend(nt):
        _cast_slabs(w32_refs, w16_refs)
        hss = [slice(g * dh, (g + 1) * dh) for g in range(group)]
        ms = []
        for g in range(group):
            far = rb_ref[REL_BUCKETS - 1, hg * group + g] * LOG2E
            s = lax.dot_general(k_ref[0, :nt * blk, hss[g]], q_ref[0, :, hss[g]], NT_DIMS,
                                preferred_element_type=F32)
            m8 = None
            for j in range(nt):
                rs = slice(j * blk, (j + 1) * blk)
                d = nt - 1 - j
                if d == 0:
                    sj = s[rs] + bias_scr[g, 0]
                elif d == 1:
                    sj = s[rs] + bias_scr[g, 1] + sel_scr[g, qi, j:j + 1, :]
                else:
                    sj = s[rs] + (sel_scr[g, qi, j:j + 1, :] + far)
                s_scr[g, rs, :] = sj
                mj = jnp.max(sj.reshape(blk // sub, sub, blk), axis=0)
                m8 = mj if m8 is None else jnp.maximum(m8, mj)
            ms.append(jnp.max(m8, axis=0, keepdims=True))
        for g in range(group):
            l8 = jnp.zeros((sub, blk), F32)
            acc = None
            for j in range(nt):
                rs = slice(j * blk, (j + 1) * blk)
                p = jnp.exp2(s_scr[g, rs, :] - ms[g])
                l8 = l8 + jnp.sum(p.reshape(blk // sub, sub, blk), axis=0)
                pv = jnp.dot(vt_scr[g, :, rs], p.astype(BF16), preferred_element_type=F32)
                acc = pv if acc is None else acc + pv
            l = jnp.sum(l8, axis=0, keepdims=True)
            o_ref[0, :, hss[g]] = (acc / l).T.astype(o_ref.dtype)

    for nt in range(1, n_blocks + 1):
        pl.when(qi == nt - 1)(functools.partial(attend, nt))


def _moba(proj3, rel_bias, cast_weights, *, heads, group=4):
    bsz, seq, _ = proj3.shape
    nb = seq // MOBA_BLOCK
    gw = group * HEAD_DIM
    cq, ck, cv = (c * (SEG // gw) for c in (COL_QA, COL_KA, COL_VA))
    cast_specs = _cast_specs(cast_weights, (heads // group) * bsz * nb, lambda h, b, i: (h * bsz + b) * nb + i)
    outs = pl.pallas_call(
        functools.partial(_moba_kernel, n_blocks=nb, group=group, n_cast=len(cast_weights)),
        grid=(heads // group, bsz, nb),
        in_specs=[
            pl.BlockSpec(memory_space=pltpu.SMEM),
            pl.BlockSpec((1, seq, gw), lambda h, b, i: (b, 0, cq + h)),
            pl.BlockSpec((1, MOBA_BLOCK, gw), lambda h, b, i: (b, i, cq + h)),
            pl.BlockSpec((1, seq, gw), lambda h, b, i: (b, 0, ck + h)),
            pl.BlockSpec((1, seq, gw), lambda h, b, i: (b, 0, cv + h)),
            *cast_specs,
        ],
        out_specs=[pl.BlockSpec((1, MOBA_BLOCK, gw), lambda h, b, i: (b, i, h)), *cast_specs],
        out_shape=[jax.ShapeDtypeStruct((bsz, seq, heads * HEAD_DIM), BF16),
                   *(jax.ShapeDtypeStruct(w.shape, BF16) for w in cast_weights)],
        scratch_shapes=[
            pltpu.VMEM((group, 2, MOBA_BLOCK, MOBA_BLOCK), F32),
            pltpu.VMEM((group, HEAD_DIM, seq), BF16),
            pltpu.VMEM((group, nb, 16, MOBA_BLOCK), F32),
            pltpu.VMEM((group, seq, MOBA_BLOCK), F32),
        ],
        compiler_params=_params("arbitrary", "arbitrary", "arbitrary"),
        name="moba",
    )(rel_bias, proj3, proj3, proj3, proj3, *cast_weights)
    return outs[0], outs[1:]


TN_DIMS = (((0,), (0,)), ((), ()))


def _cumsum_rows(x, period):
    pos = lax.broadcasted_iota(jnp.int32, x.shape, 0) % period
    sh = 1
    while sh < period:
        x = x + jnp.where(pos >= sh, pltpu.roll(x, sh, axis=0), 0.0)
        sh *= 2
    return x


def _hgrn_kernel(lbl_ref, gain_ref, q_ref, f_ref, i_ref, g_ref, *rest, heads, layer, chunks, n_cast):
    w32_refs = rest[:n_cast]
    o_ref = rest[n_cast]
    w16_refs = rest[n_cast + 1:2 * n_cast + 1]
    st_scr, = rest[2 * n_cast + 1:]
    _cast_slabs(w32_refs, w16_refs)
    ch = HGRN_CHUNK
    dh = HEAD_DIM
    rows = chunks * ch

    @pl.when(pl.program_id(1) == 0)
    def _():
        st_scr[...] = jnp.zeros_like(st_scr)

    lg = lbl_ref[...]
    e = jnp.exp(lg - jnp.max(lg, axis=0, keepdims=True))
    lb = jnp.sum(e[:layer + 1], axis=0, keepdims=True) / jnp.sum(e, axis=0, keepdims=True)

    half = 0.5 * (1.0 - lb)
    f = (lb + half) + half * jnp.tanh(0.5 * f_ref[0])
    bcum = _cumsum_rows(jnp.log(f), ch)
    q_dec = (q_ref[0].astype(F32) * jnp.exp(bcum)).astype(BF16)
    k_dec32 = (1.0 - f) * jnp.exp(-bcum)
    k_dec = k_dec32.astype(BF16)
    decays = [jnp.exp(bcum[(c + 1) * ch - 1:(c + 1) * ch, :]) for c in range(chunks)]
    k_end = [(k_dec32[c * ch:(c + 1) * ch, :] * decays[c]).astype(BF16) for c in range(chunks)]
    v = i_ref[0]
    hg = 0.5 * g_ref[0].astype(F32)
    gate_gain = (hg + hg * jnp.tanh(hg)) * gain_ref[...]

    r = lax.broadcasted_iota(jnp.int32, (rows, rows), 0)
    c_ = lax.broadcasted_iota(jnp.int32, (rows, rows), 1)
    causal_in_chunk = (r >= c_) & (r // ch == c_ // ch)

    for hd in range(heads):
        sl = slice(hd * dh, (hd + 1) * dh)
        qd, vh = q_dec[:, sl], v[:, sl]
        a = lax.dot_general(qd, k_dec[:, sl], NT_DIMS, preferred_element_type=F32)
        a = jnp.where(causal_in_chunk, a, 0.0).astype(BF16)
        o_intra = jnp.dot(a, vh, preferred_element_type=F32)
        st = st_scr[hd]
        outs = []
        for c in range(chunks):
            rs = slice(c * ch, (c + 1) * ch)
            outs.append(o_intra[rs] + lax.dot_general(qd[rs], st.astype(BF16), NT_DIMS, preferred_element_type=F32))
            st = st * decays[c][:, sl] + lax.dot_general(vh[rs], k_end[c][:, sl], TN_DIMS,
                                                         preferred_element_type=F32)
        st_scr[hd] = st
        o = jnp.concatenate(outs, axis=0)
        ms = jnp.mean(o * o, axis=-1, keepdims=True)
        o_ref[0, :, sl] = (o * lax.rsqrt(ms + NORM_EPS) * gate_gain[:, sl]).astype(o_ref.dtype)


def _hgrn(proj3, fb3, lb_logits, gain, cast_weights, *, heads, layer, chunks=4):
    bsz, seq, _ = proj3.shape
    width = heads * HEAD_DIM
    rows = chunks * HGRN_CHUNK
    blk = (1, rows, width)
    nc = seq // rows
    cast_specs = _cast_specs(cast_weights, bsz * nc, lambda b, c: b * nc + c)
    outs = pl.pallas_call(
        functools.partial(_hgrn_kernel, heads=heads, layer=layer, chunks=chunks, n_cast=len(cast_weights)),
        grid=(bsz, nc),
        in_specs=[
            pl.BlockSpec(lb_logits.shape, lambda b, c: (0, 0)),
            pl.BlockSpec((1, width), lambda b, c: (0, 0)),
            pl.BlockSpec(blk, lambda b, c: (b, c, COL_QB)),
            pl.BlockSpec(blk, lambda b, c: (b, c, 0)),
            pl.BlockSpec(blk, lambda b, c: (b, c, COL_IB)),
            pl.BlockSpec(blk, lambda b, c: (b, c, COL_GB)),
            *cast_specs,
        ],
        out_specs=[pl.BlockSpec(blk, lambda b, c: (b, c, 0)), *cast_specs],
        out_shape=[jax.ShapeDtypeStruct((bsz, seq, width), BF16),
                   *(jax.ShapeDtypeStruct(w.shape, BF16) for w in cast_weights)],
        scratch_shapes=[pltpu.VMEM((heads, HEAD_DIM, HEAD_DIM), F32)],
        compiler_params=_params("arbitrary", "arbitrary"),
        name="hgrn2",
    )(lb_logits, gain, proj3, fb3, proj3, proj3, *cast_weights)
    return outs[0], outs[1:]


def _mix_kernel(ya_ref, yb_ref, ga0_ref, ga1_ref, gb0_ref, gb1_ref, x_ref, wa_ref, wb_ref, wo_ref, gain_ref,
                *rest, tc, n_cast):
    w32_refs = rest[:n_cast]
    x1_ref, h2_ref = rest[n_cast:n_cast + 2]
    w16_refs = rest[n_cast + 2:2 * n_cast + 2]
    mixed_scr, = rest[2 * n_cast + 2:]
    _cast_slabs(w32_refs, w16_refs)
    d = x_ref.shape[1]
    gate_refs = ((ga0_ref, gb0_ref), (ga1_ref, gb1_ref))
    ya = ya_ref[...]
    yb = yb_ref[...]
    for cb in range(d // tc):
        cs = slice(cb * tc, (cb + 1) * tc)
        ga_ref, gb_ref = gate_refs[(cb * tc) // SEG]
        gs = slice((cb * tc) % SEG, (cb * tc) % SEG + tc)
        pa = jnp.dot(ya, wa_ref[:, cs], preferred_element_type=F32)
        pb = jnp.dot(yb, wb_ref[:, cs], preferred_element_type=F32)
        mixed = (jax.nn.sigmoid(ga_ref[:, gs].astype(F32)) * pa
                 + jax.nn.sigmoid(gb_ref[:, gs].astype(F32)) * pb)
        mixed_scr[:, cs] = mixed.astype(BF16)
    ssq = jnp.zeros((x_ref.shape[0], 1), F32)
    for cb in range(d // tc):
        cs = slice(cb * tc, (cb + 1) * tc)
        x1 = x_ref[:, cs] + jnp.dot(mixed_scr[...], wo_ref[:, cs], preferred_element_type=F32)
        x1_ref[:, cs] = x1
        ssq = ssq + jnp.sum(x1 * x1, axis=-1, keepdims=True)
    inv = lax.rsqrt(ssq * (1.0 / d) + NORM_EPS)
    for cb in range(d // tc):
        cs = slice(cb * tc, (cb + 1) * tc)
        h2_ref[:, cs] = (x1_ref[:, cs] * inv * gain_ref[:, cs]).astype(BF16)


def _mix(ya, yb, proj, x2d, wa, wb, wo, gain, cast_weights, *, tm=512, tc=512):
    n, d = x2d.shape
    row = lambda i: (i, 0)
    seg = lambda c: pl.BlockSpec((tm, SEG), lambda i: (i, c))
    cast_specs = _cast_specs(cast_weights, n // tm, lambda i: i)
    outs = pl.pallas_call(
        functools.partial(_mix_kernel, tc=tc, n_cast=len(cast_weights)),
        grid=(n // tm,),
        in_specs=[
            pl.BlockSpec((tm, A_WIDTH), row),
            pl.BlockSpec((tm, B_WIDTH), row),
            seg(COL_GATE_A), seg(COL_GATE_A + 1), seg(COL_GATE_B), seg(COL_GATE_B + 1),
            pl.BlockSpec((tm, d), row),
            _resident(wa.shape), _resident(wb.shape), _resident(wo.shape), _resident(gain.shape),
            *cast_specs,
        ],
        out_specs=[pl.BlockSpec((tm, d), row), pl.BlockSpec((tm, d), row), *cast_specs],
        out_shape=[jax.ShapeDtypeStruct((n, d), F32), jax.ShapeDtypeStruct((n, d), BF16),
                   *(jax.ShapeDtypeStruct(w.shape, BF16) for w in cast_weights)],
        scratch_shapes=[pltpu.VMEM((tm, d), BF16)],
        compiler_params=_params("arbitrary"),
        name="mix",
    )(ya, yb, proj, proj, proj, proj, x2d, wa, wb, wo, gain, *cast_weights)
    return outs[0], outs[1], outs[2:]


def _ffn_kernel(h_ref, wg_ref, wu_ref, wd_ref, o_ref):
    @pl.when(pl.program_id(1) == 0)
    def _():
        o_ref[...] = jnp.zeros_like(o_ref)

    h = h_ref[...]
    g = jnp.dot(h, wg_ref[...], preferred_element_type=F32)
    u = jnp.dot(h, wu_ref[...], preferred_element_type=F32)
    act = (g * jax.nn.sigmoid(g) * u).astype(BF16)
    o_ref[...] += jnp.dot(act, wd_ref[...], preferred_element_type=F32)


def _ffn(h2, w_gu, w_down, *, tm=1024, tf=512):
    n, d = h2.shape
    dff = w_down.shape[0]
    nj = dff // tf
    return pl.pallas_call(
        _ffn_kernel,
        grid=(n // tm, nj),
        in_specs=[
            pl.BlockSpec((tm, d), lambda i, j: (i, 0)),
            pl.BlockSpec((d, tf), lambda i, j: (0, j)),
            pl.BlockSpec((d, tf), lambda i, j: (0, j + nj)),
            pl.BlockSpec((tf, d), lambda i, j: (j, 0)),
        ],
        out_specs=pl.BlockSpec((tm, d), lambda i, j: (i, 0)),
        out_shape=jax.ShapeDtypeStruct((n, d), F32),
        compiler_params=_params("arbitrary", "arbitrary"),
        name="ffn",
    )(h2, w_gu, w_gu, w_down)


def _ple_kernel(x1_ref, y_ref, p_ref, wg_ref, wp_ref, gain_ref, o_ref, x2_scr, *, tc, final_norm):
    d = x1_ref.shape[1]
    x2_scr[...] = x1_ref[...] + y_ref[...]
    xb = x2_scr[...].astype(BF16)
    pb = p_ref[...].astype(BF16)
    ssq = jnp.zeros((x1_ref.shape[0], 1), F32)
    for cb in range(d // tc):
        cs = slice(cb * tc, (cb + 1) * tc)
        gate = jax.nn.sigmoid(jnp.dot(xb, wg_ref[:, cs], preferred_element_type=F32))
        pe = jnp.dot(pb, wp_ref[:, cs], preferred_element_type=F32)
        x3 = x2_scr[:, cs] + gate * pe
        o_ref[:, cs] = x3
        ssq = ssq + jnp.sum(x3 * x3, axis=-1, keepdims=True)
    if not final_norm:
        return
    inv = lax.rsqrt(ssq * (1.0 / d) + NORM_EPS)
    for cb in range(d // tc):
        cs = slice(cb * tc, (cb + 1) * tc)
        o_ref[:, cs] = o_ref[:, cs] * inv * gain_ref[:, cs]


def _ple(x1, y, p2d, wg, wp, gain, *, final_norm, tm=512, tc=512):
    n, d = x1.shape
    row = lambda i: (i, 0)
    return pl.pallas_call(
        functools.partial(_ple_kernel, tc=tc, final_norm=final_norm),
        grid=(n // tm,),
        in_specs=[
            pl.BlockSpec((tm, d), row),
            pl.BlockSpec((tm, d), row),
            pl.BlockSpec((tm, p2d.shape[1]), row),
            _resident(wg.shape), _resident(wp.shape), _resident(gain.shape),
        ],
        out_specs=pl.BlockSpec((tm, d), row),
        out_shape=jax.ShapeDtypeStruct((n, d), F32),
        scratch_shapes=[pltpu.VMEM((tm, d), F32)],
        compiler_params=_params("arbitrary"),
        name="ple",
    )(x1, y, p2d, wg, wp, gain)


def kernel(x, p, norm_mix, w_in, hgrn_norm, w_proj_a, w_proj_b, w_out, norm_ffn, w_gate_up, w_down, w_ple,
           w_ple_gate, rel_bias, hgrn_lb_logits, norm_final):
    bsz, seq, d = x.shape
    n = bsz * seq
    depth = w_in.shape[0]
    a_heads = A_WIDTH // HEAD_DIM
    b_heads = B_WIDTH // HEAD_DIM

    cscale = np.ones((1, IN_WIDTH), np.float32)
    cscale[:, COL_QA * SEG:(COL_QA + 1) * SEG] = LOG2E / math.sqrt(HEAD_DIM)
    cscale = jnp.asarray(cscale)

    xc = x.reshape(n, d)
    for i in range(depth):
        proj, fb = _in_proj(xc, norm_mix[i][None], w_in[i], cscale)
        proj3 = proj.reshape(bsz, seq, IN_WIDTH)
        ya, (wa, wb, wo) = _moba(proj3, rel_bias, (w_proj_a[i], w_proj_b[i], w_out[i]), heads=a_heads)
        yb, (wgu, wdn) = _hgrn(proj3, fb.reshape(bsz, seq, SEG), hgrn_lb_logits, hgrn_norm[i][None],
                               (w_gate_up[i], w_down[i]), heads=b_heads, layer=i)
        x1, h2, (wpg, wpl) = _mix(ya.reshape(n, A_WIDTH), yb.reshape(n, B_WIDTH), proj, xc, wa, wb, wo,
                                  norm_ffn[i][None], (w_ple_gate[i], w_ple[i]))
        y = _ffn(h2, wgu, wdn)
        xc = _ple(x1, y, p[i].reshape(n, -1), wpg, wpl, norm_final[None], final_norm=(i == depth - 1))
    return xc.reshape(bsz, seq, d)
```

```python
import functools
import math

import jax
import jax.numpy as jnp
import numpy as np
from jax import lax
from jax.experimental import pallas as pl
from jax.experimental.pallas import tpu as pltpu

F32 = jnp.float32
BF16 = jnp.bfloat16

D_MODEL = 2048
HEAD_DIM = 128
A_WIDTH = 1024
B_WIDTH = 1024
MOBA_BLOCK = 256
MOBA_TOPK = 3
REL_BUCKETS = 32
REL_MAX_EXACT = 16
REL_MAX_DIST = 128
HGRN_CHUNK = 64
D_FF = 5632
NORM_EPS = 1e-6

COL_QA, COL_KA, COL_VA, COL_QB, COL_FB, COL_IB, COL_GB, COL_GATE_A, COL_GATE_B = 0, 1, 2, 3, 4, 5, 6, 7, 9
IN_WIDTH = 11 * 1024
SEG = 1024

BF16_SUBLANES = 16
MASK_VALUE = -1e30
VMEM_LIMIT = 56 * 1024 * 1024

NT_DIMS = (((1,), (1,)), ((), ()))
LOG2E = math.log2(math.e)


def _bucket_thresholds():
    n = np.arange(0, 4096, dtype=np.int64)
    nf = np.maximum(n, REL_MAX_EXACT).astype(np.float64)
    large = REL_MAX_EXACT + (np.log(nf / REL_MAX_EXACT) / math.log(REL_MAX_DIST / REL_MAX_EXACT)
                             * (REL_BUCKETS - REL_MAX_EXACT)).astype(np.int64)
    large = np.minimum(large, REL_BUCKETS - 1)
    bucket = np.where(n < REL_MAX_EXACT, n, large)
    return [int(np.argmax(bucket >= k)) for k in range(REL_BUCKETS)]


BUCKET_LO = _bucket_thresholds()


def _params(*sem):
    return pltpu.CompilerParams(dimension_semantics=sem, vmem_limit_bytes=VMEM_LIMIT)


def _resident(shape):
    return pl.BlockSpec(shape, lambda *_: (0,) * len(shape), pipeline_mode=pl.Buffered(1))


def _cast_specs(weights, n_steps, step_index):
    specs = []
    for w in weights:
        r = BF16_SUBLANES
        while w.shape[0] % r or w.shape[0] // r > n_steps:
            r += BF16_SUBLANES
        last = w.shape[0] // r - 1
        specs.append(pl.BlockSpec((r, w.shape[1]), lambda *ids, last=last: (jnp.minimum(step_index(*ids), last), 0)))
    return specs


def _cast_slabs(w32_refs, w16_refs):
    for w32, w16 in zip(w32_refs, w16_refs):
        w16[...] = w32[...].astype(BF16)


def _in_proj_kernel(x_ref, gain_ref, w_ref, cscale_ref, *rest, rows_per_step, first_tile):
    if first_tile:
        o_ref, fb_ref, w16_ref, h_scr = rest
    else:
        _, _, o_ref, fb_ref, h_scr = rest
    j = pl.program_id(1)

    @pl.when(j == 0)
    def _():
        def body(r, _):
            rs = pl.ds(pl.multiple_of(r * rows_per_step, rows_per_step), rows_per_step)
            x = x_ref[rs, :]
            ms = jnp.mean(x * x, axis=-1, keepdims=True)
            h_scr[rs, :] = (x * lax.rsqrt(ms + NORM_EPS) * gain_ref[...]).astype(BF16)
            return 0
        lax.fori_loop(0, x_ref.shape[0] // rows_per_step, body, 0)

    if first_tile:
        w = w_ref[...].astype(BF16)
        w16_ref[...] = w
    else:
        w = w_ref[...]
    acc = jnp.dot(h_scr[...], w, preferred_element_type=F32)
    o_ref[...] = (acc * cscale_ref[...]).astype(BF16)
    pltpu.store(fb_ref, acc, mask=jnp.broadcast_to(j == COL_FB, acc.shape))


def _in_proj(x2d, gain, w_f32, cscale, *, tm=1024, tn=SEG):
    n, d = x2d.shape
    width = w_f32.shape[1]
    nj = width // tn
    out_shape = [jax.ShapeDtypeStruct((n, width), BF16), jax.ShapeDtypeStruct((n, SEG), F32)]
    scratch = [pltpu.VMEM((tm, d), BF16)]

    def specs(row0, x_mode):
        ins = [
            pl.BlockSpec((tm, d), lambda i, j: (i + row0, 0), **x_mode),
            pl.BlockSpec((1, d), lambda i, j: (0, 0)),
            pl.BlockSpec((d, tn), lambda i, j: (0, j)),
            pl.BlockSpec((1, tn), lambda i, j: (0, j)),
        ]
        outs = [
            pl.BlockSpec((tm, tn), lambda i, j: (i + row0, j)),
            pl.BlockSpec((tm, SEG), lambda i, j: (i + row0, 0)),
        ]
        return ins, outs

    ins, outs = specs(0, dict(pipeline_mode=pl.Buffered(1)))
    proj, fb, w16 = pl.pallas_call(
        functools.partial(_in_proj_kernel, rows_per_step=128, first_tile=True),
        grid=(1, nj),
        in_specs=ins,
        out_specs=outs + [pl.BlockSpec((d, tn), lambda i, j: (0, j))],
        out_shape=out_shape + [jax.ShapeDtypeStruct((d, width), BF16)],
        scratch_shapes=scratch,
        compiler_params=_params("arbitrary", "arbitrary"),
        name="in_proj_first",
    )(x2d, gain, w_f32, cscale)

    ins, outs = specs(1, {})
    any_spec = pl.BlockSpec(memory_space=pl.ANY)
    return pl.pallas_call(
        functools.partial(_in_proj_kernel, rows_per_step=128, first_tile=False),
        grid=(n // tm - 1, nj),
        in_specs=ins + [any_spec, any_spec],
        out_specs=outs,
        out_shape=out_shape,
        input_output_aliases={4: 0, 5: 1},
        scratch_shapes=scratch,
        compiler_params=_params("arbitrary", "arbitrary"),
        name="in_proj",
    )(x2d, gain, w16, cscale, proj, fb)


def _moba_kernel(rb_ref, qall_ref, q_ref, k_ref, v_ref, *rest, n_blocks, group, n_cast):
    w32_refs = rest[:n_cast]
    o_ref = rest[n_cast]
    w16_refs = rest[n_cast + 1:2 * n_cast + 1]
    bias_scr, vt_scr, sel_scr, s_scr = rest[2 * n_cast + 1:]
    hg = pl.program_id(0)
    b = pl.program_id(1)
    qi = pl.program_id(2)
    blk = MOBA_BLOCK
    dh = HEAD_DIM
    sub = 8

    @pl.when((b == 0) & (qi == 0))
    def _build_bias():
        r = lax.broadcasted_iota(jnp.int32, (blk, blk), 1)
        c = lax.broadcasted_iota(jnp.int32, (blk, blk), 0)
        for g in range(group):
            hd = hg * group + g
            for d in range(2):
                rel = d * blk + r - c
                bias = jnp.full((blk, blk), rb_ref[REL_BUCKETS - 1, hd] * LOG2E, F32)
                for kk in range(REL_BUCKETS - 2, -1, -1):
                    bias = jnp.where(rel < BUCKET_LO[kk + 1], rb_ref[kk, hd] * LOG2E, bias)
                if d == 0:
                    bias = jnp.where(rel < 0, MASK_VALUE, bias)
                bias_scr[g, d] = bias

    @pl.when(qi == 0)
    def _per_sequence():
        for g in range(group):
            hs = slice(g * dh, (g + 1) * dh)
            means = []
            for jb in range(n_blocks):
                rs = slice(jb * blk, (jb + 1) * blk)
                means.append(jnp.mean(k_ref[0, rs, hs].astype(F32), axis=0, keepdims=True))
                vt_scr[g, :, rs] = v_ref[0, rs, hs].T
            km = jnp.concatenate(means + [jnp.zeros((16 - n_blocks, dh), F32)], axis=0)

            km_hi = km.astype(BF16)
            km_lo = (km - km_hi.astype(F32)).astype(BF16)
            qall = qall_ref[0, :, hs]
            sc = (lax.dot_general(km_hi, qall, NT_DIMS, preferred_element_type=F32)
                  + lax.dot_general(km_lo, qall, NT_DIMS, preferred_element_type=F32))
            rows = lax.broadcasted_iota(jnp.int32, sc.shape, 0)
            rows_f = rows.astype(F32)
            past = rows < lax.broadcasted_iota(jnp.int32, sc.shape, 1) // blk
            s = jnp.where(past, sc, -jnp.inf)
            picked = jnp.zeros(sc.shape, jnp.bool_)
            for _ in range(MOBA_TOPK):
                top = jnp.max(s, axis=0, keepdims=True)
                first = jnp.min(jnp.where(s == top, rows_f, 1e9), axis=0, keepdims=True)
                hit = rows_f == first
                picked = picked | hit
                s = jnp.where(hit, -jnp.inf, s)
            mask = jnp.where(picked & past, 0.0, MASK_VALUE)
            for qb in range(n_blocks):
                sel_scr[g, qb] = mask[:, qb * blk:(qb + 1) * blk]

    def attend(nt):
        _cast_slabs(w32_refs, w16_refs)
        hss = [slice(g * dh, (g + 1) * dh) for g in range(group)]
        ms = []
        for g in range(group):
            far = rb_ref[REL_BUCKETS - 1, hg * group + g] * LOG2E
            s = lax.dot_general(k_ref[0, :nt * blk, hss[g]], q_ref[0, :, hss[g]], NT_DIMS,
                                preferred_element_type=F32)
            m8 = None
            for j in range(nt):
                rs = slice(j * blk, (j + 1) * blk)
                d = nt - 1 - j
                if d == 0:
                    sj = s[rs] + bias_scr[g, 0]
                elif d == 1:
                    sj = s[rs] + bias_scr[g, 1] + sel_scr[g, qi, j:j + 1, :]
                else:
                    sj = s[rs] + (sel_scr[g, qi, j:j + 1, :] + far)
                s_scr[g, rs, :] = sj
                mj = jnp.max(sj.reshape(blk // sub, sub, blk), axis=0)
                m8 = mj if m8 is None else jnp.maximum(m8, mj)
            ms.append(jnp.max(m8, axis=0, keepdims=True))
        for g in range(group):
            l8 = jnp.zeros((sub, blk), F32)
            acc = None
            for j in range(nt):
                rs = slice(j * blk, (j + 1) * blk)
                p = jnp.exp2(s_scr[g, rs, :] - ms[g])
                l8 = l8 + jnp.sum(p.reshape(blk // sub, sub, blk), axis=0)
                pv = jnp.dot(vt_scr[g, :, rs], p.astype(BF16), preferred_element_type=F32)
                acc = pv if acc is None else acc + pv
            l = jnp.sum(l8, axis=0, keepdims=True)
            o_ref[0, :, hss[g]] = (acc / l).T.astype(o_ref.dtype)

    for nt in range(1, n_blocks + 1):
        pl.when(qi == nt - 1)(functools.partial(attend, nt))


def _moba(proj3, rel_bias, cast_weights, *, heads, group=4):
    bsz, seq, _ = proj3.shape
    nb = seq // MOBA_BLOCK
    gw = group * HEAD_DIM
    cq, ck, cv = (c * (SEG // gw) for c in (COL_QA, COL_KA, COL_VA))
    cast_specs = _cast_specs(cast_weights, (heads // group) * bsz * nb, lambda h, b, i: (h * bsz + b) * nb + i)
    outs = pl.pallas_call(
        functools.partial(_moba_kernel, n_blocks=nb, group=group, n_cast=len(cast_weights)),
        grid=(heads // group, bsz, nb),
        in_specs=[
            pl.BlockSpec(memory_space=pltpu.SMEM),
            pl.BlockSpec((1, seq, gw), lambda h, b, i: (b, 0, cq + h)),
            pl.BlockSpec((1, MOBA_BLOCK, gw), lambda h, b, i: (b, i, cq + h)),
            pl.BlockSpec((1, seq, gw), lambda h, b, i: (b, 0, ck + h)),
            pl.BlockSpec((1, seq, gw), lambda h, b, i: (b, 0, cv + h)),
            *cast_specs,
        ],
        out_specs=[pl.BlockSpec((1, MOBA_BLOCK, gw), lambda h, b, i: (b, i, h)), *cast_specs],
        out_shape=[jax.ShapeDtypeStruct((bsz, seq, heads * HEAD_DIM), BF16),
                   *(jax.ShapeDtypeStruct(w.shape, BF16) for w in cast_weights)],
        scratch_shapes=[
            pltpu.VMEM((group, 2, MOBA_BLOCK, MOBA_BLOCK), F32),
            pltpu.VMEM((group, HEAD_DIM, seq), BF16),
            pltpu.VMEM((group, nb, 16, MOBA_BLOCK), F32),
            pltpu.VMEM((group, seq, MOBA_BLOCK), F32),
        ],
        compiler_params=_params("arbitrary", "arbitrary", "arbitrary"),
        name="moba",
    )(rel_bias, proj3, proj3, proj3, proj3, *cast_weights)
    return outs[0], outs[1:]


TN_DIMS = (((0,), (0,)), ((), ()))


def _cumsum_rows(x, period):
    pos = lax.broadcasted_iota(jnp.int32, x.shape, 0) % period
    sh = 1
    while sh < period:
        x = x + jnp.where(pos >= sh, pltpu.roll(x, sh, axis=0), 0.0)
        sh *= 2
    return x


def _hgrn_kernel(lbl_ref, gain_ref, q_ref, f_ref, i_ref, g_ref, *rest, heads, layer, chunks, n_cast):
    w32_refs = rest[:n_cast]
    o_ref = rest[n_cast]
    w16_refs = rest[n_cast + 1:2 * n_cast + 1]
    st_scr, = rest[2 * n_cast + 1:]
    _cast_slabs(w32_refs, w16_refs)
    ch = HGRN_CHUNK
    dh = HEAD_DIM
    rows = chunks * ch

    @pl.when(pl.program_id(1) == 0)
    def _():
        st_scr[...] = jnp.zeros_like(st_scr)

    lg = lbl_ref[...]
    e = jnp.exp(lg - jnp.max(lg, axis=0, keepdims=True))
    lb = jnp.sum(e[:layer + 1], axis=0, keepdims=True) / jnp.sum(e, axis=0, keepdims=True)

    half = 0.5 * (1.0 - lb)
    f = (lb + half) + half * jnp.tanh(0.5 * f_ref[0])
    bcum = _cumsum_rows(jnp.log(f), ch)
    q_dec = (q_ref[0].astype(F32) * jnp.exp(bcum)).astype(BF16)
    k_dec32 = (1.0 - f) * jnp.exp(-bcum)
    k_dec = k_dec32.astype(BF16)
    decays = [jnp.exp(bcum[(c + 1) * ch - 1:(c + 1) * ch, :]) for c in range(chunks)]
    k_end = [(k_dec32[c * ch:(c + 1) * ch, :] * decays[c]).astype(BF16) for c in range(chunks)]
    v = i_ref[0]
    hg = 0.5 * g_ref[0].astype(F32)
    gate_gain = (hg + hg * jnp.tanh(hg)) * gain_ref[...]

    r = lax.broadcasted_iota(jnp.int32, (rows, rows), 0)
    c_ = lax.broadcasted_iota(jnp.int32, (rows, rows), 1)
    causal_in_chunk = (r >= c_) & (r // ch == c_ // ch)

    for hd in range(heads):
        sl = slice(hd * dh, (hd + 1) * dh)
        qd, vh = q_dec[:, sl], v[:, sl]
        a = lax.dot_general(qd, k_dec[:, sl], NT_DIMS, preferred_element_type=F32)
        a = jnp.where(causal_in_chunk, a, 0.0).astype(BF16)
        o_intra = jnp.dot(a, vh, preferred_element_type=F32)
        st = st_scr[hd]
        outs = []
        for c in range(chunks):
            rs = slice(c * ch, (c + 1) * ch)
            outs.append(o_intra[rs] + lax.dot_general(qd[rs], st.astype(BF16), NT_DIMS, preferred_element_type=F32))
            st = st * decays[c][:, sl] + lax.dot_general(vh[rs], k_end[c][:, sl], TN_DIMS,
                                                         preferred_element_type=F32)
        st_scr[hd] = st
        o = jnp.concatenate(outs, axis=0)
        ms = jnp.mean(o * o, axis=-1, keepdims=True)
        o_ref[0, :, sl] = (o * lax.rsqrt(ms + NORM_EPS) * gate_gain[:, sl]).astype(o_ref.dtype)


def _hgrn(proj3, fb3, lb_logits, gain, cast_weights, *, heads, layer, chunks=4):
    bsz, seq, _ = proj3.shape
    width = heads * HEAD_DIM
    rows = chunks * HGRN_CHUNK
    blk = (1, rows, width)
    nc = seq // rows
    cast_specs = _cast_specs(cast_weights, bsz * nc, lambda b, c: b * nc + c)
    outs = pl.pallas_call(
        functools.partial(_hgrn_kernel, heads=heads, layer=layer, chunks=chunks, n_cast=len(cast_weights)),
        grid=(bsz, nc),
        in_specs=[
            pl.BlockSpec(lb_logits.shape, lambda b, c: (0, 0)),
            pl.BlockSpec((1, width), lambda b, c: (0, 0)),
            pl.BlockSpec(blk, lambda b, c: (b, c, COL_QB)),
            pl.BlockSpec(blk, lambda b, c: (b, c, 0)),
            pl.BlockSpec(blk, lambda b, c: (b, c, COL_IB)),
            pl.BlockSpec(blk, lambda b, c: (b, c, COL_GB)),
            *cast_specs,
        ],
        out_specs=[pl.BlockSpec(blk, lambda b, c: (b, c, 0)), *cast_specs],
        out_shape=[jax.ShapeDtypeStruct((bsz, seq, width), BF16),
                   *(jax.ShapeDtypeStruct(w.shape, BF16) for w in cast_weights)],
        scratch_shapes=[pltpu.VMEM((heads, HEAD_DIM, HEAD_DIM), F32)],
        compiler_params=_params("arbitrary", "arbitrary"),
        name="hgrn2",
    )(lb_logits, gain, proj3, fb3, proj3, proj3, *cast_weights)
    return outs[0], outs[1:]


def _mix_kernel(ya_ref, yb_ref, ga0_ref, ga1_ref, gb0_ref, gb1_ref, x_ref, wa_ref, wb_ref, wo_ref, gain_ref,
                *rest, tc, n_cast):
    w32_refs = rest[:n_cast]
    x1_ref, h2_ref = rest[n_cast:n_cast + 2]
    w16_refs = rest[n_cast + 2:2 * n_cast + 2]
    mixed_scr, = rest[2 * n_cast + 2:]
    _cast_slabs(w32_refs, w16_refs)
    d = x_ref.shape[1]
    gate_refs = ((ga0_ref, gb0_ref), (ga1_ref, gb1_ref))
    ya = ya_ref[...]
    yb = yb_ref[...]
    for cb in range(d // tc):
        cs = slice(cb * tc, (cb + 1) * tc)
        ga_ref, gb_ref = gate_refs[(cb * tc) // SEG]
        gs = slice((cb * tc) % SEG, (cb * tc) % SEG + tc)
        pa = jnp.dot(ya, wa_ref[:, cs], preferred_element_type=F32)
        pb = jnp.dot(yb, wb_ref[:, cs], preferred_element_type=F32)
        mixed = (jax.nn.sigmoid(ga_ref[:, gs].astype(F32)) * pa
                 + jax.nn.sigmoid(gb_ref[:, gs].astype(F32)) * pb)
        mixed_scr[:, cs] = mixed.astype(BF16)
    ssq = jnp.zeros((x_ref.shape[0], 1), F32)
    for cb in range(d // tc):
        cs = slice(cb * tc, (cb + 1) * tc)
        x1 = x_ref[:, cs] + jnp.dot(mixed_scr[...], wo_ref[:, cs], preferred_element_type=F32)
        x1_ref[:, cs] = x1
        ssq = ssq + jnp.sum(x1 * x1, axis=-1, keepdims=True)
    inv = lax.rsqrt(ssq * (1.0 / d) + NORM_EPS)
    for cb in range(d // tc):
        cs = slice(cb * tc, (cb + 1) * tc)
        h2_ref[:, cs] = (x1_ref[:, cs] * inv * gain_ref[:, cs]).astype(BF16)


def _mix(ya, yb, proj, x2d, wa, wb, wo, gain, cast_weights, *, tm=512, tc=512):
    n, d = x2d.shape
    row = lambda i: (i, 0)
    seg = lambda c: pl.BlockSpec((tm, SEG), lambda i: (i, c))
    cast_specs = _cast_specs(cast_weights, n // tm, lambda i: i)
    outs = pl.pallas_call(
        functools.partial(_mix_kernel, tc=tc, n_cast=len(cast_weights)),
        grid=(n // tm,),
        in_specs=[
            pl.BlockSpec((tm, A_WIDTH), row),
            pl.BlockSpec((tm, B_WIDTH), row),
            seg(COL_GATE_A), seg(COL_GATE_A + 1), seg(COL_GATE_B), seg(COL_GATE_B + 1),
            pl.BlockSpec((tm, d), row),
            _resident(wa.shape), _resident(wb.shape), _resident(wo.shape), _resident(gain.shape),
            *cast_specs,
        ],
        out_specs=[pl.BlockSpec((tm, d), row), pl.BlockSpec((tm, d), row), *cast_specs],
        out_shape=[jax.ShapeDtypeStruct((n, d), F32), jax.ShapeDtypeStruct((n, d), BF16),
                   *(jax.ShapeDtypeStruct(w.shape, BF16) for w in cast_weights)],
        scratch_shapes=[pltpu.VMEM((tm, d), BF16)],
        compiler_params=_params("arbitrary"),
        name="mix",
    )(ya, yb, proj, proj, proj, proj, x2d, wa, wb, wo, gain, *cast_weights)
    return outs[0], outs[1], outs[2:]


def _ffn_kernel(h_ref, wg_ref, wu_ref, wd_ref, o_ref):
    @pl.when(pl.program_id(1) == 0)
    def _():
        o_ref[...] = jnp.zeros_like(o_ref)

    h = h_ref[...]
    g = jnp.dot(h, wg_ref[...], preferred_element_type=F32)
    u = jnp.dot(h, wu_ref[...], preferred_element_type=F32)
    act = (g * jax.nn.sigmoid(g) * u).astype(BF16)
    o_ref[...] += jnp.dot(act, wd_ref[...], preferred_element_type=F32)


def _ffn(h2, w_gu, w_down, *, tm=1024, tf=512):
    n, d = h2.shape
    dff = w_down.shape[0]
    nj = dff // tf
    return pl.pallas_call(
        _ffn_kernel,
        grid=(n // tm, nj),
        in_specs=[
            pl.BlockSpec((tm, d), lambda i, j: (i, 0)),
            pl.BlockSpec((d, tf), lambda i, j: (0, j)),
            pl.BlockSpec((d, tf), lambda i, j: (0, j + nj)),
            pl.BlockSpec((tf, d), lambda i, j: (j, 0)),
        ],
        out_specs=pl.BlockSpec((tm, d), lambda i, j: (i, 0)),
        out_shape=jax.ShapeDtypeStruct((n, d), F32),
        compiler_params=_params("arbitrary", "arbitrary"),
        name="ffn",
    )(h2, w_gu, w_gu, w_down)


def _ple_kernel(x1_ref, y_ref, p_ref, wg_ref, wp_ref, gain_ref, o_ref, x2_scr, *, tc, final_norm):
    d = x1_ref.shape[1]
    x2_scr[...] = x1_ref[...] + y_ref[...]
    xb = x2_scr[...].astype(BF16)
    pb = p_ref[...].astype(BF16)
    ssq = jnp.zeros((x1_ref.shape[0], 1), F32)
    for cb in range(d // tc):
        cs = slice(cb * tc, (cb + 1) * tc)
        gate = jax.nn.sigmoid(jnp.dot(xb, wg_ref[:, cs], preferred_element_type=F32))
        pe = jnp.dot(pb, wp_ref[:, cs], preferred_element_type=F32)
        x3 = x2_scr[:, cs] + gate * pe
        o_ref[:, cs] = x3
        ssq = ssq + jnp.sum(x3 * x3, axis=-1, keepdims=True)
    if not final_norm:
        return
    inv = lax.rsqrt(ssq * (1.0 / d) + NORM_EPS)
    for cb in range(d // tc):
        cs = slice(cb * tc, (cb + 1) * tc)
        o_ref[:, cs] = o_ref[:, cs] * inv * gain_ref[:, cs]


def _ple(x1, y, p2d, wg, wp, gain, *, final_norm, tm=512, tc=512):
    n, d = x1.shape
    row = lambda i: (i, 0)
    return pl.pallas_call(
        functools.partial(_ple_kernel, tc=tc, final_norm=final_norm),
        grid=(n // tm,),
        in_specs=[
            pl.BlockSpec((tm, d), row),
            pl.BlockSpec((tm, d), row),
            pl.BlockSpec((tm, p2d.shape[1]), row),
            _resident(wg.shape), _resident(wp.shape), _resident(gain.shape),
        ],
        out_specs=pl.BlockSpec((tm, d), row),
        out_shape=jax.ShapeDtypeStruct((n, d), F32),
        scratch_shapes=[pltpu.VMEM((tm, d), F32)],
        compiler_params=_params("arbitrary"),
        name="ple",
    )(x1, y, p2d, wg, wp, gain)


def kernel(x, p, norm_mix, w_in, hgrn_norm, w_proj_a, w_proj_b, w_out, norm_ffn, w_gate_up, w_down, w_ple,
           w_ple_gate, rel_bias, hgrn_lb_logits, norm_final):
    bsz, seq, d = x.shape
    n = bsz * seq
    depth = w_in.shape[0]
    a_heads = A_WIDTH // HEAD_DIM
    b_heads = B_WIDTH // HEAD_DIM

    cscale = np.ones((1, IN_WIDTH), np.float32)
    cscale[:, COL_QA * SEG:(COL_QA + 1) * SEG] = LOG2E / math.sqrt(HEAD_DIM)
    cscale = jnp.asarray(cscale)

    xc = x.reshape(n, d)
    for i in range(depth):
        proj, fb = _in_proj(xc, norm_mix[i][None], w_in[i], cscale)
        proj3 = proj.reshape(bsz, seq, IN_WIDTH)
        ya, _ = _moba(proj3, rel_bias, (), heads=a_heads)
        yb, (wa, wb, wo, wgu, wdn) = _hgrn(proj3, fb.reshape(bsz, seq, SEG), hgrn_lb_logits, hgrn_norm[i][None],
                                           (w_proj_a[i], w_proj_b[i], w_out[i], w_gate_up[i], w_down[i]),
                                           heads=b_heads, layer=i)
        x1, h2, (wpg, wpl) = _mix(ya.reshape(n, A_WIDTH), yb.reshape(n, B_WIDTH), proj, xc, wa, wb, wo,
                                  norm_ffn[i][None], (w_ple_gate[i], w_ple[i]))
        y = _ffn(h2, wgu, wdn)
        xc = _ple(x1, y, p[i].reshape(n, -1), wpg, wpl, norm_final[None], final_norm=(i == depth - 1))
    return xc.reshape(bsz, seq, d)
```

```python
import functools
import math

import jax
import jax.numpy as jnp
import numpy as np
from jax import lax
from jax.experimental import pallas as pl
from jax.experimental.pallas import tpu as pltpu

F32 = jnp.float32
BF16 = jnp.bfloat16

D_MODEL = 2048
HEAD_DIM = 128
A_WIDTH = 1024
B_WIDTH = 1024
MOBA_BLOCK = 256
MOBA_TOPK = 3
REL_BUCKETS = 32
REL_MAX_EXACT = 16
REL_MAX_DIST = 128
HGRN_CHUNK = 64
D_FF = 5632
NORM_EPS = 1e-6

COL_QA, COL_KA, COL_VA, COL_QB, COL_FB, COL_IB, COL_GB, COL_GATE_A, COL_GATE_B = 0, 1, 2, 3, 4, 5, 6, 7, 9
IN_WIDTH = 11 * 1024
SEG = 1024

BF16_SUBLANES = 16
MASK_VALUE = -1e30
VMEM_LIMIT = 56 * 1024 * 1024

NT_DIMS = (((1,), (1,)), ((), ()))
LOG2E = math.log2(math.e)


def _bucket_thresholds():
    n = np.arange(0, 4096, dtype=np.int64)
    nf = np.maximum(n, REL_MAX_EXACT).astype(np.float64)
    large = REL_MAX_EXACT + (np.log(nf / REL_MAX_EXACT) / math.log(REL_MAX_DIST / REL_MAX_EXACT)
                             * (REL_BUCKETS - REL_MAX_EXACT)).astype(np.int64)
    large = np.minimum(large, REL_BUCKETS - 1)
    bucket = np.where(n < REL_MAX_EXACT, n, large)
    return [int(np.argmax(bucket >= k)) for k in range(REL_BUCKETS)]


BUCKET_LO = _bucket_thresholds()


def _params(*sem):
    return pltpu.CompilerParams(dimension_semantics=sem, vmem_limit_bytes=VMEM_LIMIT)


def _resident(shape):
    return pl.BlockSpec(shape, lambda *_: (0,) * len(shape), pipeline_mode=pl.Buffered(1))


def _cast_specs(weights, n_steps, step_index):
    specs = []
    for w in weights:
        r = BF16_SUBLANES
        while w.shape[0] % r or w.shape[0] // r > n_steps:
            r += BF16_SUBLANES
        last = w.shape[0] // r - 1
        specs.append(pl.BlockSpec((r, w.shape[1]), lambda *ids, last=last: (jnp.minimum(step_index(*ids), last), 0)))
    return specs


def _cast_slabs(w32_refs, w16_refs):
    for w32, w16 in zip(w32_refs, w16_refs):
        w16[...] = w32[...].astype(BF16)


def _in_proj_kernel(x_ref, gain_ref, w_ref, cscale_ref, *rest, rows_per_step, first_tile):
    if first_tile:
        o_ref, fb_ref, w16_ref, h_scr = rest
    else:
        _, _, o_ref, fb_ref, h_scr = rest
    j = pl.program_id(1)

    @pl.when(j == 0)
    def _():
        def body(r, _):
            rs = pl.ds(pl.multiple_of(r * rows_per_step, rows_per_step), rows_per_step)
            x = x_ref[rs, :]
            ms = jnp.mean(x * x, axis=-1, keepdims=True)
            h_scr[rs, :] = (x * lax.rsqrt(ms + NORM_EPS) * gain_ref[...]).astype(BF16)
            return 0
        lax.fori_loop(0, x_ref.shape[0] // rows_per_step, body, 0)

    if first_tile:
        w = w_ref[...].astype(BF16)
        w16_ref[...] = w
    else:
        w = w_ref[...]
    acc = jnp.dot(h_scr[...], w, preferred_element_type=F32)
    o_ref[...] = (acc * cscale_ref[...]).astype(BF16)
    pltpu.store(fb_ref, acc, mask=jnp.broadcast_to(j == COL_FB, acc.shape))


def _in_proj(x2d, gain, w_f32, cscale, *, tm=1024, tn=SEG):
    n, d = x2d.shape
    width = w_f32.shape[1]
    nj = width // tn
    out_shape = [jax.ShapeDtypeStruct((n, width), BF16), jax.ShapeDtypeStruct((n, SEG), F32)]
    scratch = [pltpu.VMEM((tm, d), BF16)]

    def specs(row0, x_mode):
        ins = [
            pl.BlockSpec((tm, d), lambda i, j: (i + row0, 0), **x_mode),
            pl.BlockSpec((1, d), lambda i, j: (0, 0)),
            pl.BlockSpec((d, tn), lambda i, j: (0, j)),
            pl.BlockSpec((1, tn), lambda i, j: (0, j)),
        ]
        outs = [
            pl.BlockSpec((tm, tn), lambda i, j: (i + row0, j)),
            pl.BlockSpec((tm, SEG), lambda i, j: (i + row0, 0)),
        ]
        return ins, outs

    ins, outs = specs(0, dict(pipeline_mode=pl.Buffered(1)))
    proj, fb, w16 = pl.pallas_call(
        functools.partial(_in_proj_kernel, rows_per_step=128, first_tile=True),
        grid=(1, nj),
        in_specs=ins,
        out_specs=outs + [pl.BlockSpec((d, tn), lambda i, j: (0, j))],
        out_shape=out_shape + [jax.ShapeDtypeStruct((d, width), BF16)],
        scratch_shapes=scratch,
        compiler_params=_params("arbitrary", "arbitrary"),
        name="in_proj_first",
    )(x2d, gain, w_f32, cscale)

    ins, outs = specs(1, {})
    any_spec = pl.BlockSpec(memory_space=pl.ANY)
    return pl.pallas_call(
        functools.partial(_in_proj_kernel, rows_per_step=128, first_tile=False),
        grid=(n // tm - 1, nj),
        in_specs=ins + [any_spec, any_spec],
        out_specs=outs,
        out_shape=out_shape,
        input_output_aliases={4: 0, 5: 1},
        scratch_shapes=scratch,
        compiler_params=_params("arbitrary", "arbitrary"),
        name="in_proj",
    )(x2d, gain, w16, cscale, proj, fb)


def _moba_kernel(rb_ref, qall_ref, q_ref, k_ref, v_ref, o_ref, bias_scr, vt_scr, sel_scr, s_scr, *, n_blocks, group):
    hg = pl.program_id(0)
    b = pl.program_id(1)
    t = pl.program_id(2)
    blk = MOBA_BLOCK
    dh = HEAD_DIM
    sub = 8
    half_blocks = n_blocks // 2

    @pl.when((b == 0) & (t == 0))
    def _build_bias():
        r = lax.broadcasted_iota(jnp.int32, (blk, blk), 1)
        c = lax.broadcasted_iota(jnp.int32, (blk, blk), 0)
        for g in range(group):
            hd = hg * group + g
            for d in range(2):
                rel = d * blk + r - c
                bias = jnp.full((blk, blk), rb_ref[REL_BUCKETS - 1, hd] * LOG2E, F32)
                for kk in range(REL_BUCKETS - 2, -1, -1):
                    bias = jnp.where(rel < BUCKET_LO[kk + 1], rb_ref[kk, hd] * LOG2E, bias)
                if d == 0:
                    bias = jnp.where(rel < 0, MASK_VALUE, bias)
                bias_scr[g, d] = bias

    @pl.when(t == 0)
    def _per_sequence():
        for g in range(group):
            hs = slice(g * dh, (g + 1) * dh)
            means = []
            for jb in range(n_blocks):
                rs = slice(jb * blk, (jb + 1) * blk)
                means.append(jnp.mean(k_ref[0, rs, hs].astype(F32), axis=0, keepdims=True))
                vt_scr[g, :, rs] = v_ref[0, rs, hs].T
            km = jnp.concatenate(means + [jnp.zeros((16 - n_blocks, dh), F32)], axis=0)

            km_hi = km.astype(BF16)
            km_lo = (km - km_hi.astype(F32)).astype(BF16)
            qall = qall_ref[0, :, hs]
            sc = (lax.dot_general(km_hi, qall, NT_DIMS, preferred_element_type=F32)
                  + lax.dot_general(km_lo, qall, NT_DIMS, preferred_element_type=F32))
            rows = lax.broadcasted_iota(jnp.int32, sc.shape, 0)
            rows_f = rows.astype(F32)
            past = rows < lax.broadcasted_iota(jnp.int32, sc.shape, 1) // blk
            s = jnp.where(past, sc, -jnp.inf)
            picked = jnp.zeros(sc.shape, jnp.bool_)
            for _ in range(MOBA_TOPK):
                top = jnp.max(s, axis=0, keepdims=True)
                first = jnp.min(jnp.where(s == top, rows_f, 1e9), axis=0, keepdims=True)
                hit = rows_f == first
                picked = picked | hit
                s = jnp.where(hit, -jnp.inf, s)
            mask = jnp.where(picked & past, 0.0, MASK_VALUE)
            for qb in range(n_blocks):
                sel_scr[g, qb] = mask[:, qb * blk:(qb + 1) * blk]

    def attend(half, qb):
        nt = qb + 1
        hss = [slice(g * dh, (g + 1) * dh) for g in range(group)]
        ms = []
        for g in range(group):
            far = rb_ref[REL_BUCKETS - 1, hg * group + g] * LOG2E
            s = lax.dot_general(k_ref[0, :nt * blk, hss[g]], q_ref[0, half, 0, :, hss[g]], NT_DIMS,
                                preferred_element_type=F32)
            m8 = None
            for j in range(nt):
                rs = slice(j * blk, (j + 1) * blk)
                d = nt - 1 - j
                if d == 0:
                    sj = s[rs] + bias_scr[g, 0]
                elif d == 1:
                    sj = s[rs] + bias_scr[g, 1] + sel_scr[g, qb, j:j + 1, :]
                else:
                    sj = s[rs] + (sel_scr[g, qb, j:j + 1, :] + far)
                s_scr[half, g, rs, :] = sj
                mj = jnp.max(sj.reshape(blk // sub, sub, blk), axis=0)
                m8 = mj if m8 is None else jnp.maximum(m8, mj)
            ms.append(jnp.max(m8, axis=0, keepdims=True))
        for g in range(group):
            l8 = jnp.zeros((sub, blk), F32)
            acc = None
            for j in range(nt):
                rs = slice(j * blk, (j + 1) * blk)
                p = jnp.exp2(s_scr[half, g, rs, :] - ms[g])
                l8 = l8 + jnp.sum(p.reshape(blk // sub, sub, blk), axis=0)
                pv = jnp.dot(vt_scr[g, :, rs], p.astype(BF16), preferred_element_type=F32)
                acc = pv if acc is None else acc + pv
            l = jnp.sum(l8, axis=0, keepdims=True)
            o_ref[0, half, 0, :, hss[g]] = (acc / l).T.astype(o_ref.dtype)

    def attend_pair(tt):
        attend(0, tt)
        attend(1, tt + half_blocks)

    for tt in range(half_blocks):
        pl.when(t == tt)(functools.partial(attend_pair, tt))


def _moba(proj3, rel_bias, *, heads, group=4):
    bsz, seq, width = proj3.shape
    nb = seq // MOBA_BLOCK
    hb = nb // 2
    gw = group * HEAD_DIM
    cq, ck, cv = (c * (SEG // gw) for c in (COL_QA, COL_KA, COL_VA))
    proj5 = proj3.reshape(bsz, 2, hb, MOBA_BLOCK, width)
    pair_blk = (1, 2, 1, MOBA_BLOCK, gw)
    out = pl.pallas_call(
        functools.partial(_moba_kernel, n_blocks=nb, group=group),
        grid=(heads // group, bsz, hb),
        in_specs=[
            pl.BlockSpec(memory_space=pltpu.SMEM),
            pl.BlockSpec((1, seq, gw), lambda h, b, i: (b, 0, cq + h)),
            pl.BlockSpec(pair_blk, lambda h, b, i: (b, 0, i, 0, cq + h)),
            pl.BlockSpec((1, seq, gw), lambda h, b, i: (b, 0, ck + h)),
            pl.BlockSpec((1, seq, gw), lambda h, b, i: (b, 0, cv + h)),
        ],
        out_specs=pl.BlockSpec(pair_blk, lambda h, b, i: (b, 0, i, 0, h)),
        out_shape=jax.ShapeDtypeStruct((bsz, 2, hb, MOBA_BLOCK, heads * HEAD_DIM), BF16),
        scratch_shapes=[
            pltpu.VMEM((group, 2, MOBA_BLOCK, MOBA_BLOCK), F32),
            pltpu.VMEM((group, HEAD_DIM, seq), BF16),
            pltpu.VMEM((group, nb, 16, MOBA_BLOCK), F32),
            pltpu.VMEM((2, group, seq, MOBA_BLOCK), F32),
        ],
        compiler_params=_params("arbitrary", "arbitrary", "arbitrary"),
        name="moba",
    )(rel_bias, proj3, proj5, proj3, proj3)
    return out.reshape(bsz, seq, heads * HEAD_DIM)


TN_DIMS = (((0,), (0,)), ((), ()))


def _cumsum_rows(x, period):
    pos = lax.broadcasted_iota(jnp.int32, x.shape, 0) % period
    sh = 1
    while sh < period:
        x = x + jnp.where(pos >= sh, pltpu.roll(x, sh, axis=0), 0.0)
        sh *= 2
    return x


def _hgrn_kernel(lbl_ref, gain_ref, q_ref, f_ref, i_ref, g_ref, *rest, heads, layer, chunks, n_cast):
    w32_refs = rest[:n_cast]
    o_ref = rest[n_cast]
    w16_refs = rest[n_cast + 1:2 * n_cast + 1]
    st_scr, = rest[2 * n_cast + 1:]
    _cast_slabs(w32_refs, w16_refs)
    ch = HGRN_CHUNK
    dh = HEAD_DIM
    rows = chunks * ch

    @pl.when(pl.program_id(1) == 0)
    def _():
        st_scr[...] = jnp.zeros_like(st_scr)

    lg = lbl_ref[...]
    e = jnp.exp(lg - jnp.max(lg, axis=0, keepdims=True))
    lb = jnp.sum(e[:layer + 1], axis=0, keepdims=True) / jnp.sum(e, axis=0, keepdims=True)

    half = 0.5 * (1.0 - lb)
    f = (lb + half) + half * jnp.tanh(0.5 * f_ref[0])
    bcum = _cumsum_rows(jnp.log(f), ch)
    q_dec = (q_ref[0].astype(F32) * jnp.exp(bcum)).astype(BF16)
    k_dec32 = (1.0 - f) * jnp.exp(-bcum)
    k_dec = k_dec32.astype(BF16)
    decays = [jnp.exp(bcum[(c + 1) * ch - 1:(c + 1) * ch, :]) for c in range(chunks)]
    k_end = [(k_dec32[c * ch:(c + 1) * ch, :] * decays[c]).astype(BF16) for c in range(chunks)]
    v = i_ref[0]
    hg = 0.5 * g_ref[0].astype(F32)
    gate_gain = (hg + hg * jnp.tanh(hg)) * gain_ref[...]

    r = lax.broadcasted_iota(jnp.int32, (rows, rows), 0)
    c_ = lax.broadcasted_iota(jnp.int32, (rows, rows), 1)
    causal_in_chunk = (r >= c_) & (r // ch == c_ // ch)

    for hd in range(heads):
        sl = slice(hd * dh, (hd + 1) * dh)
        qd, vh = q_dec[:, sl], v[:, sl]
        a = lax.dot_general(qd, k_dec[:, sl], NT_DIMS, preferred_element_type=F32)
        a = jnp.where(causal_in_chunk, a, 0.0).astype(BF16)
        o_intra = jnp.dot(a, vh, preferred_element_type=F32)
        st = st_scr[hd]
        outs = []
        for c in range(chunks):
            rs = slice(c * ch, (c + 1) * ch)
            outs.append(o_intra[rs] + lax.dot_general(qd[rs], st.astype(BF16), NT_DIMS, preferred_element_type=F32))
            st = st * decays[c][:, sl] + lax.dot_general(vh[rs], k_end[c][:, sl], TN_DIMS,
                                                         preferred_element_type=F32)
        st_scr[hd] = st
        o = jnp.concatenate(outs, axis=0)
        ms = jnp.mean(o * o, axis=-1, keepdims=True)
        o_ref[0, :, sl] = (o * lax.rsqrt(ms + NORM_EPS) * gate_gain[:, sl]).astype(o_ref.dtype)


def _hgrn(proj3, fb3, lb_logits, gain, cast_weights, *, heads, layer, chunks=4):
    bsz, seq, _ = proj3.shape
    width = heads * HEAD_DIM
    rows = chunks * HGRN_CHUNK
    blk = (1, rows, width)
    nc = seq // rows
    cast_specs = _cast_specs(cast_weights, bsz * nc, lambda b, c: b * nc + c)
    outs = pl.pallas_call(
        functools.partial(_hgrn_kernel, heads=heads, layer=layer, chunks=chunks, n_cast=len(cast_weights)),
        grid=(bsz, nc),
        in_specs=[
            pl.BlockSpec(lb_logits.shape, lambda b, c: (0, 0)),
            pl.BlockSpec((1, width), lambda b, c: (0, 0)),
            pl.BlockSpec(blk, lambda b, c: (b, c, COL_QB)),
            pl.BlockSpec(blk, lambda b, c: (b, c, 0)),
            pl.BlockSpec(blk, lambda b, c: (b, c, COL_IB)),
            pl.BlockSpec(blk, lambda b, c: (b, c, COL_GB)),
            *cast_specs,
        ],
        out_specs=[pl.BlockSpec(blk, lambda b, c: (b, c, 0)), *cast_specs],
        out_shape=[jax.ShapeDtypeStruct((bsz, seq, width), BF16),
                   *(jax.ShapeDtypeStruct(w.shape, BF16) for w in cast_weights)],
        scratch_shapes=[pltpu.VMEM((heads, HEAD_DIM, HEAD_DIM), F32)],
        compiler_params=_params("arbitrary", "arbitrary"),
        name="hgrn2",
    )(lb_logits, gain, proj3, fb3, proj3, proj3, *cast_weights)
    return outs[0], outs[1:]


def _mix_kernel(ya_ref, yb_ref, ga0_ref, ga1_ref, gb0_ref, gb1_ref, x_ref, wa_ref, wb_ref, wo_ref, gain_ref,
                *rest, tc, n_cast):
    w32_refs = rest[:n_cast]
    x1_ref, h2_ref = rest[n_cast:n_cast + 2]
    w16_refs = rest[n_cast + 2:2 * n_cast + 2]
    mixed_scr, = rest[2 * n_cast + 2:]
    _cast_slabs(w32_refs, w16_refs)
    d = x_ref.shape[1]
    gate_refs = ((ga0_ref, gb0_ref), (ga1_ref, gb1_ref))
    ya = ya_ref[...]
    yb = yb_ref[...]
    for cb in range(d // tc):
        cs = slice(cb * tc, (cb + 1) * tc)
        ga_ref, gb_ref = gate_refs[(cb * tc) // SEG]
        gs = slice((cb * tc) % SEG, (cb * tc) % SEG + tc)
        pa = jnp.dot(ya, wa_ref[:, cs], preferred_element_type=F32)
        pb = jnp.dot(yb, wb_ref[:, cs], preferred_element_type=F32)
        mixed = (jax.nn.sigmoid(ga_ref[:, gs].astype(F32)) * pa
                 + jax.nn.sigmoid(gb_ref[:, gs].astype(F32)) * pb)
        mixed_scr[:, cs] = mixed.astype(BF16)
    ssq = jnp.zeros((x_ref.shape[0], 1), F32)
    for cb in range(d // tc):
        cs = slice(cb * tc, (cb + 1) * tc)
        x1 = x_ref[:, cs] + jnp.dot(mixed_scr[...], wo_ref[:, cs], preferred_element_type=F32)
        x1_ref[:, cs] = x1
        ssq = ssq + jnp.sum(x1 * x1, axis=-1, keepdims=True)
    inv = lax.rsqrt(ssq * (1.0 / d) + NORM_EPS)
    for cb in range(d // tc):
        cs = slice(cb * tc, (cb + 1) * tc)
        h2_ref[:, cs] = (x1_ref[:, cs] * inv * gain_ref[:, cs]).astype(BF16)


def _mix(ya, yb, proj, x2d, wa, wb, wo, gain, cast_weights, *, tm=512, tc=512):
    n, d = x2d.shape
    row = lambda i: (i, 0)
    seg = lambda c: pl.BlockSpec((tm, SEG), lambda i: (i, c))
    cast_specs = _cast_specs(cast_weights, n // tm, lambda i: i)
    outs = pl.pallas_call(
        functools.partial(_mix_kernel, tc=tc, n_cast=len(cast_weights)),
        grid=(n // tm,),
        in_specs=[
            pl.BlockSpec((tm, A_WIDTH), row),
            pl.BlockSpec((tm, B_WIDTH), row),
            seg(COL_GATE_A), seg(COL_GATE_A + 1), seg(COL_GATE_B), seg(COL_GATE_B + 1),
            pl.BlockSpec((tm, d), row),
            _resident(wa.shape), _resident(wb.shape), _resident(wo.shape), _resident(gain.shape),
            *cast_specs,
        ],
        out_specs=[pl.BlockSpec((tm, d), row), pl.BlockSpec((tm, d), row), *cast_specs],
        out_shape=[jax.ShapeDtypeStruct((n, d), F32), jax.ShapeDtypeStruct((n, d), BF16),
                   *(jax.ShapeDtypeStruct(w.shape, BF16) for w in cast_weights)],
        scratch_shapes=[pltpu.VMEM((tm, d), BF16)],
        compiler_params=_params("arbitrary"),
        name="mix",
    )(ya, yb, proj, proj, proj, proj, x2d, wa, wb, wo, gain, *cast_weights)
    return outs[0], outs[1], outs[2:]


def _ffn_kernel(h_ref, wg_ref, wu_ref, wd_ref, o_ref):
    @pl.when(pl.program_id(1) == 0)
    def _():
        o_ref[...] = jnp.zeros_like(o_ref)

    h = h_ref[...]
    g = jnp.dot(h, wg_ref[...], preferred_element_type=F32)
    u = jnp.dot(h, wu_ref[...], preferred_element_type=F32)
    act = (g * jax.nn.sigmoid(g) * u).astype(BF16)
    o_ref[...] += jnp.dot(act, wd_ref[...], preferred_element_type=F32)


def _ffn(h2, w_gu, w_down, *, tm=1024, tf=512):
    n, d = h2.shape
    dff = w_down.shape[0]
    nj = dff // tf
    return pl.pallas_call(
        _ffn_kernel,
        grid=(n // tm, nj),
        in_specs=[
            pl.BlockSpec((tm, d), lambda i, j: (i, 0)),
            pl.BlockSpec((d, tf), lambda i, j: (0, j)),
            pl.BlockSpec((d, tf), lambda i, j: (0, j + nj)),
            pl.BlockSpec((tf, d), lambda i, j: (j, 0)),
        ],
        out_specs=pl.BlockSpec((tm, d), lambda i, j: (i, 0)),
        out_shape=jax.ShapeDtypeStruct((n, d), F32),
        compiler_params=_params("arbitrary", "arbitrary"),
        name="ffn",
    )(h2, w_gu, w_gu, w_down)


def _ple_kernel(x1_ref, y_ref, p_ref, wg_ref, wp_ref, gain_ref, o_ref, x2_scr, *, tc, final_norm):
    d = x1_ref.shape[1]
    x2_scr[...] = x1_ref[...] + y_ref[...]
    xb = x2_scr[...].astype(BF16)
    pb = p_ref[...].astype(BF16)
    ssq = jnp.zeros((x1_ref.shape[0], 1), F32)
    for cb in range(d // tc):
        cs = slice(cb * tc, (cb + 1) * tc)
        gate = jax.nn.sigmoid(jnp.dot(xb, wg_ref[:, cs], preferred_element_type=F32))
        pe = jnp.dot(pb, wp_ref[:, cs], preferred_element_type=F32)
        x3 = x2_scr[:, cs] + gate * pe
        o_ref[:, cs] = x3
        ssq = ssq + jnp.sum(x3 * x3, axis=-1, keepdims=True)
    if not final_norm:
        return
    inv = lax.rsqrt(ssq * (1.0 / d) + NORM_EPS)
    for cb in range(d // tc):
        cs = slice(cb * tc, (cb + 1) * tc)
        o_ref[:, cs] = o_ref[:, cs] * inv * gain_ref[:, cs]


def _ple(x1, y, p2d, wg, wp, gain, *, final_norm, tm=512, tc=512):
    n, d = x1.shape
    row = lambda i: (i, 0)
    return pl.pallas_call(
        functools.partial(_ple_kernel, tc=tc, final_norm=final_norm),
        grid=(n // tm,),
        in_specs=[
            pl.BlockSpec((tm, d), row),
            pl.BlockSpec((tm, d), row),
            pl.BlockSpec((tm, p2d.shape[1]), row),
            _resident(wg.shape), _resident(wp.shape), _resident(gain.shape),
        ],
        out_specs=pl.BlockSpec((tm, d), row),
        out_shape=jax.ShapeDtypeStruct((n, d), F32),
        scratch_shapes=[pltpu.VMEM((tm, d), F32)],
        compiler_params=_params("arbitrary"),
        name="ple",
    )(x1, y, p2d, wg, wp, gain)


def kernel(x, p, norm_mix, w_in, hgrn_norm, w_proj_a, w_proj_b, w_out, norm_ffn, w_gate_up, w_down, w_ple,
           w_ple_gate, rel_bias, hgrn_lb_logits, norm_final):
    bsz, seq, d = x.shape
    n = bsz * seq
    depth = w_in.shape[0]
    a_heads = A_WIDTH // HEAD_DIM
    b_heads = B_WIDTH // HEAD_DIM

    cscale = np.ones((1, IN_WIDTH), np.float32)
    cscale[:, COL_QA * SEG:(COL_QA + 1) * SEG] = LOG2E / math.sqrt(HEAD_DIM)
    cscale = jnp.asarray(cscale)

    xc = x.reshape(n, d)
    for i in range(depth):
        proj, fb = _in_proj(xc, norm_mix[i][None], w_in[i], cscale)
        proj3 = proj.reshape(bsz, seq, IN_WIDTH)
        ya = _moba(proj3, rel_bias, heads=a_heads)
        yb, (wa, wb, wo, wgu, wdn) = _hgrn(proj3, fb.reshape(bsz, seq, SEG), hgrn_lb_logits, hgrn_norm[i][None],
                                           (w_proj_a[i], w_proj_b[i], w_out[i], w_gate_up[i], w_down[i]),
                                           heads=b_heads, layer=i)
        x1, h2, (wpg, wpl) = _mix(ya.reshape(n, A_WIDTH), yb.reshape(n, B_WIDTH), proj, xc, wa, wb, wo,
                                  norm_ffn[i][None], (w_ple_gate[i], w_ple[i]))
        y = _ffn(h2, wgu, wdn)
        xc = _ple(x1, y, p[i].reshape(n, -1), wpg, wpl, norm_final[None], final_norm=(i == depth - 1))
    return xc.reshape(bsz, seq, d)
```

```python
import functools
import math

import jax
import jax.numpy as jnp
import numpy as np
from jax import lax
from jax.experimental import pallas as pl
from jax.experimental.pallas import tpu as pltpu

F32 = jnp.float32
BF16 = jnp.bfloat16

D_MODEL = 2048
HEAD_DIM = 128
A_WIDTH = 1024
B_WIDTH = 1024
MOBA_BLOCK = 256
MOBA_TOPK = 3
REL_BUCKETS = 32
REL_MAX_EXACT = 16
REL_MAX_DIST = 128
HGRN_CHUNK = 64
D_FF = 5632
NORM_EPS = 1e-6

COL_QA, COL_KA, COL_VA, COL_QB, COL_FB, COL_IB, COL_GB, COL_GATE_A, COL_GATE_B = 0, 1, 2, 3, 4, 5, 6, 7, 9
IN_WIDTH = 11 * 1024
SEG = 1024

BF16_SUBLANES = 16
MASK_VALUE = -1e30
VMEM_LIMIT = 56 * 1024 * 1024

NT_DIMS = (((1,), (1,)), ((), ()))
LOG2E = math.log2(math.e)


def _bucket_thresholds():
    n = np.arange(0, 4096, dtype=np.int64)
    nf = np.maximum(n, REL_MAX_EXACT).astype(np.float64)
    large = REL_MAX_EXACT + (np.log(nf / REL_MAX_EXACT) / math.log(REL_MAX_DIST / REL_MAX_EXACT)
                             * (REL_BUCKETS - REL_MAX_EXACT)).astype(np.int64)
    large = np.minimum(large, REL_BUCKETS - 1)
    bucket = np.where(n < REL_MAX_EXACT, n, large)
    return [int(np.argmax(bucket >= k)) for k in range(REL_BUCKETS)]


BUCKET_LO = _bucket_thresholds()


def _params(*sem):
    return pltpu.CompilerParams(dimension_semantics=sem, vmem_limit_bytes=VMEM_LIMIT)


def _resident(shape):
    return pl.BlockSpec(shape, lambda *_: (0,) * len(shape), pipeline_mode=pl.Buffered(1))


def _cast_specs(weights, n_steps, step_index):
    specs = []
    for w in weights:
        r = BF16_SUBLANES
        while w.shape[0] % r or w.shape[0] // r > n_steps:
            r += BF16_SUBLANES
        last = w.shape[0] // r - 1
        specs.append(pl.BlockSpec((r, w.shape[1]), lambda *ids, last=last: (jnp.minimum(step_index(*ids), last), 0)))
    return specs


def _cast_slabs(w32_refs, w16_refs):
    for w32, w16 in zip(w32_refs, w16_refs):
        w16[...] = w32[...].astype(BF16)


def _in_proj_kernel(x_ref, gain_ref, w_ref, cscale_ref, *rest, rows_per_step, first_tile):
    if first_tile:
        o_ref, fb_ref, w16_ref, h_scr = rest
    else:
        _, _, o_ref, fb_ref, h_scr = rest
    j = pl.program_id(1)

    @pl.when(j == 0)
    def _():
        def body(r, _):
            rs = pl.ds(pl.multiple_of(r * rows_per_step, rows_per_step), rows_per_step)
            x = x_ref[rs, :]
            ms = jnp.mean(x * x, axis=-1, keepdims=True)
            h_scr[rs, :] = (x * lax.rsqrt(ms + NORM_EPS) * gain_ref[...]).astype(BF16)
            return 0
        lax.fori_loop(0, x_ref.shape[0] // rows_per_step, body, 0)

    if first_tile:
        w = w_ref[...].astype(BF16)
        w16_ref[...] = w
    else:
        w = w_ref[...]
    acc = jnp.dot(h_scr[...], w, preferred_element_type=F32)
    o_ref[...] = (acc * cscale_ref[...]).astype(BF16)
    pltpu.store(fb_ref, acc, mask=jnp.broadcast_to(j == COL_FB, acc.shape))


def _in_proj(x2d, gain, w_f32, cscale, *, tm=1024, tn=SEG):
    n, d = x2d.shape
    width = w_f32.shape[1]
    nj = width // tn
    out_shape = [jax.ShapeDtypeStruct((n, width), BF16), jax.ShapeDtypeStruct((n, SEG), F32)]
    scratch = [pltpu.VMEM((tm, d), BF16)]

    def specs(row0, x_mode):
        ins = [
            pl.BlockSpec((tm, d), lambda i, j: (i + row0, 0), **x_mode),
            pl.BlockSpec((1, d), lambda i, j: (0, 0)),
            pl.BlockSpec((d, tn), lambda i, j: (0, j)),
            pl.BlockSpec((1, tn), lambda i, j: (0, j)),
        ]
        outs = [
            pl.BlockSpec((tm, tn), lambda i, j: (i + row0, j)),
            pl.BlockSpec((tm, SEG), lambda i, j: (i + row0, 0)),
        ]
        return ins, outs

    ins, outs = specs(0, dict(pipeline_mode=pl.Buffered(1)))
    proj, fb, w16 = pl.pallas_call(
        functools.partial(_in_proj_kernel, rows_per_step=128, first_tile=True),
        grid=(1, nj),
        in_specs=ins,
        out_specs=outs + [pl.BlockSpec((d, tn), lambda i, j: (0, j))],
        out_shape=out_shape + [jax.ShapeDtypeStruct((d, width), BF16)],
        scratch_shapes=scratch,
        compiler_params=_params("arbitrary", "arbitrary"),
        name="in_proj_first",
    )(x2d, gain, w_f32, cscale)

    ins, outs = specs(1, {})
    any_spec = pl.BlockSpec(memory_space=pl.ANY)
    return pl.pallas_call(
        functools.partial(_in_proj_kernel, rows_per_step=128, first_tile=False),
        grid=(n // tm - 1, nj),
        in_specs=ins + [any_spec, any_spec],
        out_specs=outs,
        out_shape=out_shape,
        input_output_aliases={4: 0, 5: 1},
        scratch_shapes=scratch,
        compiler_params=_params("arbitrary", "arbitrary"),
        name="in_proj",
    )(x2d, gain, w16, cscale, proj, fb)


def _moba_kernel(rb_ref, qall_ref, q_ref, k_ref, v_ref, o_ref, bias_scr, vt_scr, sel_scr, s_scr, *, n_blocks, group,
                 q_per_step):
    hg = pl.program_id(0)
    b = pl.program_id(1)
    t = pl.program_id(2)
    blk = MOBA_BLOCK
    dh = HEAD_DIM
    sub = 8
    stride = n_blocks // q_per_step

    @pl.when((b == 0) & (t == 0))
    def _build_bias():
        r = lax.broadcasted_iota(jnp.int32, (blk, blk), 1)
        c = lax.broadcasted_iota(jnp.int32, (blk, blk), 0)
        for g in range(group):
            hd = hg * group + g
            for d in range(2):
                rel = d * blk + r - c
                bias = jnp.full((blk, blk), rb_ref[REL_BUCKETS - 1, hd] * LOG2E, F32)
                for kk in range(REL_BUCKETS - 2, -1, -1):
                    bias = jnp.where(rel < BUCKET_LO[kk + 1], rb_ref[kk, hd] * LOG2E, bias)
                if d == 0:
                    bias = jnp.where(rel < 0, MASK_VALUE, bias)
                bias_scr[g, d] = bias

    @pl.when(t == 0)
    def _per_sequence():
        for g in range(group):
            hs = slice(g * dh, (g + 1) * dh)
            means = []
            for jb in range(n_blocks):
                rs = slice(jb * blk, (jb + 1) * blk)
                means.append(jnp.mean(k_ref[0, rs, hs].astype(F32), axis=0, keepdims=True))
                vt_scr[g, :, rs] = v_ref[0, rs, hs].T
            km = jnp.concatenate(means + [jnp.zeros((16 - n_blocks, dh), F32)], axis=0)

            km_hi = km.astype(BF16)
            km_lo = (km - km_hi.astype(F32)).astype(BF16)
            qall = qall_ref[0, :, hs]
            sc = (lax.dot_general(km_hi, qall, NT_DIMS, preferred_element_type=F32)
                  + lax.dot_general(km_lo, qall, NT_DIMS, preferred_element_type=F32))
            rows = lax.broadcasted_iota(jnp.int32, sc.shape, 0)
            rows_f = rows.astype(F32)
            past = rows < lax.broadcasted_iota(jnp.int32, sc.shape, 1) // blk
            s = jnp.where(past, sc, -jnp.inf)
            picked = jnp.zeros(sc.shape, jnp.bool_)
            for _ in range(MOBA_TOPK):
                top = jnp.max(s, axis=0, keepdims=True)
                first = jnp.min(jnp.where(s == top, rows_f, 1e9), axis=0, keepdims=True)
                hit = rows_f == first
                picked = picked | hit
                s = jnp.where(hit, -jnp.inf, s)
            mask = jnp.where(picked & past, 0.0, MASK_VALUE)
            for qb in range(n_blocks):
                sel_scr[g, qb] = mask[:, qb * blk:(qb + 1) * blk]

    def attend(slot, qb, row0):
        nt = qb + 1
        hss = [slice(g * dh, (g + 1) * dh) for g in range(group)]
        ms = []
        for g in range(group):
            far = rb_ref[REL_BUCKETS - 1, hg * group + g] * LOG2E
            s = lax.dot_general(k_ref[0, :nt * blk, hss[g]], q_ref[0, slot, 0, :, hss[g]], NT_DIMS,
                                preferred_element_type=F32)
            m8 = None
            for j in range(nt):
                rs = slice(j * blk, (j + 1) * blk)
                d = nt - 1 - j
                if d == 0:
                    sj = s[rs] + bias_scr[g, 0]
                elif d == 1:
                    sj = s[rs] + bias_scr[g, 1] + sel_scr[g, qb, j:j + 1, :]
                else:
                    sj = s[rs] + (sel_scr[g, qb, j:j + 1, :] + far)
                s_scr[g, row0 + j * blk:row0 + (j + 1) * blk, :] = sj
                mj = jnp.max(sj.reshape(blk // sub, sub, blk), axis=0)
                m8 = mj if m8 is None else jnp.maximum(m8, mj)
            ms.append(jnp.max(m8, axis=0, keepdims=True))
        for g in range(group):
            l8 = jnp.zeros((sub, blk), F32)
            acc = None
            for j in range(nt):
                rs = slice(j * blk, (j + 1) * blk)
                p = jnp.exp2(s_scr[g, row0 + j * blk:row0 + (j + 1) * blk, :] - ms[g])
                l8 = l8 + jnp.sum(p.reshape(blk // sub, sub, blk), axis=0)
                pv = jnp.dot(vt_scr[g, :, rs], p.astype(BF16), preferred_element_type=F32)
                acc = pv if acc is None else acc + pv
            l = jnp.sum(l8, axis=0, keepdims=True)
            o_ref[0, slot, 0, :, hss[g]] = (acc / l).T.astype(o_ref.dtype)

    def attend_set(tt):
        row0 = 0
        for slot in range(q_per_step):
            qb = tt + slot * stride
            attend(slot, qb, row0)
            row0 += (qb + 1) * blk

    for tt in range(stride):
        pl.when(t == tt)(functools.partial(attend_set, tt))


def _moba(proj3, rel_bias, *, heads, group=4, q_per_step=4):
    bsz, seq, width = proj3.shape
    nb = seq // MOBA_BLOCK
    stride = nb // q_per_step
    gw = group * HEAD_DIM
    cq, ck, cv = (c * (SEG // gw) for c in (COL_QA, COL_KA, COL_VA))
    proj5 = proj3.reshape(bsz, q_per_step, stride, MOBA_BLOCK, width)
    q_blk = (1, q_per_step, 1, MOBA_BLOCK, gw)
    s_rows = sum(nb - k * stride for k in range(q_per_step)) * MOBA_BLOCK
    out = pl.pallas_call(
        functools.partial(_moba_kernel, n_blocks=nb, group=group, q_per_step=q_per_step),
        grid=(heads // group, bsz, stride),
        in_specs=[
            pl.BlockSpec(memory_space=pltpu.SMEM),
            pl.BlockSpec((1, seq, gw), lambda h, b, i: (b, 0, cq + h)),
            pl.BlockSpec(q_blk, lambda h, b, i: (b, 0, i, 0, cq + h)),
            pl.BlockSpec((1, seq, gw), lambda h, b, i: (b, 0, ck + h)),
            pl.BlockSpec((1, seq, gw), lambda h, b, i: (b, 0, cv + h)),
        ],
        out_specs=pl.BlockSpec(q_blk, lambda h, b, i: (b, 0, i, 0, h)),
        out_shape=jax.ShapeDtypeStruct((bsz, q_per_step, stride, MOBA_BLOCK, heads * HEAD_DIM), BF16),
        scratch_shapes=[
            pltpu.VMEM((group, 2, MOBA_BLOCK, MOBA_BLOCK), F32),
            pltpu.VMEM((group, HEAD_DIM, seq), BF16),
            pltpu.VMEM((group, nb, 16, MOBA_BLOCK), F32),
            pltpu.VMEM((group, s_rows, MOBA_BLOCK), F32),
        ],
        compiler_params=_params("arbitrary", "arbitrary", "arbitrary"),
        name="moba",
    )(rel_bias, proj3, proj5, proj3, proj3)
    return out.reshape(bsz, seq, heads * HEAD_DIM)


TN_DIMS = (((0,), (0,)), ((), ()))


def _cumsum_rows(x, period):
    pos = lax.broadcasted_iota(jnp.int32, x.shape, 0) % period
    sh = 1
    while sh < period:
        x = x + jnp.where(pos >= sh, pltpu.roll(x, sh, axis=0), 0.0)
        sh *= 2
    return x


def _hgrn_kernel(lbl_ref, gain_ref, q_ref, f_ref, i_ref, g_ref, *rest, heads, layer, chunks, n_cast):
    w32_refs = rest[:n_cast]
    o_ref = rest[n_cast]
    w16_refs = rest[n_cast + 1:2 * n_cast + 1]
    st_scr, = rest[2 * n_cast + 1:]
    _cast_slabs(w32_refs, w16_refs)
    ch = HGRN_CHUNK
    dh = HEAD_DIM
    rows = chunks * ch

    @pl.when(pl.program_id(1) == 0)
    def _():
        st_scr[...] = jnp.zeros_like(st_scr)

    lg = lbl_ref[...]
    e = jnp.exp(lg - jnp.max(lg, axis=0, keepdims=True))
    lb = jnp.sum(e[:layer + 1], axis=0, keepdims=True) / jnp.sum(e, axis=0, keepdims=True)

    half = 0.5 * (1.0 - lb)
    f = (lb + half) + half * jnp.tanh(0.5 * f_ref[0])
    bcum = _cumsum_rows(jnp.log(f), ch)
    q_dec = (q_ref[0].astype(F32) * jnp.exp(bcum)).astype(BF16)
    k_dec32 = (1.0 - f) * jnp.exp(-bcum)
    k_dec = k_dec32.astype(BF16)
    decays = [jnp.exp(bcum[(c + 1) * ch - 1:(c + 1) * ch, :]) for c in range(chunks)]
    k_end = [(k_dec32[c * ch:(c + 1) * ch, :] * decays[c]).astype(BF16) for c in range(chunks)]
    v = i_ref[0]
    hg = 0.5 * g_ref[0].astype(F32)
    gate_gain = (hg + hg * jnp.tanh(hg)) * gain_ref[...]

    r = lax.broadcasted_iota(jnp.int32, (rows, rows), 0)
    c_ = lax.broadcasted_iota(jnp.int32, (rows, rows), 1)
    causal_in_chunk = (r >= c_) & (r // ch == c_ // ch)

    for hd in range(heads):
        sl = slice(hd * dh, (hd + 1) * dh)
        qd, vh = q_dec[:, sl], v[:, sl]
        a = lax.dot_general(qd, k_dec[:, sl], NT_DIMS, preferred_element_type=F32)
        a = jnp.where(causal_in_chunk, a, 0.0).astype(BF16)
        o_intra = jnp.dot(a, vh, preferred_element_type=F32)
        st = st_scr[hd]
        outs = []
        for c in range(chunks):
            rs = slice(c * ch, (c + 1) * ch)
            outs.append(o_intra[rs] + lax.dot_general(qd[rs], st.astype(BF16), NT_DIMS, preferred_element_type=F32))
            st = st * decays[c][:, sl] + lax.dot_general(vh[rs], k_end[c][:, sl], TN_DIMS,
                                                         preferred_element_type=F32)
        st_scr[hd] = st
        o = jnp.concatenate(outs, axis=0)
        ms = jnp.mean(o * o, axis=-1, keepdims=True)
        o_ref[0, :, sl] = (o * lax.rsqrt(ms + NORM_EPS) * gate_gain[:, sl]).astype(o_ref.dtype)


def _hgrn(proj3, fb3, lb_logits, gain, cast_weights, *, heads, layer, chunks=4):
    bsz, seq, _ = proj3.shape
    width = heads * HEAD_DIM
    rows = chunks * HGRN_CHUNK
    blk = (1, rows, width)
    nc = seq // rows
    cast_specs = _cast_specs(cast_weights, bsz * nc, lambda b, c: b * nc + c)
    outs = pl.pallas_call(
        functools.partial(_hgrn_kernel, heads=heads, layer=layer, chunks=chunks, n_cast=len(cast_weights)),
        grid=(bsz, nc),
        in_specs=[
            pl.BlockSpec(lb_logits.shape, lambda b, c: (0, 0)),
            pl.BlockSpec((1, width), lambda b, c: (0, 0)),
            pl.BlockSpec(blk, lambda b, c: (b, c, COL_QB)),
            pl.BlockSpec(blk, lambda b, c: (b, c, 0)),
            pl.BlockSpec(blk, lambda b, c: (b, c, COL_IB)),
            pl.BlockSpec(blk, lambda b, c: (b, c, COL_GB)),
            *cast_specs,
        ],
        out_specs=[pl.BlockSpec(blk, lambda b, c: (b, c, 0)), *cast_specs],
        out_shape=[jax.ShapeDtypeStruct((bsz, seq, width), BF16),
                   *(jax.ShapeDtypeStruct(w.shape, BF16) for w in cast_weights)],
        scratch_shapes=[pltpu.VMEM((heads, HEAD_DIM, HEAD_DIM), F32)],
        compiler_params=_params("arbitrary", "arbitrary"),
        name="hgrn2",
    )(lb_logits, gain, proj3, fb3, proj3, proj3, *cast_weights)
    return outs[0], outs[1:]


def _mix_kernel(ya_ref, yb_ref, ga0_ref, ga1_ref, gb0_ref, gb1_ref, x_ref, wa_ref, wb_ref, wo_ref, gain_ref,
                *rest, tc, n_cast):
    w32_refs = rest[:n_cast]
    x1_ref, h2_ref = rest[n_cast:n_cast + 2]
    w16_refs = rest[n_cast + 2:2 * n_cast + 2]
    mixed_scr, = rest[2 * n_cast + 2:]
    _cast_slabs(w32_refs, w16_refs)
    d = x_ref.shape[1]
    gate_refs = ((ga0_ref, gb0_ref), (ga1_ref, gb1_ref))
    ya = ya_ref[...]
    yb = yb_ref[...]
    for cb in range(d // tc):
        cs = slice(cb * tc, (cb + 1) * tc)
        ga_ref, gb_ref = gate_refs[(cb * tc) // SEG]
        gs = slice((cb * tc) % SEG, (cb * tc) % SEG + tc)
        pa = jnp.dot(ya, wa_ref[:, cs], preferred_element_type=F32)
        pb = jnp.dot(yb, wb_ref[:, cs], preferred_element_type=F32)
        mixed = (jax.nn.sigmoid(ga_ref[:, gs].astype(F32)) * pa
                 + jax.nn.sigmoid(gb_ref[:, gs].astype(F32)) * pb)
        mixed_scr[:, cs] = mixed.astype(BF16)
    ssq = jnp.zeros((x_ref.shape[0], 1), F32)
    for cb in range(d // tc):
        cs = slice(cb * tc, (cb + 1) * tc)
        x1 = x_ref[:, cs] + jnp.dot(mixed_scr[...], wo_ref[:, cs], preferred_element_type=F32)
        x1_ref[:, cs] = x1
        ssq = ssq + jnp.sum(x1 * x1, axis=-1, keepdims=True)
    inv = lax.rsqrt(ssq * (1.0 / d) + NORM_EPS)
    for cb in range(d // tc):
        cs = slice(cb * tc, (cb + 1) * tc)
        h2_ref[:, cs] = (x1_ref[:, cs] * inv * gain_ref[:, cs]).astype(BF16)


def _mix(ya, yb, proj, x2d, wa, wb, wo, gain, cast_weights, *, tm=512, tc=512):
    n, d = x2d.shape
    row = lambda i: (i, 0)
    seg = lambda c: pl.BlockSpec((tm, SEG), lambda i: (i, c))
    cast_specs = _cast_specs(cast_weights, n // tm, lambda i: i)
    outs = pl.pallas_call(
        functools.partial(_mix_kernel, tc=tc, n_cast=len(cast_weights)),
        grid=(n // tm,),
        in_specs=[
            pl.BlockSpec((tm, A_WIDTH), row),
            pl.BlockSpec((tm, B_WIDTH), row),
            seg(COL_GATE_A), seg(COL_GATE_A + 1), seg(COL_GATE_B), seg(COL_GATE_B + 1),
            pl.BlockSpec((tm, d), row),
            _resident(wa.shape), _resident(wb.shape), _resident(wo.shape), _resident(gain.shape),
            *cast_specs,
        ],
        out_specs=[pl.BlockSpec((tm, d), row), pl.BlockSpec((tm, d), row), *cast_specs],
        out_shape=[jax.ShapeDtypeStruct((n, d), F32), jax.ShapeDtypeStruct((n, d), BF16),
                   *(jax.ShapeDtypeStruct(w.shape, BF16) for w in cast_weights)],
        scratch_shapes=[pltpu.VMEM((tm, d), BF16)],
        compiler_params=_params("arbitrary"),
        name="mix",
    )(ya, yb, proj, proj, proj, proj, x2d, wa, wb, wo, gain, *cast_weights)
    return outs[0], outs[1], outs[2:]


def _ffn_kernel(h_ref, wg_ref, wu_ref, wd_ref, o_ref):
    @pl.when(pl.program_id(1) == 0)
    def _():
        o_ref[...] = jnp.zeros_like(o_ref)

    h = h_ref[...]
    g = jnp.dot(h, wg_ref[...], preferred_element_type=F32)
    u = jnp.dot(h, wu_ref[...], preferred_element_type=F32)
    act = (g * jax.nn.sigmoid(g) * u).astype(BF16)
    o_ref[...] += jnp.dot(act, wd_ref[...], preferred_element_type=F32)


def _ffn(h2, w_gu, w_down, *, tm=1024, tf=512):
    n, d = h2.shape
    dff = w_down.shape[0]
    nj = dff // tf
    return pl.pallas_call(
        _ffn_kernel,
        grid=(n // tm, nj),
        in_specs=[
            pl.BlockSpec((tm, d), lambda i, j: (i, 0)),
            pl.BlockSpec((d, tf), lambda i, j: (0, j)),
            pl.BlockSpec((d, tf), lambda i, j: (0, j + nj)),
            pl.BlockSpec((tf, d), lambda i, j: (j, 0)),
        ],
        out_specs=pl.BlockSpec((tm, d), lambda i, j: (i, 0)),
        out_shape=jax.ShapeDtypeStruct((n, d), F32),
        compiler_params=_params("arbitrary", "arbitrary"),
        name="ffn",
    )(h2, w_gu, w_gu, w_down)


def _ple_kernel(x1_ref, y_ref, p_ref, wg_ref, wp_ref, gain_ref, o_ref, x2_scr, *, tc, final_norm):
    d = x1_ref.shape[1]
    x2_scr[...] = x1_ref[...] + y_ref[...]
    xb = x2_scr[...].astype(BF16)
    pb = p_ref[...].astype(BF16)
    ssq = jnp.zeros((x1_ref.shape[0], 1), F32)
    for cb in range(d // tc):
        cs = slice(cb * tc, (cb + 1) * tc)
        gate = jax.nn.sigmoid(jnp.dot(xb, wg_ref[:, cs], preferred_element_type=F32))
        pe = jnp.dot(pb, wp_ref[:, cs], preferred_element_type=F32)
        x3 = x2_scr[:, cs] + gate * pe
        o_ref[:, cs] = x3
        ssq = ssq + jnp.sum(x3 * x3, axis=-1, keepdims=True)
    if not final_norm:
        return
    inv = lax.rsqrt(ssq * (1.0 / d) + NORM_EPS)
    for cb in range(d // tc):
        cs = slice(cb * tc, (cb + 1) * tc)
        o_ref[:, cs] = o_ref[:, cs] * inv * gain_ref[:, cs]


def _ple(x1, y, p2d, wg, wp, gain, *, final_norm, tm=512, tc=512):
    n, d = x1.shape
    row = lambda i: (i, 0)
    return pl.pallas_call(
        functools.partial(_ple_kernel, tc=tc, final_norm=final_norm),
        grid=(n // tm,),
        in_specs=[
            pl.BlockSpec((tm, d), row),
            pl.BlockSpec((tm, d), row),
            pl.BlockSpec((tm, p2d.shape[1]), row),
            _resident(wg.shape), _resident(wp.shape), _resident(gain.shape),
        ],
        out_specs=pl.BlockSpec((tm, d), row),
        out_shape=jax.ShapeDtypeStruct((n, d), F32),
        scratch_shapes=[pltpu.VMEM((tm, d), F32)],
        compiler_params=_params("arbitrary"),
        name="ple",
    )(x1, y, p2d, wg, wp, gain)


def kernel(x, p, norm_mix, w_in, hgrn_norm, w_proj_a, w_proj_b, w_out, norm_ffn, w_gate_up, w_down, w_ple,
           w_ple_gate, rel_bias, hgrn_lb_logits, norm_final):
    bsz, seq, d = x.shape
    n = bsz * seq
    depth = w_in.shape[0]
    a_heads = A_WIDTH // HEAD_DIM
    b_heads = B_WIDTH // HEAD_DIM

    cscale = np.ones((1, IN_WIDTH), np.float32)
    cscale[:, COL_QA * SEG:(COL_QA + 1) * SEG] = LOG2E / math.sqrt(HEAD_DIM)
    cscale = jnp.asarray(cscale)

    xc = x.reshape(n, d)
    for i in range(depth):
        proj, fb = _in_proj(xc, norm_mix[i][None], w_in[i], cscale)
        proj3 = proj.reshape(bsz, seq, IN_WIDTH)
        ya = _moba(proj3, rel_bias, heads=a_heads)
        yb, (wa, wb, wo, wgu, wdn) = _hgrn(proj3, fb.reshape(bsz, seq, SEG), hgrn_lb_logits, hgrn_norm[i][None],
                                           (w_proj_a[i], w_proj_b[i], w_out[i], w_gate_up[i], w_down[i]),
                                           heads=b_heads, layer=i)
        x1, h2, (wpg, wpl) = _mix(ya.reshape(n, A_WIDTH), yb.reshape(n, B_WIDTH), proj, xc, wa, wb, wo,
                                  norm_ffn[i][None], (w_ple_gate[i], w_ple[i]))
        y = _ffn(h2, wgu, wdn)
        xc = _ple(x1, y, p[i].reshape(n, -1), wpg, wpl, norm_final[None], final_norm=(i == depth - 1))
    return xc.reshape(bsz, seq, d)
```

```python
import functools
import math

import jax
import jax.numpy as jnp
import numpy as np
from jax import lax
from jax.experimental import pallas as pl
from jax.experimental.pallas import tpu as pltpu

F32 = jnp.float32
BF16 = jnp.bfloat16

D_MODEL = 2048
HEAD_DIM = 128
A_WIDTH = 1024
B_WIDTH = 1024
MOBA_BLOCK = 256
MOBA_TOPK = 3
REL_BUCKETS = 32
REL_MAX_EXACT = 16
REL_MAX_DIST = 128
HGRN_CHUNK = 64
D_FF = 5632
NORM_EPS = 1e-6

COL_QA, COL_KA, COL_VA, COL_QB, COL_FB, COL_IB, COL_GB, COL_GATE_A, COL_GATE_B = 0, 1, 2, 3, 4, 5, 6, 7, 9
IN_WIDTH = 11 * 1024
SEG = 1024

BF16_SUBLANES = 16
MASK_VALUE = -1e30
VMEM_LIMIT = 56 * 1024 * 1024

NT_DIMS = (((1,), (1,)), ((), ()))
LOG2E = math.log2(math.e)


def _bucket_thresholds():
    n = np.arange(0, 4096, dtype=np.int64)
    nf = np.maximum(n, REL_MAX_EXACT).astype(np.float64)
    large = REL_MAX_EXACT + (np.log(nf / REL_MAX_EXACT) / math.log(REL_MAX_DIST / REL_MAX_EXACT)
                             * (REL_BUCKETS - REL_MAX_EXACT)).astype(np.int64)
    large = np.minimum(large, REL_BUCKETS - 1)
    bucket = np.where(n < REL_MAX_EXACT, n, large)
    return [int(np.argmax(bucket >= k)) for k in range(REL_BUCKETS)]


BUCKET_LO = _bucket_thresholds()


def _params(*sem):
    return pltpu.CompilerParams(dimension_semantics=sem, vmem_limit_bytes=VMEM_LIMIT)


def _resident(shape):
    return pl.BlockSpec(shape, lambda *_: (0,) * len(shape), pipeline_mode=pl.Buffered(1))


def _cast_specs(weights, n_steps, step_index):
    specs = []
    for w in weights:
        r = BF16_SUBLANES
        while w.shape[0] % r or w.shape[0] // r > n_steps:
            r += BF16_SUBLANES
        last = w.shape[0] // r - 1
        specs.append(pl.BlockSpec((r, w.shape[1]), lambda *ids, last=last: (jnp.minimum(step_index(*ids), last), 0)))
    return specs


def _cast_slabs(w32_refs, w16_refs):
    for w32, w16 in zip(w32_refs, w16_refs):
        w16[...] = w32[...].astype(BF16)


def _in_proj_kernel(x_ref, gain_ref, w_ref, cscale_ref, *rest, rows_per_step, first_tile):
    if first_tile:
        o_ref, fb_ref, w16_ref, h_scr = rest
    else:
        _, _, o_ref, fb_ref, h_scr = rest
    j = pl.program_id(1)

    @pl.when(j == 0)
    def _():
        def body(r, _):
            rs = pl.ds(pl.multiple_of(r * rows_per_step, rows_per_step), rows_per_step)
            x = x_ref[rs, :]
            ms = jnp.mean(x * x, axis=-1, keepdims=True)
            h_scr[rs, :] = (x * lax.rsqrt(ms + NORM_EPS) * gain_ref[...]).astype(BF16)
            return 0
        lax.fori_loop(0, x_ref.shape[0] // rows_per_step, body, 0)

    if first_tile:
        w = w_ref[...].astype(BF16)
        w16_ref[...] = w
    else:
        w = w_ref[...]
    acc = jnp.dot(h_scr[...], w, preferred_element_type=F32)
    o_ref[...] = (acc * cscale_ref[...]).astype(BF16)
    pltpu.store(fb_ref, acc, mask=jnp.broadcast_to(j == COL_FB, acc.shape))


def _in_proj(x2d, gain, w_f32, cscale, *, tm=1024, tn=SEG):
    n, d = x2d.shape
    width = w_f32.shape[1]
    nj = width // tn
    out_shape = [jax.ShapeDtypeStruct((n, width), BF16), jax.ShapeDtypeStruct((n, SEG), F32)]
    scratch = [pltpu.VMEM((tm, d), BF16)]

    def specs(row0, x_mode):
        ins = [
            pl.BlockSpec((tm, d), lambda i, j: (i + row0, 0), **x_mode),
            pl.BlockSpec((1, d), lambda i, j: (0, 0)),
            pl.BlockSpec((d, tn), lambda i, j: (0, j)),
            pl.BlockSpec((1, tn), lambda i, j: (0, j)),
        ]
        outs = [
            pl.BlockSpec((tm, tn), lambda i, j: (i + row0, j)),
            pl.BlockSpec((tm, SEG), lambda i, j: (i + row0, 0)),
        ]
        return ins, outs

    ins, outs = specs(0, dict(pipeline_mode=pl.Buffered(1)))
    proj, fb, w16 = pl.pallas_call(
        functools.partial(_in_proj_kernel, rows_per_step=128, first_tile=True),
        grid=(1, nj),
        in_specs=ins,
        out_specs=outs + [pl.BlockSpec((d, tn), lambda i, j: (0, j))],
        out_shape=out_shape + [jax.ShapeDtypeStruct((d, width), BF16)],
        scratch_shapes=scratch,
        compiler_params=_params("arbitrary", "arbitrary"),
        name="in_proj_first",
    )(x2d, gain, w_f32, cscale)

    ins, outs = specs(1, {})
    any_spec = pl.BlockSpec(memory_space=pl.ANY)
    return pl.pallas_call(
        functools.partial(_in_proj_kernel, rows_per_step=128, first_tile=False),
        grid=(n // tm - 1, nj),
        in_specs=ins + [any_spec, any_spec],
        out_specs=outs,
        out_shape=out_shape,
        input_output_aliases={4: 0, 5: 1},
        scratch_shapes=scratch,
        compiler_params=_params("arbitrary", "arbitrary"),
        name="in_proj",
    )(x2d, gain, w16, cscale, proj, fb)


def _moba_kernel(rb_ref, qall_ref, q_ref, k_ref, v_ref, o_ref, bias_scr, vt_scr, sel_scr, s_scr, *, n_blocks, group,
                 q_per_step):
    hg = pl.program_id(0)
    b = pl.program_id(1)
    t = pl.program_id(2)
    blk = MOBA_BLOCK
    dh = HEAD_DIM
    sub = 8
    stride = n_blocks // q_per_step

    @pl.when((b == 0) & (t == 0))
    def _build_bias():
        r = lax.broadcasted_iota(jnp.int32, (blk, blk), 1)
        c = lax.broadcasted_iota(jnp.int32, (blk, blk), 0)
        for g in range(group):
            hd = hg * group + g
            for d in range(2):
                rel = d * blk + r - c
                bias = jnp.full((blk, blk), rb_ref[REL_BUCKETS - 1, hd] * LOG2E, F32)
                for kk in range(REL_BUCKETS - 2, -1, -1):
                    bias = jnp.where(rel < BUCKET_LO[kk + 1], rb_ref[kk, hd] * LOG2E, bias)
                if d == 0:
                    bias = jnp.where(rel < 0, MASK_VALUE, bias)
                bias_scr[g, d] = bias

    @pl.when(t == 0)
    def _per_sequence():
        for g in range(group):
            hs = slice(g * dh, (g + 1) * dh)
            means = []
            for jb in range(n_blocks):
                rs = slice(jb * blk, (jb + 1) * blk)
                means.append(jnp.mean(k_ref[0, rs, hs].astype(F32), axis=0, keepdims=True))
                vt_scr[g, :, rs] = v_ref[0, rs, hs].T
            km = jnp.concatenate(means + [jnp.zeros((16 - n_blocks, dh), F32)], axis=0)

            km_hi = km.astype(BF16)
            km_lo = (km - km_hi.astype(F32)).astype(BF16)
            qall = qall_ref[0, :, hs]
            sc = (lax.dot_general(km_hi, qall, NT_DIMS, preferred_element_type=F32)
                  + lax.dot_general(km_lo, qall, NT_DIMS, preferred_element_type=F32))
            rows = lax.broadcasted_iota(jnp.int32, sc.shape, 0)
            rows_f = rows.astype(F32)
            past = rows < lax.broadcasted_iota(jnp.int32, sc.shape, 1) // blk
            s = jnp.where(past, sc, -jnp.inf)
            picked = jnp.zeros(sc.shape, jnp.bool_)
            for _ in range(MOBA_TOPK):
                top = jnp.max(s, axis=0, keepdims=True)
                first = jnp.min(jnp.where(s == top, rows_f, 1e9), axis=0, keepdims=True)
                hit = rows_f == first
                picked = picked | hit
                s = jnp.where(hit, -jnp.inf, s)
            mask = jnp.where(picked & past, 0.0, MASK_VALUE)
            for qb in range(n_blocks):
                sel_scr[g, qb] = mask[:, qb * blk:(qb + 1) * blk]

    hss = [slice(g * dh, (g + 1) * dh) for g in range(group)]

    def logits_pass(slot, qb, row0, g):
        nt = qb + 1
        far = rb_ref[REL_BUCKETS - 1, hg * group + g] * LOG2E
        s = lax.dot_general(k_ref[0, :nt * blk, hss[g]], q_ref[0, slot, 0, :, hss[g]], NT_DIMS,
                            preferred_element_type=F32)
        m8 = None
        for j in range(nt):
            rs = slice(j * blk, (j + 1) * blk)
            d = nt - 1 - j
            if d == 0:
                sj = s[rs] + bias_scr[g, 0]
            elif d == 1:
                sj = s[rs] + bias_scr[g, 1] + sel_scr[g, qb, j:j + 1, :]
            else:
                sj = s[rs] + (sel_scr[g, qb, j:j + 1, :] + far)
            s_scr[g, row0 + j * blk:row0 + (j + 1) * blk, :] = sj
            mj = jnp.max(sj.reshape(blk // sub, sub, blk), axis=0)
            m8 = mj if m8 is None else jnp.maximum(m8, mj)
        return jnp.max(m8, axis=0, keepdims=True)

    def softmax_pv_pass(slot, qb, row0, g, m):
        l8 = jnp.zeros((sub, blk), F32)
        acc = None
        for j in range(qb + 1):
            rs = slice(j * blk, (j + 1) * blk)
            p = jnp.exp2(s_scr[g, row0 + j * blk:row0 + (j + 1) * blk, :] - m)
            l8 = l8 + jnp.sum(p.reshape(blk // sub, sub, blk), axis=0)
            pv = jnp.dot(vt_scr[g, :, rs], p.astype(BF16), preferred_element_type=F32)
            acc = pv if acc is None else acc + pv
        l = jnp.sum(l8, axis=0, keepdims=True)
        o_ref[0, slot, 0, :, hss[g]] = (acc / l).T.astype(o_ref.dtype)

    def attend_set(tt):
        work, row0 = [], 0
        for slot in range(q_per_step):
            qb = tt + slot * stride
            work.append((slot, qb, row0))
            row0 += (qb + 1) * blk
        ms = {(slot, g): logits_pass(slot, qb, r0, g) for slot, qb, r0 in work for g in range(group)}
        for slot, qb, r0 in work:
            for g in range(group):
                softmax_pv_pass(slot, qb, r0, g, ms[slot, g])

    for tt in range(stride):
        pl.when(t == tt)(functools.partial(attend_set, tt))


def _moba(proj3, rel_bias, *, heads, group=4, q_per_step=4):
    bsz, seq, width = proj3.shape
    nb = seq // MOBA_BLOCK
    stride = nb // q_per_step
    gw = group * HEAD_DIM
    cq, ck, cv = (c * (SEG // gw) for c in (COL_QA, COL_KA, COL_VA))
    proj5 = proj3.reshape(bsz, q_per_step, stride, MOBA_BLOCK, width)
    q_blk = (1, q_per_step, 1, MOBA_BLOCK, gw)
    s_rows = sum(nb - k * stride for k in range(q_per_step)) * MOBA_BLOCK
    out = pl.pallas_call(
        functools.partial(_moba_kernel, n_blocks=nb, group=group, q_per_step=q_per_step),
        grid=(heads // group, bsz, stride),
        in_specs=[
            pl.BlockSpec(memory_space=pltpu.SMEM),
            pl.BlockSpec((1, seq, gw), lambda h, b, i: (b, 0, cq + h)),
            pl.BlockSpec(q_blk, lambda h, b, i: (b, 0, i, 0, cq + h)),
            pl.BlockSpec((1, seq, gw), lambda h, b, i: (b, 0, ck + h)),
            pl.BlockSpec((1, seq, gw), lambda h, b, i: (b, 0, cv + h)),
        ],
        out_specs=pl.BlockSpec(q_blk, lambda h, b, i: (b, 0, i, 0, h)),
        out_shape=jax.ShapeDtypeStruct((bsz, q_per_step, stride, MOBA_BLOCK, heads * HEAD_DIM), BF16),
        scratch_shapes=[
            pltpu.VMEM((group, 2, MOBA_BLOCK, MOBA_BLOCK), F32),
            pltpu.VMEM((group, HEAD_DIM, seq), BF16),
            pltpu.VMEM((group, nb, 16, MOBA_BLOCK), F32),
            pltpu.VMEM((group, s_rows, MOBA_BLOCK), F32),
        ],
        compiler_params=_params("arbitrary", "arbitrary", "arbitrary"),
        name="moba",
    )(rel_bias, proj3, proj5, proj3, proj3)
    return out.reshape(bsz, seq, heads * HEAD_DIM)


TN_DIMS = (((0,), (0,)), ((), ()))


def _cumsum_rows(x, period):
    pos = lax.broadcasted_iota(jnp.int32, x.shape, 0) % period
    sh = 1
    while sh < period:
        x = x + jnp.where(pos >= sh, pltpu.roll(x, sh, axis=0), 0.0)
        sh *= 2
    return x


def _hgrn_kernel(lbl_ref, gain_ref, q_ref, f_ref, i_ref, g_ref, *rest, heads, layer, chunks, n_cast):
    w32_refs = rest[:n_cast]
    o_ref = rest[n_cast]
    w16_refs = rest[n_cast + 1:2 * n_cast + 1]
    st_scr, = rest[2 * n_cast + 1:]
    _cast_slabs(w32_refs, w16_refs)
    ch = HGRN_CHUNK
    dh = HEAD_DIM
    rows = chunks * ch

    @pl.when(pl.program_id(1) == 0)
    def _():
        st_scr[...] = jnp.zeros_like(st_scr)

    lg = lbl_ref[...]
    e = jnp.exp(lg - jnp.max(lg, axis=0, keepdims=True))
    lb = jnp.sum(e[:layer + 1], axis=0, keepdims=True) / jnp.sum(e, axis=0, keepdims=True)

    half = 0.5 * (1.0 - lb)
    f = (lb + half) + half * jnp.tanh(0.5 * f_ref[0])
    bcum = _cumsum_rows(jnp.log(f), ch)
    q_dec = (q_ref[0].astype(F32) * jnp.exp(bcum)).astype(BF16)
    k_dec32 = (1.0 - f) * jnp.exp(-bcum)
    k_dec = k_dec32.astype(BF16)
    decays = [jnp.exp(bcum[(c + 1) * ch - 1:(c + 1) * ch, :]) for c in range(chunks)]
    k_end = [(k_dec32[c * ch:(c + 1) * ch, :] * decays[c]).astype(BF16) for c in range(chunks)]
    v = i_ref[0]
    hg = 0.5 * g_ref[0].astype(F32)
    gate_gain = (hg + hg * jnp.tanh(hg)) * gain_ref[...]

    r = lax.broadcasted_iota(jnp.int32, (rows, rows), 0)
    c_ = lax.broadcasted_iota(jnp.int32, (rows, rows), 1)
    causal_in_chunk = (r >= c_) & (r // ch == c_ // ch)

    sls = [slice(hd * dh, (hd + 1) * dh) for hd in range(heads)]
    a_all = [jnp.where(causal_in_chunk,
                       lax.dot_general(q_dec[:, sl], k_dec[:, sl], NT_DIMS, preferred_element_type=F32), 0.0).astype(BF16)
             for sl in sls]
    o_all = [jnp.dot(a, v[:, sl], preferred_element_type=F32) for a, sl in zip(a_all, sls)]
    sts = [st_scr[hd] for hd in range(heads)]
    outs = [[] for _ in range(heads)]
    for c in range(chunks):
        rs = slice(c * ch, (c + 1) * ch)
        for hd, sl in enumerate(sls):
            st = sts[hd]
            outs[hd].append(o_all[hd][rs] + lax.dot_general(q_dec[rs, sl], st.astype(BF16), NT_DIMS,
                                                            preferred_element_type=F32))
            sts[hd] = st * decays[c][:, sl] + lax.dot_general(v[rs, sl], k_end[c][:, sl], TN_DIMS,
                                                              preferred_element_type=F32)
    for hd, sl in enumerate(sls):
        st_scr[hd] = sts[hd]
        o = jnp.concatenate(outs[hd], axis=0)
        ms = jnp.mean(o * o, axis=-1, keepdims=True)
        o_ref[0, :, sl] = (o * lax.rsqrt(ms + NORM_EPS) * gate_gain[:, sl]).astype(o_ref.dtype)


def _hgrn(proj3, fb3, lb_logits, gain, cast_weights, *, heads, layer, chunks=4):
    bsz, seq, _ = proj3.shape
    width = heads * HEAD_DIM
    rows = chunks * HGRN_CHUNK
    blk = (1, rows, width)
    nc = seq // rows
    cast_specs = _cast_specs(cast_weights, bsz * nc, lambda b, c: b * nc + c)
    outs = pl.pallas_call(
        functools.partial(_hgrn_kernel, heads=heads, layer=layer, chunks=chunks, n_cast=len(cast_weights)),
        grid=(bsz, nc),
        in_specs=[
            pl.BlockSpec(lb_logits.shape, lambda b, c: (0, 0)),
            pl.BlockSpec((1, width), lambda b, c: (0, 0)),
            pl.BlockSpec(blk, lambda b, c: (b, c, COL_QB)),
            pl.BlockSpec(blk, lambda b, c: (b, c, 0)),
            pl.BlockSpec(blk, lambda b, c: (b, c, COL_IB)),
            pl.BlockSpec(blk, lambda b, c: (b, c, COL_GB)),
            *cast_specs,
        ],
        out_specs=[pl.BlockSpec(blk, lambda b, c: (b, c, 0)), *cast_specs],
        out_shape=[jax.ShapeDtypeStruct((bsz, seq, width), BF16),
                   *(jax.ShapeDtypeStruct(w.shape, BF16) for w in cast_weights)],
        scratch_shapes=[pltpu.VMEM((heads, HEAD_DIM, HEAD_DIM), F32)],
        compiler_params=_params("arbitrary", "arbitrary"),
        name="hgrn2",
    )(lb_logits, gain, proj3, fb3, proj3, proj3, *cast_weights)
    return outs[0], outs[1:]


def _mix_kernel(ya_ref, yb_ref, ga0_ref, ga1_ref, gb0_ref, gb1_ref, x_ref, wa_ref, wb_ref, wo_ref, gain_ref,
                *rest, tc, n_cast):
    w32_refs = rest[:n_cast]
    x1_ref, h2_ref = rest[n_cast:n_cast + 2]
    w16_refs = rest[n_cast + 2:2 * n_cast + 2]
    mixed_scr, = rest[2 * n_cast + 2:]
    _cast_slabs(w32_refs, w16_refs)
    d = x_ref.shape[1]
    gate_refs = ((ga0_ref, gb0_ref), (ga1_ref, gb1_ref))
    ya = ya_ref[...]
    yb = yb_ref[...]
    for cb in range(d // tc):
        cs = slice(cb * tc, (cb + 1) * tc)
        ga_ref, gb_ref = gate_refs[(cb * tc) // SEG]
        gs = slice((cb * tc) % SEG, (cb * tc) % SEG + tc)
        pa = jnp.dot(ya, wa_ref[:, cs], preferred_element_type=F32)
        pb = jnp.dot(yb, wb_ref[:, cs], preferred_element_type=F32)
        mixed = (jax.nn.sigmoid(ga_ref[:, gs].astype(F32)) * pa
                 + jax.nn.sigmoid(gb_ref[:, gs].astype(F32)) * pb)
        mixed_scr[:, cs] = mixed.astype(BF16)
    ssq = jnp.zeros((x_ref.shape[0], 1), F32)
    for cb in range(d // tc):
        cs = slice(cb * tc, (cb + 1) * tc)
        x1 = x_ref[:, cs] + jnp.dot(mixed_scr[...], wo_ref[:, cs], preferred_element_type=F32)
        x1_ref[:, cs] = x1
        ssq = ssq + jnp.sum(x1 * x1, axis=-1, keepdims=True)
    inv = lax.rsqrt(ssq * (1.0 / d) + NORM_EPS)
    for cb in range(d // tc):
        cs = slice(cb * tc, (cb + 1) * tc)
        h2_ref[:, cs] = (x1_ref[:, cs] * inv * gain_ref[:, cs]).astype(BF16)


def _mix(ya, yb, proj, x2d, wa, wb, wo, gain, cast_weights, *, tm=512, tc=512):
    n, d = x2d.shape
    row = lambda i: (i, 0)
    seg = lambda c: pl.BlockSpec((tm, SEG), lambda i: (i, c))
    cast_specs = _cast_specs(cast_weights, n // tm, lambda i: i)
    outs = pl.pallas_call(
        functools.partial(_mix_kernel, tc=tc, n_cast=len(cast_weights)),
        grid=(n // tm,),
        in_specs=[
            pl.BlockSpec((tm, A_WIDTH), row),
            pl.BlockSpec((tm, B_WIDTH), row),
            seg(COL_GATE_A), seg(COL_GATE_A + 1), seg(COL_GATE_B), seg(COL_GATE_B + 1),
            pl.BlockSpec((tm, d), row),
            _resident(wa.shape), _resident(wb.shape), _resident(wo.shape), _resident(gain.shape),
            *cast_specs,
        ],
        out_specs=[pl.BlockSpec((tm, d), row), pl.BlockSpec((tm, d), row), *cast_specs],
        out_shape=[jax.ShapeDtypeStruct((n, d), F32), jax.ShapeDtypeStruct((n, d), BF16),
                   *(jax.ShapeDtypeStruct(w.shape, BF16) for w in cast_weights)],
        scratch_shapes=[pltpu.VMEM((tm, d), BF16)],
        compiler_params=_params("arbitrary"),
        name="mix",
    )(ya, yb, proj, proj, proj, proj, x2d, wa, wb, wo, gain, *cast_weights)
    return outs[0], outs[1], outs[2:]


def _ffn_kernel(h_ref, wg_ref, wu_ref, wd_ref, o_ref):
    @pl.when(pl.program_id(1) == 0)
    def _():
        o_ref[...] = jnp.zeros_like(o_ref)

    h = h_ref[...]
    g = jnp.dot(h, wg_ref[...], preferred_element_type=F32)
    u = jnp.dot(h, wu_ref[...], preferred_element_type=F32)
    act = (g * jax.nn.sigmoid(g) * u).astype(BF16)
    o_ref[...] += jnp.dot(act, wd_ref[...], preferred_element_type=F32)


def _ffn(h2, w_gu, w_down, *, tm=1024, tf=512):
    n, d = h2.shape
    dff = w_down.shape[0]
    nj = dff // tf
    return pl.pallas_call(
        _ffn_kernel,
        grid=(n // tm, nj),
        in_specs=[
            pl.BlockSpec((tm, d), lambda i, j: (i, 0)),
            pl.BlockSpec((d, tf), lambda i, j: (0, j)),
            pl.BlockSpec((d, tf), lambda i, j: (0, j + nj)),
            pl.BlockSpec((tf, d), lambda i, j: (j, 0)),
        ],
        out_specs=pl.BlockSpec((tm, d), lambda i, j: (i, 0)),
        out_shape=jax.ShapeDtypeStruct((n, d), F32),
        compiler_params=_params("arbitrary", "arbitrary"),
        name="ffn",
    )(h2, w_gu, w_gu, w_down)


def _ple_kernel(x1_ref, y_ref, p_ref, wg_ref, wp_ref, gain_ref, o_ref, x2_scr, *, tc, final_norm):
    d = x1_ref.shape[1]
    x2_scr[...] = x1_ref[...] + y_ref[...]
    xb = x2_scr[...].astype(BF16)
    pb = p_ref[...].astype(BF16)
    ssq = jnp.zeros((x1_ref.shape[0], 1), F32)
    for cb in range(d // tc):
        cs = slice(cb * tc, (cb + 1) * tc)
        gate = jax.nn.sigmoid(jnp.dot(xb, wg_ref[:, cs], preferred_element_type=F32))
        pe = jnp.dot(pb, wp_ref[:, cs], preferred_element_type=F32)
        x3 = x2_scr[:, cs] + gate * pe
        o_ref[:, cs] = x3
        ssq = ssq + jnp.sum(x3 * x3, axis=-1, keepdims=True)
    if not final_norm:
        return
    inv = lax.rsqrt(ssq * (1.0 / d) + NORM_EPS)
    for cb in range(d // tc):
        cs = slice(cb * tc, (cb + 1) * tc)
        o_ref[:, cs] = o_ref[:, cs] * inv * gain_ref[:, cs]


def _ple(x1, y, p2d, wg, wp, gain, *, final_norm, tm=512, tc=512):
    n, d = x1.shape
    row = lambda i: (i, 0)
    return pl.pallas_call(
        functools.partial(_ple_kernel, tc=tc, final_norm=final_norm),
        grid=(n // tm,),
        in_specs=[
            pl.BlockSpec((tm, d), row),
            pl.BlockSpec((tm, d), row),
            pl.BlockSpec((tm, p2d.shape[1]), row),
            _resident(wg.shape), _resident(wp.shape), _resident(gain.shape),
        ],
        out_specs=pl.BlockSpec((tm, d), row),
        out_shape=jax.ShapeDtypeStruct((n, d), F32),
        scratch_shapes=[pltpu.VMEM((tm, d), F32)],
        compiler_params=_params("arbitrary"),
        name="ple",
    )(x1, y, p2d, wg, wp, gain)


def kernel(x, p, norm_mix, w_in, hgrn_norm, w_proj_a, w_proj_b, w_out, norm_ffn, w_gate_up, w_down, w_ple,
           w_ple_gate, rel_bias, hgrn_lb_logits, norm_final):
    bsz, seq, d = x.shape
    n = bsz * seq
    depth = w_in.shape[0]
    a_heads = A_WIDTH // HEAD_DIM
    b_heads = B_WIDTH // HEAD_DIM

    cscale = np.ones((1, IN_WIDTH), np.float32)
    cscale[:, COL_QA * SEG:(COL_QA + 1) * SEG] = LOG2E / math.sqrt(HEAD_DIM)
    cscale = jnp.asarray(cscale)

    xc = x.reshape(n, d)
    for i in range(depth):
        proj, fb = _in_proj(xc, norm_mix[i][None], w_in[i], cscale)
        proj3 = proj.reshape(bsz, seq, IN_WIDTH)
        ya = _moba(proj3, rel_bias, heads=a_heads)
        yb, (wa, wb, wo, wgu, wdn) = _hgrn(proj3, fb.reshape(bsz, seq, SEG), hgrn_lb_logits, hgrn_norm[i][None],
                                           (w_proj_a[i], w_proj_b[i], w_out[i], w_gate_up[i], w_down[i]),
                                           heads=b_heads, layer=i)
        x1, h2, (wpg, wpl) = _mix(ya.reshape(n, A_WIDTH), yb.reshape(n, B_WIDTH), proj, xc, wa, wb, wo,
                                  norm_ffn[i][None], (w_ple_gate[i], w_ple[i]))
        y = _ffn(h2, wgu, wdn)
        xc = _ple(x1, y, p[i].reshape(n, -1), wpg, wpl, norm_final[None], final_norm=(i == depth - 1))
    return xc.reshape(bsz, seq, d)
```

```python
import functools
import math

import jax
import jax.numpy as jnp
import numpy as np
from jax import lax
from jax.experimental import pallas as pl
from jax.experimental.pallas import tpu as pltpu

F32 = jnp.float32
BF16 = jnp.bfloat16

D_MODEL = 2048
HEAD_DIM = 128
A_WIDTH = 1024
B_WIDTH = 1024
MOBA_BLOCK = 256
MOBA_TOPK = 3
REL_BUCKETS = 32
REL_MAX_EXACT = 16
REL_MAX_DIST = 128
HGRN_CHUNK = 64
D_FF = 5632
NORM_EPS = 1e-6

COL_QA, COL_KA, COL_VA, COL_QB, COL_FB, COL_IB, COL_GB, COL_GATE_A, COL_GATE_B = 0, 1, 2, 3, 4, 5, 6, 7, 9
IN_WIDTH = 11 * 1024
SEG = 1024

BF16_SUBLANES = 16
MASK_VALUE = -1e30
VMEM_LIMIT = 56 * 1024 * 1024

NT_DIMS = (((1,), (1,)), ((), ()))
LOG2E = math.log2(math.e)


def _bucket_thresholds():
    n = np.arange(0, 4096, dtype=np.int64)
    nf = np.maximum(n, REL_MAX_EXACT).astype(np.float64)
    large = REL_MAX_EXACT + (np.log(nf / REL_MAX_EXACT) / math.log(REL_MAX_DIST / REL_MAX_EXACT)
                             * (REL_BUCKETS - REL_MAX_EXACT)).astype(np.int64)
    large = np.minimum(large, REL_BUCKETS - 1)
    bucket = np.where(n < REL_MAX_EXACT, n, large)
    return [int(np.argmax(bucket >= k)) for k in range(REL_BUCKETS)]


BUCKET_LO = _bucket_thresholds()


def _params(*sem):
    return pltpu.CompilerParams(dimension_semantics=sem, vmem_limit_bytes=VMEM_LIMIT)


def _resident(shape):
    return pl.BlockSpec(shape, lambda *_: (0,) * len(shape), pipeline_mode=pl.Buffered(1))


def _cast_specs(weights, n_steps, step_index):
    specs = []
    for w in weights:
        r = BF16_SUBLANES
        while w.shape[0] % r or w.shape[0] // r > n_steps:
            r += BF16_SUBLANES
        last = w.shape[0] // r - 1
        specs.append(pl.BlockSpec((r, w.shape[1]), lambda *ids, last=last: (jnp.minimum(step_index(*ids), last), 0)))
    return specs


def _cast_slabs(w32_refs, w16_refs):
    for w32, w16 in zip(w32_refs, w16_refs):
        w16[...] = w32[...].astype(BF16)


def _in_proj_kernel(x_ref, gain_ref, w_ref, cscale_ref, *rest, rows_per_step, first_tile):
    if first_tile:
        o_ref, fb_ref, w16_ref, h_scr = rest
    else:
        _, _, o_ref, fb_ref, h_scr = rest
    j = pl.program_id(1)

    @pl.when(j == 0)
    def _():
        for r in range(x_ref.shape[0] // rows_per_step):
            rs = slice(r * rows_per_step, (r + 1) * rows_per_step)
            x = x_ref[rs, :]
            ms = jnp.mean(x * x, axis=-1, keepdims=True)
            h_scr[rs, :] = (x * lax.rsqrt(ms + NORM_EPS) * gain_ref[...]).astype(BF16)

    if first_tile:
        w = w_ref[...].astype(BF16)
        w16_ref[...] = w
    else:
        w = w_ref[...]
    acc = jnp.dot(h_scr[...], w, preferred_element_type=F32)
    o_ref[...] = (acc * cscale_ref[...]).astype(BF16)
    pltpu.store(fb_ref, acc, mask=jnp.broadcast_to(j == COL_FB, acc.shape))


def _in_proj(x2d, gain, w_f32, cscale, *, tm=1024, tn=SEG):
    n, d = x2d.shape
    width = w_f32.shape[1]
    nj = width // tn
    out_shape = [jax.ShapeDtypeStruct((n, width), BF16), jax.ShapeDtypeStruct((n, SEG), F32)]
    scratch = [pltpu.VMEM((tm, d), BF16)]

    def specs(row0, x_mode):
        ins = [
            pl.BlockSpec((tm, d), lambda i, j: (i + row0, 0), **x_mode),
            pl.BlockSpec((1, d), lambda i, j: (0, 0)),
            pl.BlockSpec((d, tn), lambda i, j: (0, j)),
            pl.BlockSpec((1, tn), lambda i, j: (0, j)),
        ]
        outs = [
            pl.BlockSpec((tm, tn), lambda i, j: (i + row0, j)),
            pl.BlockSpec((tm, SEG), lambda i, j: (i + row0, 0)),
        ]
        return ins, outs

    ins, outs = specs(0, dict(pipeline_mode=pl.Buffered(1)))
    proj, fb, w16 = pl.pallas_call(
        functools.partial(_in_proj_kernel, rows_per_step=128, first_tile=True),
        grid=(1, nj),
        in_specs=ins,
        out_specs=outs + [pl.BlockSpec((d, tn), lambda i, j: (0, j))],
        out_shape=out_shape + [jax.ShapeDtypeStruct((d, width), BF16)],
        scratch_shapes=scratch,
        compiler_params=_params("arbitrary", "arbitrary"),
        name="in_proj_first",
    )(x2d, gain, w_f32, cscale)

    ins, outs = specs(1, {})
    any_spec = pl.BlockSpec(memory_space=pl.ANY)
    return pl.pallas_call(
        functools.partial(_in_proj_kernel, rows_per_step=128, first_tile=False),
        grid=(n // tm - 1, nj),
        in_specs=ins + [any_spec, any_spec],
        out_specs=outs,
        out_shape=out_shape,
        input_output_aliases={4: 0, 5: 1},
        scratch_shapes=scratch,
        compiler_params=_params("arbitrary", "arbitrary"),
        name="in_proj",
    )(x2d, gain, w16, cscale, proj, fb)


def _moba_kernel(rb_ref, qall_ref, q_ref, k_ref, v_ref, o_ref, bias_scr, vt_scr, sel_scr, s_scr, *, n_blocks, group,
                 q_per_step):
    hg = pl.program_id(0)
    b = pl.program_id(1)
    t = pl.program_id(2)
    blk = MOBA_BLOCK
    dh = HEAD_DIM
    sub = 8
    stride = n_blocks // q_per_step

    @pl.when((b == 0) & (t == 0))
    def _build_bias():
        r = lax.broadcasted_iota(jnp.int32, (blk, blk), 1)
        c = lax.broadcasted_iota(jnp.int32, (blk, blk), 0)
        for g in range(group):
            hd = hg * group + g
            for d in range(2):
                rel = d * blk + r - c
                bias = jnp.full((blk, blk), rb_ref[REL_BUCKETS - 1, hd] * LOG2E, F32)
                for kk in range(REL_BUCKETS - 2, -1, -1):
                    bias = jnp.where(rel < BUCKET_LO[kk + 1], rb_ref[kk, hd] * LOG2E, bias)
                if d == 0:
                    bias = jnp.where(rel < 0, MASK_VALUE, bias)
                bias_scr[g, d] = bias

    @pl.when(t == 0)
    def _per_sequence():
        for g in range(group):
            hs = slice(g * dh, (g + 1) * dh)
            means = []
            for jb in range(n_blocks):
                rs = slice(jb * blk, (jb + 1) * blk)
                means.append(jnp.mean(k_ref[0, rs, hs].astype(F32), axis=0, keepdims=True))
                vt_scr[g, :, rs] = v_ref[0, rs, hs].T
            km = jnp.concatenate(means + [jnp.zeros((16 - n_blocks, dh), F32)], axis=0)

            km_hi = km.astype(BF16)
            km_lo = (km - km_hi.astype(F32)).astype(BF16)
            qall = qall_ref[0, :, hs]
            sc = (lax.dot_general(km_hi, qall, NT_DIMS, preferred_element_type=F32)
                  + lax.dot_general(km_lo, qall, NT_DIMS, preferred_element_type=F32))
            rows = lax.broadcasted_iota(jnp.int32, sc.shape, 0)
            rows_f = rows.astype(F32)
            past = rows < lax.broadcasted_iota(jnp.int32, sc.shape, 1) // blk
            s = jnp.where(past, sc, -jnp.inf)
            picked = jnp.zeros(sc.shape, jnp.bool_)
            for _ in range(MOBA_TOPK):
                top = jnp.max(s, axis=0, keepdims=True)
                first = jnp.min(jnp.where(s == top, rows_f, 1e9), axis=0, keepdims=True)
                hit = rows_f == first
                picked = picked | hit
                s = jnp.where(hit, -jnp.inf, s)
            mask = jnp.where(picked & past, 0.0, MASK_VALUE)
            for qb in range(n_blocks):
                sel_scr[g, qb] = mask[:, qb * blk:(qb + 1) * blk]

    hss = [slice(g * dh, (g + 1) * dh) for g in range(group)]

    def logits_pass(slot, qb, row0, g):
        nt = qb + 1
        far = rb_ref[REL_BUCKETS - 1, hg * group + g] * LOG2E
        s = lax.dot_general(k_ref[0, :nt * blk, hss[g]], q_ref[0, slot, 0, :, hss[g]], NT_DIMS,
                            preferred_element_type=F32)
        m8 = None
        for j in range(nt):
            rs = slice(j * blk, (j + 1) * blk)
            d = nt - 1 - j
            if d == 0:
                sj = s[rs] + bias_scr[g, 0]
            elif d == 1:
                sj = s[rs] + bias_scr[g, 1] + sel_scr[g, qb, j:j + 1, :]
            else:
                sj = s[rs] + (sel_scr[g, qb, j:j + 1, :] + far)
            s_scr[g, row0 + j * blk:row0 + (j + 1) * blk, :] = sj
            mj = jnp.max(sj.reshape(blk // sub, sub, blk), axis=0)
            m8 = mj if m8 is None else jnp.maximum(m8, mj)
        return jnp.max(m8, axis=0, keepdims=True)

    def softmax_pv_pass(slot, qb, row0, g, m):
        l8 = jnp.zeros((sub, blk), F32)
        acc = None
        for j in range(qb + 1):
            rs = slice(j * blk, (j + 1) * blk)
            p = jnp.exp2(s_scr[g, row0 + j * blk:row0 + (j + 1) * blk, :] - m)
            l8 = l8 + jnp.sum(p.reshape(blk // sub, sub, blk), axis=0)
            pv = jnp.dot(vt_scr[g, :, rs], p.astype(BF16), preferred_element_type=F32)
            acc = pv if acc is None else acc + pv
        l = jnp.sum(l8, axis=0, keepdims=True)
        o_ref[0, slot, 0, :, hss[g]] = (acc / l).T.astype(o_ref.dtype)

    def attend_set(tt):
        work, row0 = [], 0
        for slot in range(q_per_step):
            qb = tt + slot * stride
            work.append((slot, qb, row0))
            row0 += (qb + 1) * blk
        ms = {(slot, g): logits_pass(slot, qb, r0, g) for g in range(group) for slot, qb, r0 in work}
        for g in range(group):
            for slot, qb, r0 in work:
                softmax_pv_pass(slot, qb, r0, g, ms[slot, g])

    for tt in range(stride):
        pl.when(t == tt)(functools.partial(attend_set, tt))


def _moba(proj3, rel_bias, *, heads, group=4, q_per_step=4):
    bsz, seq, width = proj3.shape
    nb = seq // MOBA_BLOCK
    stride = nb // q_per_step
    gw = group * HEAD_DIM
    cq, ck, cv = (c * (SEG // gw) for c in (COL_QA, COL_KA, COL_VA))
    proj5 = proj3.reshape(bsz, q_per_step, stride, MOBA_BLOCK, width)
    q_blk = (1, q_per_step, 1, MOBA_BLOCK, gw)
    s_rows = sum(nb - k * stride for k in range(q_per_step)) * MOBA_BLOCK
    out = pl.pallas_call(
        functools.partial(_moba_kernel, n_blocks=nb, group=group, q_per_step=q_per_step),
        grid=(heads // group, bsz, stride),
        in_specs=[
            pl.BlockSpec(memory_space=pltpu.SMEM),
            pl.BlockSpec((1, seq, gw), lambda h, b, i: (b, 0, cq + h)),
            pl.BlockSpec(q_blk, lambda h, b, i: (b, 0, i, 0, cq + h)),
            pl.BlockSpec((1, seq, gw), lambda h, b, i: (b, 0, ck + h)),
            pl.BlockSpec((1, seq, gw), lambda h, b, i: (b, 0, cv + h)),
        ],
        out_specs=pl.BlockSpec(q_blk, lambda h, b, i: (b, 0, i, 0, h)),
        out_shape=jax.ShapeDtypeStruct((bsz, q_per_step, stride, MOBA_BLOCK, heads * HEAD_DIM), BF16),
        scratch_shapes=[
            pltpu.VMEM((group, 2, MOBA_BLOCK, MOBA_BLOCK), F32),
            pltpu.VMEM((group, HEAD_DIM, seq), BF16),
            pltpu.VMEM((group, nb, 16, MOBA_BLOCK), F32),
            pltpu.VMEM((group, s_rows, MOBA_BLOCK), F32),
        ],
        compiler_params=_params("arbitrary", "arbitrary", "arbitrary"),
        name="moba",
    )(rel_bias, proj3, proj5, proj3, proj3)
    return out.reshape(bsz, seq, heads * HEAD_DIM)


TN_DIMS = (((0,), (0,)), ((), ()))


def _cumsum_rows(x, period):
    pos = lax.broadcasted_iota(jnp.int32, x.shape, 0) % period
    sh = 1
    while sh < period:
        x = x + jnp.where(pos >= sh, pltpu.roll(x, sh, axis=0), 0.0)
        sh *= 2
    return x


def _hgrn_kernel(lbl_ref, gain_ref, q_ref, f_ref, i_ref, g_ref, *rest, heads, layer, chunks, n_cast):
    w32_refs = rest[:n_cast]
    o_ref = rest[n_cast]
    w16_refs = rest[n_cast + 1:2 * n_cast + 1]
    st_scr, = rest[2 * n_cast + 1:]
    _cast_slabs(w32_refs, w16_refs)
    ch = HGRN_CHUNK
    dh = HEAD_DIM
    rows = chunks * ch

    @pl.when(pl.program_id(1) == 0)
    def _():
        st_scr[...] = jnp.zeros_like(st_scr)

    lg = lbl_ref[...]
    e = jnp.exp(lg - jnp.max(lg, axis=0, keepdims=True))
    lb = jnp.sum(e[:layer + 1], axis=0, keepdims=True) / jnp.sum(e, axis=0, keepdims=True)

    half = 0.5 * (1.0 - lb)
    f = (lb + half) + half * jnp.tanh(0.5 * f_ref[0])
    bcum = _cumsum_rows(jnp.log(f), ch)
    q_dec = (q_ref[0].astype(F32) * jnp.exp(bcum)).astype(BF16)
    k_dec32 = (1.0 - f) * jnp.exp(-bcum)
    k_dec = k_dec32.astype(BF16)
    decays = [jnp.exp(bcum[(c + 1) * ch - 1:(c + 1) * ch, :]) for c in range(chunks)]
    k_end = [(k_dec32[c * ch:(c + 1) * ch, :] * decays[c]).astype(BF16) for c in range(chunks)]
    v = i_ref[0]
    hg = 0.5 * g_ref[0].astype(F32)
    gate_gain = (hg + hg * jnp.tanh(hg)) * gain_ref[...]

    r = lax.broadcasted_iota(jnp.int32, (rows, rows), 0)
    c_ = lax.broadcasted_iota(jnp.int32, (rows, rows), 1)
    causal_in_chunk = (r >= c_) & (r // ch == c_ // ch)

    sls = [slice(hd * dh, (hd + 1) * dh) for hd in range(heads)]
    a_all = [jnp.where(causal_in_chunk,
                       lax.dot_general(q_dec[:, sl], k_dec[:, sl], NT_DIMS, preferred_element_type=F32), 0.0).astype(BF16)
             for sl in sls]
    o_all = [jnp.dot(a, v[:, sl], preferred_element_type=F32) for a, sl in zip(a_all, sls)]
    sts = [st_scr[hd] for hd in range(heads)]
    outs = [[] for _ in range(heads)]
    for c in range(chunks):
        rs = slice(c * ch, (c + 1) * ch)
        for hd, sl in enumerate(sls):
            st = sts[hd]
            outs[hd].append(o_all[hd][rs] + lax.dot_general(q_dec[rs, sl], st.astype(BF16), NT_DIMS,
                                                            preferred_element_type=F32))
            sts[hd] = st * decays[c][:, sl] + lax.dot_general(v[rs, sl], k_end[c][:, sl], TN_DIMS,
                                                              preferred_element_type=F32)
    for hd, sl in enumerate(sls):
        st_scr[hd] = sts[hd]
        o = jnp.concatenate(outs[hd], axis=0)
        ms = jnp.mean(o * o, axis=-1, keepdims=True)
        o_ref[0, :, sl] = (o * lax.rsqrt(ms + NORM_EPS) * gate_gain[:, sl]).astype(o_ref.dtype)


def _hgrn(proj3, fb3, lb_logits, gain, cast_weights, *, heads, layer, chunks=8):
    bsz, seq, _ = proj3.shape
    width = heads * HEAD_DIM
    rows = chunks * HGRN_CHUNK
    blk = (1, rows, width)
    nc = seq // rows
    cast_specs = _cast_specs(cast_weights, bsz * nc, lambda b, c: b * nc + c)
    outs = pl.pallas_call(
        functools.partial(_hgrn_kernel, heads=heads, layer=layer, chunks=chunks, n_cast=len(cast_weights)),
        grid=(bsz, nc),
        in_specs=[
            pl.BlockSpec(lb_logits.shape, lambda b, c: (0, 0)),
            pl.BlockSpec((1, width), lambda b, c: (0, 0)),
            pl.BlockSpec(blk, lambda b, c: (b, c, COL_QB)),
            pl.BlockSpec(blk, lambda b, c: (b, c, 0)),
            pl.BlockSpec(blk, lambda b, c: (b, c, COL_IB)),
            pl.BlockSpec(blk, lambda b, c: (b, c, COL_GB)),
            *cast_specs,
        ],
        out_specs=[pl.BlockSpec(blk, lambda b, c: (b, c, 0)), *cast_specs],
        out_shape=[jax.ShapeDtypeStruct((bsz, seq, width), BF16),
                   *(jax.ShapeDtypeStruct(w.shape, BF16) for w in cast_weights)],
        scratch_shapes=[pltpu.VMEM((heads, HEAD_DIM, HEAD_DIM), F32)],
        compiler_params=_params("arbitrary", "arbitrary"),
        name="hgrn2",
    )(lb_logits, gain, proj3, fb3, proj3, proj3, *cast_weights)
    return outs[0], outs[1:]


def _mix_kernel(ya_ref, yb_ref, ga0_ref, ga1_ref, gb0_ref, gb1_ref, x_ref, wa_ref, wb_ref, wo_ref, gain_ref,
                *rest, tc, n_cast):
    w32_refs = rest[:n_cast]
    x1_ref, h2_ref = rest[n_cast:n_cast + 2]
    w16_refs = rest[n_cast + 2:2 * n_cast + 2]
    mixed_scr, = rest[2 * n_cast + 2:]
    _cast_slabs(w32_refs, w16_refs)
    d = x_ref.shape[1]
    gate_refs = ((ga0_ref, gb0_ref), (ga1_ref, gb1_ref))
    ya = ya_ref[...]
    yb = yb_ref[...]
    for cb in range(d // tc):
        cs = slice(cb * tc, (cb + 1) * tc)
        ga_ref, gb_ref = gate_refs[(cb * tc) // SEG]
        gs = slice((cb * tc) % SEG, (cb * tc) % SEG + tc)
        pa = jnp.dot(ya, wa_ref[:, cs], preferred_element_type=F32)
        pb = jnp.dot(yb, wb_ref[:, cs], preferred_element_type=F32)
        mixed = (jax.nn.sigmoid(ga_ref[:, gs].astype(F32)) * pa
                 + jax.nn.sigmoid(gb_ref[:, gs].astype(F32)) * pb)
        mixed_scr[:, cs] = mixed.astype(BF16)
    ssq = jnp.zeros((x_ref.shape[0], 1), F32)
    for cb in range(d // tc):
        cs = slice(cb * tc, (cb + 1) * tc)
        x1 = x_ref[:, cs] + jnp.dot(mixed_scr[...], wo_ref[:, cs], preferred_element_type=F32)
        x1_ref[:, cs] = x1
        ssq = ssq + jnp.sum(x1 * x1, axis=-1, keepdims=True)
    inv = lax.rsqrt(ssq * (1.0 / d) + NORM_EPS)
    for cb in range(d // tc):
        cs = slice(cb * tc, (cb + 1) * tc)
        h2_ref[:, cs] = (x1_ref[:, cs] * inv * gain_ref[:, cs]).astype(BF16)


def _mix(ya, yb, proj, x2d, wa, wb, wo, gain, cast_weights, *, tm=512, tc=512):
    n, d = x2d.shape
    row = lambda i: (i, 0)
    seg = lambda c: pl.BlockSpec((tm, SEG), lambda i: (i, c))
    cast_specs = _cast_specs(cast_weights, n // tm, lambda i: i)
    outs = pl.pallas_call(
        functools.partial(_mix_kernel, tc=tc, n_cast=len(cast_weights)),
        grid=(n // tm,),
        in_specs=[
            pl.BlockSpec((tm, A_WIDTH), row),
            pl.BlockSpec((tm, B_WIDTH), row),
            seg(COL_GATE_A), seg(COL_GATE_A + 1), seg(COL_GATE_B), seg(COL_GATE_B + 1),
            pl.BlockSpec((tm, d), row),
            _resident(wa.shape), _resident(wb.shape), _resident(wo.shape), _resident(gain.shape),
            *cast_specs,
        ],
        out_specs=[pl.BlockSpec((tm, d), row), pl.BlockSpec((tm, d), row), *cast_specs],
        out_shape=[jax.ShapeDtypeStruct((n, d), F32), jax.ShapeDtypeStruct((n, d), BF16),
                   *(jax.ShapeDtypeStruct(w.shape, BF16) for w in cast_weights)],
        scratch_shapes=[pltpu.VMEM((tm, d), BF16)],
        compiler_params=_params("arbitrary"),
        name="mix",
    )(ya, yb, proj, proj, proj, proj, x2d, wa, wb, wo, gain, *cast_weights)
    return outs[0], outs[1], outs[2:]


def _ffn_kernel(h_ref, wg_ref, wu_ref, wd_ref, o_ref):
    @pl.when(pl.program_id(1) == 0)
    def _():
        o_ref[...] = jnp.zeros_like(o_ref)

    h = h_ref[...]
    g = jnp.dot(h, wg_ref[...], preferred_element_type=F32)
    u = jnp.dot(h, wu_ref[...], preferred_element_type=F32)
    act = (g * jax.nn.sigmoid(g) * u).astype(BF16)
    o_ref[...] += jnp.dot(act, wd_ref[...], preferred_element_type=F32)


def _ffn(h2, w_gu, w_down, *, tm=1024, tf=512):
    n, d = h2.shape
    dff = w_down.shape[0]
    nj = dff // tf
    return pl.pallas_call(
        _ffn_kernel,
        grid=(n // tm, nj),
        in_specs=[
            pl.BlockSpec((tm, d), lambda i, j: (i, 0)),
            pl.BlockSpec((d, tf), lambda i, j: (0, j)),
            pl.BlockSpec((d, tf), lambda i, j: (0, j + nj)),
            pl.BlockSpec((tf, d), lambda i, j: (j, 0)),
        ],
        out_specs=pl.BlockSpec((tm, d), lambda i, j: (i, 0)),
        out_shape=jax.ShapeDtypeStruct((n, d), F32),
        compiler_params=_params("arbitrary", "arbitrary"),
        name="ffn",
    )(h2, w_gu, w_gu, w_down)


def _ple_kernel(x1_ref, y_ref, p_ref, wg_ref, wp_ref, gain_ref, o_ref, x2_scr, *, tc, final_norm):
    d = x1_ref.shape[1]
    x2_scr[...] = x1_ref[...] + y_ref[...]
    xb = x2_scr[...].astype(BF16)
    pb = p_ref[...].astype(BF16)
    ssq = jnp.zeros((x1_ref.shape[0], 1), F32)
    for cb in range(d // tc):
        cs = slice(cb * tc, (cb + 1) * tc)
        gate = jax.nn.sigmoid(jnp.dot(xb, wg_ref[:, cs], preferred_element_type=F32))
        pe = jnp.dot(pb, wp_ref[:, cs], preferred_element_type=F32)
        x3 = x2_scr[:, cs] + gate * pe
        o_ref[:, cs] = x3
        ssq = ssq + jnp.sum(x3 * x3, axis=-1, keepdims=True)
    if not final_norm:
        return
    inv = lax.rsqrt(ssq * (1.0 / d) + NORM_EPS)
    for cb in range(d // tc):
        cs = slice(cb * tc, (cb + 1) * tc)
        o_ref[:, cs] = o_ref[:, cs] * inv * gain_ref[:, cs]


def _ple(x1, y, p2d, wg, wp, gain, *, final_norm, tm=512, tc=512):
    n, d = x1.shape
    row = lambda i: (i, 0)
    return pl.pallas_call(
        functools.partial(_ple_kernel, tc=tc, final_norm=final_norm),
        grid=(n // tm,),
        in_specs=[
            pl.BlockSpec((tm, d), row),
            pl.BlockSpec((tm, d), row),
            pl.BlockSpec((tm, p2d.shape[1]), row),
            _resident(wg.shape), _resident(wp.shape), _resident(gain.shape),
        ],
        out_specs=pl.BlockSpec((tm, d), row),
        out_shape=jax.ShapeDtypeStruct((n, d), F32),
        scratch_shapes=[pltpu.VMEM((tm, d), F32)],
        compiler_params=_params("arbitrary"),
        name="ple",
    )(x1, y, p2d, wg, wp, gain)


def kernel(x, p, norm_mix, w_in, hgrn_norm, w_proj_a, w_proj_b, w_out, norm_ffn, w_gate_up, w_down, w_ple,
           w_ple_gate, rel_bias, hgrn_lb_logits, norm_final):
    bsz, seq, d = x.shape
    n = bsz * seq
    depth = w_in.shape[0]
    a_heads = A_WIDTH // HEAD_DIM
    b_heads = B_WIDTH // HEAD_DIM

    cscale = np.ones((1, IN_WIDTH), np.float32)
    cscale[:, COL_QA * SEG:(COL_QA + 1) * SEG] = LOG2E / math.sqrt(HEAD_DIM)
    cscale = jnp.asarray(cscale)

    xc = x.reshape(n, d)
    for i in range(depth):
        proj, fb = _in_proj(xc, norm_mix[i][None], w_in[i], cscale)
        proj3 = proj.reshape(bsz, seq, IN_WIDTH)
        ya = _moba(proj3, rel_bias, heads=a_heads)
        yb, (wa, wb, wo, wgu, wdn) = _hgrn(proj3, fb.reshape(bsz, seq, SEG), hgrn_lb_logits, hgrn_norm[i][None],
                                           (w_proj_a[i], w_proj_b[i], w_out[i], w_gate_up[i], w_down[i]),
                                           heads=b_heads, layer=i)
        x1, h2, (wpg, wpl) = _mix(ya.reshape(n, A_WIDTH), yb.reshape(n, B_WIDTH), proj, xc, wa, wb, wo,
                                  norm_ffn[i][None], (w_ple_gate[i], w_ple[i]))
        y = _ffn(h2, wgu, wdn)
        xc = _ple(x1, y, p[i].reshape(n, -1), wpg, wpl, norm_final[None], final_norm=(i == depth - 1))
    return xc.reshape(bsz, seq, d)
```

```python
import functools
import math

import jax
import jax.numpy as jnp
import numpy as np
from jax import lax
from jax.experimental import pallas as pl
from jax.experimental.pallas import tpu as pltpu

F32 = jnp.float32
BF16 = jnp.bfloat16

D_MODEL = 2048
HEAD_DIM = 128
A_WIDTH = 1024
B_WIDTH = 1024
MOBA_BLOCK = 256
MOBA_TOPK = 3
REL_BUCKETS = 32
REL_MAX_EXACT = 16
REL_MAX_DIST = 128
HGRN_CHUNK = 64
D_FF = 5632
NORM_EPS = 1e-6

COL_QA, COL_KA, COL_VA, COL_QB, COL_FB, COL_IB, COL_GB, COL_GATE_A, COL_GATE_B = 0, 1, 2, 3, 4, 5, 6, 7, 9
IN_WIDTH = 11 * 1024
SEG = 1024

BF16_SUBLANES = 16
MASK_VALUE = -1e30
VMEM_LIMIT = 56 * 1024 * 1024

NT_DIMS = (((1,), (1,)), ((), ()))
LOG2E = math.log2(math.e)


def _bucket_thresholds():
    n = np.arange(0, 4096, dtype=np.int64)
    nf = np.maximum(n, REL_MAX_EXACT).astype(np.float64)
    large = REL_MAX_EXACT + (np.log(nf / REL_MAX_EXACT) / math.log(REL_MAX_DIST / REL_MAX_EXACT)
                             * (REL_BUCKETS - REL_MAX_EXACT)).astype(np.int64)
    large = np.minimum(large, REL_BUCKETS - 1)
    bucket = np.where(n < REL_MAX_EXACT, n, large)
    return [int(np.argmax(bucket >= k)) for k in range(REL_BUCKETS)]


BUCKET_LO = _bucket_thresholds()


def _params(*sem):
    return pltpu.CompilerParams(dimension_semantics=sem, vmem_limit_bytes=VMEM_LIMIT)


def _resident(shape):
    return pl.BlockSpec(shape, lambda *_: (0,) * len(shape), pipeline_mode=pl.Buffered(1))


def _cast_specs(weights, n_steps, step_index):
    specs = []
    for w in weights:
        r = BF16_SUBLANES
        while w.shape[0] % r or w.shape[0] // r > n_steps:
            r += BF16_SUBLANES
        last = w.shape[0] // r - 1
        specs.append(pl.BlockSpec((r, w.shape[1]), lambda *ids, last=last: (jnp.minimum(step_index(*ids), last), 0)))
    return specs


def _cast_slabs(w32_refs, w16_refs):
    for w32, w16 in zip(w32_refs, w16_refs):
        w16[...] = w32[...].astype(BF16)


def _in_proj_kernel(x_ref, gain_ref, w_ref, cscale_ref, *rest, rows_per_step, first_tile, n_cast=0):
    if first_tile:
        o_ref, fb_ref, w16_ref, h_scr = rest
    else:
        rest = rest[2:]
        o_ref, fb_ref = rest[n_cast:n_cast + 2]
        h_scr, = rest[2 * n_cast + 2:]
        _cast_slabs(rest[:n_cast], rest[n_cast + 2:2 * n_cast + 2])
    j = pl.program_id(1)

    @pl.when(j == 0)
    def _():
        for r in range(x_ref.shape[0] // rows_per_step):
            rs = slice(r * rows_per_step, (r + 1) * rows_per_step)
            x = x_ref[rs, :]
            ms = jnp.mean(x * x, axis=-1, keepdims=True)
            h_scr[rs, :] = (x * lax.rsqrt(ms + NORM_EPS) * gain_ref[...]).astype(BF16)

    if first_tile:
        w = w_ref[...].astype(BF16)
        w16_ref[...] = w
    else:
        w = w_ref[...]
    acc = jnp.dot(h_scr[...], w, preferred_element_type=F32)
    o_ref[...] = (acc * cscale_ref[...]).astype(BF16)
    pltpu.store(fb_ref, acc, mask=jnp.broadcast_to(j == COL_FB, acc.shape))


def _in_proj(x2d, gain, w_f32, cscale, cast_weights, *, tm=1024, tn=SEG):
    n, d = x2d.shape
    width = w_f32.shape[1]
    nj = width // tn
    out_shape = [jax.ShapeDtypeStruct((n, width), BF16), jax.ShapeDtypeStruct((n, SEG), F32)]
    scratch = [pltpu.VMEM((tm, d), BF16)]

    def specs(row0, x_mode):
        ins = [
            pl.BlockSpec((tm, d), lambda i, j: (i + row0, 0), **x_mode),
            pl.BlockSpec((1, d), lambda i, j: (0, 0)),
            pl.BlockSpec((d, tn), lambda i, j: (0, j)),
            pl.BlockSpec((1, tn), lambda i, j: (0, j)),
        ]
        outs = [
            pl.BlockSpec((tm, tn), lambda i, j: (i + row0, j)),
            pl.BlockSpec((tm, SEG), lambda i, j: (i + row0, 0)),
        ]
        return ins, outs

    ins, outs = specs(0, dict(pipeline_mode=pl.Buffered(1)))
    proj, fb, w16 = pl.pallas_call(
        functools.partial(_in_proj_kernel, rows_per_step=128, first_tile=True),
        grid=(1, nj),
        in_specs=ins,
        out_specs=outs + [pl.BlockSpec((d, tn), lambda i, j: (0, j))],
        out_shape=out_shape + [jax.ShapeDtypeStruct((d, width), BF16)],
        scratch_shapes=scratch,
        compiler_params=_params("arbitrary", "arbitrary"),
        name="in_proj_first",
    )(x2d, gain, w_f32, cscale)

    ins, outs = specs(1, {})
    any_spec = pl.BlockSpec(memory_space=pl.ANY)
    cast_specs = _cast_specs(cast_weights, (n // tm - 1) * nj, lambda i, j: i * nj + j)
    res = pl.pallas_call(
        functools.partial(_in_proj_kernel, rows_per_step=128, first_tile=False, n_cast=len(cast_weights)),
        grid=(n // tm - 1, nj),
        in_specs=ins + [any_spec, any_spec] + cast_specs,
        out_specs=outs + cast_specs,
        out_shape=out_shape + [jax.ShapeDtypeStruct(w.shape, BF16) for w in cast_weights],
        input_output_aliases={4: 0, 5: 1},
        scratch_shapes=scratch,
        compiler_params=_params("arbitrary", "arbitrary"),
        name="in_proj",
    )(x2d, gain, w16, cscale, proj, fb, *cast_weights)
    return res[0], res[1], res[2:]


def _moba_kernel(rb_ref, qall_ref, q_ref, k_ref, v_ref, o_ref, bias_scr, vt_scr, sel_scr, s_scr, *, n_blocks, group,
                 q_per_step):
    hg = pl.program_id(0)
    b = pl.program_id(1)
    t = pl.program_id(2)
    blk = MOBA_BLOCK
    dh = HEAD_DIM
    sub = 8
    stride = n_blocks // q_per_step

    @pl.when((b == 0) & (t == 0))
    def _build_bias():
        r = lax.broadcasted_iota(jnp.int32, (blk, blk), 1)
        c = lax.broadcasted_iota(jnp.int32, (blk, blk), 0)
        for g in range(group):
            hd = hg * group + g
            for d in range(2):
                rel = d * blk + r - c
                bias = jnp.full((blk, blk), rb_ref[REL_BUCKETS - 1, hd] * LOG2E, F32)
                for kk in range(REL_BUCKETS - 2, -1, -1):
                    bias = jnp.where(rel < BUCKET_LO[kk + 1], rb_ref[kk, hd] * LOG2E, bias)
                if d == 0:
                    bias = jnp.where(rel < 0, MASK_VALUE, bias)
                bias_scr[g, d] = bias

    @pl.when(t == 0)
    def _per_sequence():
        for g in range(group):
            hs = slice(g * dh, (g + 1) * dh)
            means = []
            for jb in range(n_blocks):
                rs = slice(jb * blk, (jb + 1) * blk)
                means.append(jnp.mean(k_ref[0, rs, hs].astype(F32), axis=0, keepdims=True))
                vt_scr[g, :, rs] = v_ref[0, rs, hs].T
            km = jnp.concatenate(means + [jnp.zeros((16 - n_blocks, dh), F32)], axis=0)

            km_hi = km.astype(BF16)
            km_lo = (km - km_hi.astype(F32)).astype(BF16)
            qall = qall_ref[0, :, hs]
            sc = (lax.dot_general(km_hi, qall, NT_DIMS, preferred_element_type=F32)
                  + lax.dot_general(km_lo, qall, NT_DIMS, preferred_element_type=F32))
            rows = lax.broadcasted_iota(jnp.int32, sc.shape, 0)
            rows_f = rows.astype(F32)
            past = rows < lax.broadcasted_iota(jnp.int32, sc.shape, 1) // blk
            s = jnp.where(past, sc, -jnp.inf)
            picked = jnp.zeros(sc.shape, jnp.bool_)
            for _ in range(MOBA_TOPK):
                top = jnp.max(s, axis=0, keepdims=True)
                first = jnp.min(jnp.where(s == top, rows_f, 1e9), axis=0, keepdims=True)
                hit = rows_f == first
                picked = picked | hit
                s = jnp.where(hit, -jnp.inf, s)
            mask = jnp.where(picked & past, 0.0, MASK_VALUE)
            for qb in range(n_blocks):
                sel_scr[g, qb] = mask[:, qb * blk:(qb + 1) * blk]

    hss = [slice(g * dh, (g + 1) * dh) for g in range(group)]

    def logits_pass(slot, qb, row0, g):
        nt = qb + 1
        far = rb_ref[REL_BUCKETS - 1, hg * group + g] * LOG2E
        s = lax.dot_general(k_ref[0, :nt * blk, hss[g]], q_ref[0, slot, 0, :, hss[g]], NT_DIMS,
                            preferred_element_type=F32)
        m8 = None
        for j in range(nt):
            rs = slice(j * blk, (j + 1) * blk)
            d = nt - 1 - j
            if d == 0:
                sj = s[rs] + bias_scr[g, 0]
            elif d == 1:
                sj = s[rs] + bias_scr[g, 1] + sel_scr[g, qb, j:j + 1, :]
            else:
                sj = s[rs] + (sel_scr[g, qb, j:j + 1, :] + far)
            s_scr[g, row0 + j * blk:row0 + (j + 1) * blk, :] = sj
            mj = jnp.max(sj.reshape(blk // sub, sub, blk), axis=0)
            m8 = mj if m8 is None else jnp.maximum(m8, mj)
        return jnp.max(m8, axis=0, keepdims=True)

    def softmax_pv_pass(slot, qb, row0, g, m):
        l8 = jnp.zeros((sub, blk), F32)
        acc = None
        for j in range(qb + 1):
            rs = slice(j * blk, (j + 1) * blk)
            p = jnp.exp2(s_scr[g, row0 + j * blk:row0 + (j + 1) * blk, :] - m)
            l8 = l8 + jnp.sum(p.reshape(blk // sub, sub, blk), axis=0)
            pv = jnp.dot(vt_scr[g, :, rs], p.astype(BF16), preferred_element_type=F32)
            acc = pv if acc is None else acc + pv
        l = jnp.sum(l8, axis=0, keepdims=True)
        o_ref[0, slot, 0, :, hss[g]] = (acc / l).T.astype(o_ref.dtype)

    def attend_set(tt):
        work, row0 = [], 0
        for slot in range(q_per_step):
            qb = tt + slot * stride
            work.append((slot, qb, row0))
            row0 += (qb + 1) * blk
        ms = {(slot, g): logits_pass(slot, qb, r0, g) for g in range(group) for slot, qb, r0 in work}
        for g in range(group):
            for slot, qb, r0 in work:
                softmax_pv_pass(slot, qb, r0, g, ms[slot, g])

    for tt in range(stride):
        pl.when(t == tt)(functools.partial(attend_set, tt))


def _moba(proj3, rel_bias, *, heads, group=4, q_per_step=4):
    bsz, seq, width = proj3.shape
    nb = seq // MOBA_BLOCK
    stride = nb // q_per_step
    gw = group * HEAD_DIM
    cq, ck, cv = (c * (SEG // gw) for c in (COL_QA, COL_KA, COL_VA))
    proj5 = proj3.reshape(bsz, q_per_step, stride, MOBA_BLOCK, width)
    q_blk = (1, q_per_step, 1, MOBA_BLOCK, gw)
    s_rows = sum(nb - k * stride for k in range(q_per_step)) * MOBA_BLOCK
    out = pl.pallas_call(
        functools.partial(_moba_kernel, n_blocks=nb, group=group, q_per_step=q_per_step),
        grid=(heads // group, bsz, stride),
        in_specs=[
            pl.BlockSpec(memory_space=pltpu.SMEM),
            pl.BlockSpec((1, seq, gw), lambda h, b, i: (b, 0, cq + h)),
            pl.BlockSpec(q_blk, lambda h, b, i: (b, 0, i, 0, cq + h)),
            pl.BlockSpec((1, seq, gw), lambda h, b, i: (b, 0, ck + h)),
            pl.BlockSpec((1, seq, gw), lambda h, b, i: (b, 0, cv + h)),
        ],
        out_specs=pl.BlockSpec(q_blk, lambda h, b, i: (b, 0, i, 0, h)),
        out_shape=jax.ShapeDtypeStruct((bsz, q_per_step, stride, MOBA_BLOCK, heads * HEAD_DIM), BF16),
        scratch_shapes=[
            pltpu.VMEM((group, 2, MOBA_BLOCK, MOBA_BLOCK), F32),
            pltpu.VMEM((group, HEAD_DIM, seq), BF16),
            pltpu.VMEM((group, nb, 16, MOBA_BLOCK), F32),
            pltpu.VMEM((group, s_rows, MOBA_BLOCK), F32),
        ],
        compiler_params=_params("arbitrary", "arbitrary", "arbitrary"),
        name="moba",
    )(rel_bias, proj3, proj5, proj3, proj3)
    return out.reshape(bsz, seq, heads * HEAD_DIM)


TN_DIMS = (((0,), (0,)), ((), ()))


def _cumsum_rows(x, period):
    pos = lax.broadcasted_iota(jnp.int32, x.shape, 0) % period
    sh = 1
    while sh < period:
        x = x + jnp.where(pos >= sh, pltpu.roll(x, sh, axis=0), 0.0)
        sh *= 2
    return x


def _hgrn_kernel(lbl_ref, gain_ref, q_ref, f_ref, i_ref, g_ref, *rest, heads, layer, chunks, n_cast):
    w32_refs = rest[:n_cast]
    o_ref = rest[n_cast]
    w16_refs = rest[n_cast + 1:2 * n_cast + 1]
    st_scr, = rest[2 * n_cast + 1:]
    _cast_slabs(w32_refs, w16_refs)
    ch = HGRN_CHUNK
    dh = HEAD_DIM
    rows = chunks * ch

    @pl.when(pl.program_id(1) == 0)
    def _():
        st_scr[...] = jnp.zeros_like(st_scr)

    lg = lbl_ref[...]
    e = jnp.exp(lg - jnp.max(lg, axis=0, keepdims=True))
    lb = jnp.sum(e[:layer + 1], axis=0, keepdims=True) / jnp.sum(e, axis=0, keepdims=True)

    half = 0.5 * (1.0 - lb)
    f = (lb + half) + half * jnp.tanh(0.5 * f_ref[0])
    bcum = _cumsum_rows(jnp.log(f), ch)
    q_dec = (q_ref[0].astype(F32) * jnp.exp(bcum)).astype(BF16)
    k_dec32 = (1.0 - f) * jnp.exp(-bcum)
    k_dec = k_dec32.astype(BF16)
    decays = [jnp.exp(bcum[(c + 1) * ch - 1:(c + 1) * ch, :]) for c in range(chunks)]
    k_end = [(k_dec32[c * ch:(c + 1) * ch, :] * decays[c]).astype(BF16) for c in range(chunks)]
    v = i_ref[0]
    hg = 0.5 * g_ref[0].astype(F32)
    gate_gain = (hg + hg * jnp.tanh(hg)) * gain_ref[...]

    r = lax.broadcasted_iota(jnp.int32, (rows, rows), 0)
    c_ = lax.broadcasted_iota(jnp.int32, (rows, rows), 1)
    causal_in_chunk = (r >= c_) & (r // ch == c_ // ch)

    sls = [slice(hd * dh, (hd + 1) * dh) for hd in range(heads)]
    a_all = [jnp.where(causal_in_chunk,
                       lax.dot_general(q_dec[:, sl], k_dec[:, sl], NT_DIMS, preferred_element_type=F32), 0.0).astype(BF16)
             for sl in sls]
    o_all = [jnp.dot(a, v[:, sl], preferred_element_type=F32) for a, sl in zip(a_all, sls)]
    sts = [st_scr[hd] for hd in range(heads)]
    outs = [[] for _ in range(heads)]
    for c in range(chunks):
        rs = slice(c * ch, (c + 1) * ch)
        for hd, sl in enumerate(sls):
            st = sts[hd]
            outs[hd].append(o_all[hd][rs] + lax.dot_general(q_dec[rs, sl], st.astype(BF16), NT_DIMS,
                                                            preferred_element_type=F32))
            sts[hd] = st * decays[c][:, sl] + lax.dot_general(v[rs, sl], k_end[c][:, sl], TN_DIMS,
                                                              preferred_element_type=F32)
    for hd, sl in enumerate(sls):
        st_scr[hd] = sts[hd]
        o = jnp.concatenate(outs[hd], axis=0)
        ms = jnp.mean(o * o, axis=-1, keepdims=True)
        o_ref[0, :, sl] = (o * lax.rsqrt(ms + NORM_EPS) * gate_gain[:, sl]).astype(o_ref.dtype)


def _hgrn(proj3, fb3, lb_logits, gain, cast_weights, *, heads, layer, chunks=4):
    bsz, seq, _ = proj3.shape
    width = heads * HEAD_DIM
    rows = chunks * HGRN_CHUNK
    blk = (1, rows, width)
    nc = seq // rows
    cast_specs = _cast_specs(cast_weights, bsz * nc, lambda b, c: b * nc + c)
    outs = pl.pallas_call(
        functools.partial(_hgrn_kernel, heads=heads, layer=layer, chunks=chunks, n_cast=len(cast_weights)),
        grid=(bsz, nc),
        in_specs=[
            pl.BlockSpec(lb_logits.shape, lambda b, c: (0, 0)),
            pl.BlockSpec((1, width), lambda b, c: (0, 0)),
            pl.BlockSpec(blk, lambda b, c: (b, c, COL_QB)),
            pl.BlockSpec(blk, lambda b, c: (b, c, 0)),
            pl.BlockSpec(blk, lambda b, c: (b, c, COL_IB)),
            pl.BlockSpec(blk, lambda b, c: (b, c, COL_GB)),
            *cast_specs,
        ],
        out_specs=[pl.BlockSpec(blk, lambda b, c: (b, c, 0)), *cast_specs],
        out_shape=[jax.ShapeDtypeStruct((bsz, seq, width), BF16),
                   *(jax.ShapeDtypeStruct(w.shape, BF16) for w in cast_weights)],
        scratch_shapes=[pltpu.VMEM((heads, HEAD_DIM, HEAD_DIM), F32)],
        compiler_params=_params("arbitrary", "arbitrary"),
        name="hgrn2",
    )(lb_logits, gain, proj3, fb3, proj3, proj3, *cast_weights)
    return outs[0], outs[1:]


def _mix_kernel(ya_ref, yb_ref, ga0_ref, ga1_ref, gb0_ref, gb1_ref, x_ref, wa_ref, wb_ref, wo_ref, gain_ref,
                *rest, tc, n_cast):
    w32_refs = rest[:n_cast]
    x1_ref, h2_ref = rest[n_cast:n_cast + 2]
    w16_refs = rest[n_cast + 2:2 * n_cast + 2]
    mixed_scr, = rest[2 * n_cast + 2:]
    _cast_slabs(w32_refs, w16_refs)
    d = x_ref.shape[1]
    gate_refs = ((ga0_ref, gb0_ref), (ga1_ref, gb1_ref))
    ya = ya_ref[...]
    yb = yb_ref[...]
    for cb in range(d // tc):
        cs = slice(cb * tc, (cb + 1) * tc)
        ga_ref, gb_ref = gate_refs[(cb * tc) // SEG]
        gs = slice((cb * tc) % SEG, (cb * tc) % SEG + tc)
        pa = jnp.dot(ya, wa_ref[:, cs], preferred_element_type=F32)
        pb = jnp.dot(yb, wb_ref[:, cs], preferred_element_type=F32)
        mixed = (jax.nn.sigmoid(ga_ref[:, gs].astype(F32)) * pa
                 + jax.nn.sigmoid(gb_ref[:, gs].astype(F32)) * pb)
        mixed_scr[:, cs] = mixed.astype(BF16)
    ssq = jnp.zeros((x_ref.shape[0], 1), F32)
    for cb in range(d // tc):
        cs = slice(cb * tc, (cb + 1) * tc)
        x1 = x_ref[:, cs] + jnp.dot(mixed_scr[...], wo_ref[:, cs], preferred_element_type=F32)
        x1_ref[:, cs] = x1
        ssq = ssq + jnp.sum(x1 * x1, axis=-1, keepdims=True)
    inv = lax.rsqrt(ssq * (1.0 / d) + NORM_EPS)
    for cb in range(d // tc):
        cs = slice(cb * tc, (cb + 1) * tc)
        h2_ref[:, cs] = (x1_ref[:, cs] * inv * gain_ref[:, cs]).astype(BF16)


def _mix(ya, yb, proj, x2d, wa, wb, wo, gain, cast_weights, *, tm=512, tc=512):
    n, d = x2d.shape
    row = lambda i: (i, 0)
    seg = lambda c: pl.BlockSpec((tm, SEG), lambda i: (i, c))
    cast_specs = _cast_specs(cast_weights, n // tm, lambda i: i)
    outs = pl.pallas_call(
        functools.partial(_mix_kernel, tc=tc, n_cast=len(cast_weights)),
        grid=(n // tm,),
        in_specs=[
            pl.BlockSpec((tm, A_WIDTH), row),
            pl.BlockSpec((tm, B_WIDTH), row),
            seg(COL_GATE_A), seg(COL_GATE_A + 1), seg(COL_GATE_B), seg(COL_GATE_B + 1),
            pl.BlockSpec((tm, d), row),
            _resident(wa.shape), _resident(wb.shape), _resident(wo.shape), _resident(gain.shape),
            *cast_specs,
        ],
        out_specs=[pl.BlockSpec((tm, d), row), pl.BlockSpec((tm, d), row), *cast_specs],
        out_shape=[jax.ShapeDtypeStruct((n, d), F32), jax.ShapeDtypeStruct((n, d), BF16),
                   *(jax.ShapeDtypeStruct(w.shape, BF16) for w in cast_weights)],
        scratch_shapes=[pltpu.VMEM((tm, d), BF16)],
        compiler_params=_params("arbitrary"),
        name="mix",
    )(ya, yb, proj, proj, proj, proj, x2d, wa, wb, wo, gain, *cast_weights)
    return outs[0], outs[1], outs[2:]


def _ffn_kernel(h_ref, wg_ref, wu_ref, wd_ref, o_ref):
    @pl.when(pl.program_id(1) == 0)
    def _():
        o_ref[...] = jnp.zeros_like(o_ref)

    h = h_ref[...]
    g = jnp.dot(h, wg_ref[...], preferred_element_type=F32)
    u = jnp.dot(h, wu_ref[...], preferred_element_type=F32)
    act = (g * jax.nn.sigmoid(g) * u).astype(BF16)
    o_ref[...] += jnp.dot(act, wd_ref[...], preferred_element_type=F32)


def _ffn(h2, w_gu, w_down, *, tm=1024, tf=512):
    n, d = h2.shape
    dff = w_down.shape[0]
    nj = dff // tf
    return pl.pallas_call(
        _ffn_kernel,
        grid=(n // tm, nj),
        in_specs=[
            pl.BlockSpec((tm, d), lambda i, j: (i, 0)),
            pl.BlockSpec((d, tf), lambda i, j: (0, j)),
            pl.BlockSpec((d, tf), lambda i, j: (0, j + nj)),
            pl.BlockSpec((tf, d), lambda i, j: (j, 0)),
        ],
        out_specs=pl.BlockSpec((tm, d), lambda i, j: (i, 0)),
        out_shape=jax.ShapeDtypeStruct((n, d), F32),
        compiler_params=_params("arbitrary", "arbitrary"),
        name="ffn",
    )(h2, w_gu, w_gu, w_down)


def _ple_kernel(x1_ref, y_ref, p_ref, wg_ref, wp_ref, gain_ref, o_ref, x2_scr, *, tc, final_norm):
    d = x1_ref.shape[1]
    x2_scr[...] = x1_ref[...] + y_ref[...]
    xb = x2_scr[...].astype(BF16)
    pb = p_ref[...].astype(BF16)
    ssq = jnp.zeros((x1_ref.shape[0], 1), F32)
    for cb in range(d // tc):
        cs = slice(cb * tc, (cb + 1) * tc)
        gate = jax.nn.sigmoid(jnp.dot(xb, wg_ref[:, cs], preferred_element_type=F32))
        pe = jnp.dot(pb, wp_ref[:, cs], preferred_element_type=F32)
        x3 = x2_scr[:, cs] + gate * pe
        o_ref[:, cs] = x3
        ssq = ssq + jnp.sum(x3 * x3, axis=-1, keepdims=True)
    if not final_norm:
        return
    inv = lax.rsqrt(ssq * (1.0 / d) + NORM_EPS)
    for cb in range(d // tc):
        cs = slice(cb * tc, (cb + 1) * tc)
        o_ref[:, cs] = o_ref[:, cs] * inv * gain_ref[:, cs]


def _ple(x1, y, p2d, wg, wp, gain, *, final_norm, tm=512, tc=512):
    n, d = x1.shape
    row = lambda i: (i, 0)
    return pl.pallas_call(
        functools.partial(_ple_kernel, tc=tc, final_norm=final_norm),
        grid=(n // tm,),
        in_specs=[
            pl.BlockSpec((tm, d), row),
            pl.BlockSpec((tm, d), row),
            pl.BlockSpec((tm, p2d.shape[1]), row),
            _resident(wg.shape), _resident(wp.shape), _resident(gain.shape),
        ],
        out_specs=pl.BlockSpec((tm, d), row),
        out_shape=jax.ShapeDtypeStruct((n, d), F32),
        scratch_shapes=[pltpu.VMEM((tm, d), F32)],
        compiler_params=_params("arbitrary"),
        name="ple",
    )(x1, y, p2d, wg, wp, gain)


def kernel(x, p, norm_mix, w_in, hgrn_norm, w_proj_a, w_proj_b, w_out, norm_ffn, w_gate_up, w_down, w_ple,
           w_ple_gate, rel_bias, hgrn_lb_logits, norm_final):
    bsz, seq, d = x.shape
    n = bsz * seq
    depth = w_in.shape[0]
    a_heads = A_WIDTH // HEAD_DIM
    b_heads = B_WIDTH // HEAD_DIM

    cscale = np.ones((1, IN_WIDTH), np.float32)
    cscale[:, COL_QA * SEG:(COL_QA + 1) * SEG] = LOG2E / math.sqrt(HEAD_DIM)
    cscale = jnp.asarray(cscale)

    xc = x.reshape(n, d)
    for i in range(depth):
        proj, fb, (wgu,) = _in_proj(xc, norm_mix[i][None], w_in[i], cscale, (w_gate_up[i],))
        proj3 = proj.reshape(bsz, seq, IN_WIDTH)
        ya = _moba(proj3, rel_bias, heads=a_heads)
        yb, (wa, wb, wo, wdn) = _hgrn(proj3, fb.reshape(bsz, seq, SEG), hgrn_lb_logits, hgrn_norm[i][None],
                                      (w_proj_a[i], w_proj_b[i], w_out[i], w_down[i]), heads=b_heads, layer=i)
        x1, h2, (wpg, wpl) = _mix(ya.reshape(n, A_WIDTH), yb.reshape(n, B_WIDTH), proj, xc, wa, wb, wo,
                                  norm_ffn[i][None], (w_ple_gate[i], w_ple[i]))
        y = _ffn(h2, wgu, wdn)
        xc = _ple(x1, y, p[i].reshape(n, -1), wpg, wpl, norm_final[None], final_norm=(i == depth - 1))
    return xc.reshape(bsz, seq, d)
```

```python
import functools
import math

import jax
import jax.numpy as jnp
import numpy as np
from jax import lax
from jax.experimental import pallas as pl
from jax.experimental.pallas import tpu as pltpu

F32 = jnp.float32
BF16 = jnp.bfloat16

D_MODEL = 2048
HEAD_DIM = 128
A_WIDTH = 1024
B_WIDTH = 1024
MOBA_BLOCK = 256
MOBA_TOPK = 3
REL_BUCKETS = 32
REL_MAX_EXACT = 16
REL_MAX_DIST = 128
HGRN_CHUNK = 64
D_FF = 5632
NORM_EPS = 1e-6

COL_QA, COL_KA, COL_VA, COL_QB, COL_FB, COL_IB, COL_GB, COL_GATE_A, COL_GATE_B = 0, 1, 2, 3, 4, 5, 6, 7, 9
IN_WIDTH = 11 * 1024
SEG = 1024

BF16_SUBLANES = 16
MASK_VALUE = -1e30
VMEM_LIMIT = 56 * 1024 * 1024

NT_DIMS = (((1,), (1,)), ((), ()))
LOG2E = math.log2(math.e)


def _bucket_thresholds():
    n = np.arange(0, 4096, dtype=np.int64)
    nf = np.maximum(n, REL_MAX_EXACT).astype(np.float64)
    large = REL_MAX_EXACT + (np.log(nf / REL_MAX_EXACT) / math.log(REL_MAX_DIST / REL_MAX_EXACT)
                             * (REL_BUCKETS - REL_MAX_EXACT)).astype(np.int64)
    large = np.minimum(large, REL_BUCKETS - 1)
    bucket = np.where(n < REL_MAX_EXACT, n, large)
    return [int(np.argmax(bucket >= k)) for k in range(REL_BUCKETS)]


BUCKET_LO = _bucket_thresholds()


def _params(*sem):
    return pltpu.CompilerParams(dimension_semantics=sem, vmem_limit_bytes=VMEM_LIMIT)


def _resident(shape):
    return pl.BlockSpec(shape, lambda *_: (0,) * len(shape), pipeline_mode=pl.Buffered(1))


def _cast_specs(weights, n_steps, step_index):
    specs = []
    for w in weights:
        r = BF16_SUBLANES
        while w.shape[0] % r or w.shape[0] // r > n_steps:
            r += BF16_SUBLANES
        last = w.shape[0] // r - 1
        specs.append(pl.BlockSpec((r, w.shape[1]), lambda *ids, last=last: (jnp.minimum(step_index(*ids), last), 0)))
    return specs


def _cast_slabs(w32_refs, w16_refs):
    for w32, w16 in zip(w32_refs, w16_refs):
        w16[...] = w32[...].astype(BF16)


def _in_proj_kernel(x_ref, gain_ref, w_ref, cscale_ref, *rest, rows_per_step, first_tile, n_cast=0):
    if first_tile:
        o_ref, fb_ref, w16_ref, h_scr = rest
    else:
        rest = rest[2:]
        o_ref, fb_ref = rest[n_cast:n_cast + 2]
        h_scr, = rest[2 * n_cast + 2:]
        _cast_slabs(rest[:n_cast], rest[n_cast + 2:2 * n_cast + 2])
    j = pl.program_id(1)

    @pl.when(j == 0)
    def _():
        for r in range(x_ref.shape[0] // rows_per_step):
            rs = slice(r * rows_per_step, (r + 1) * rows_per_step)
            x = x_ref[rs, :]
            ms = jnp.mean(x * x, axis=-1, keepdims=True)
            h_scr[rs, :] = (x * lax.rsqrt(ms + NORM_EPS) * gain_ref[...]).astype(BF16)

    if first_tile:
        w = w_ref[...].astype(BF16)
        w16_ref[...] = w
    else:
        w = w_ref[...]
    acc = jnp.dot(h_scr[...], w, preferred_element_type=F32)
    o_ref[...] = (acc * cscale_ref[...]).astype(BF16)
    pltpu.store(fb_ref, acc, mask=jnp.broadcast_to(j == COL_FB, acc.shape))


def _in_proj(x2d, gain, w_f32, cscale, cast_weights, *, tm=1024, tn=SEG):
    n, d = x2d.shape
    width = w_f32.shape[1]
    nj = width // tn
    out_shape = [jax.ShapeDtypeStruct((n, width), BF16), jax.ShapeDtypeStruct((n, SEG), F32)]
    scratch = [pltpu.VMEM((tm, d), BF16)]

    def specs(row0, x_mode):
        ins = [
            pl.BlockSpec((tm, d), lambda i, j: (i + row0, 0), **x_mode),
            pl.BlockSpec((1, d), lambda i, j: (0, 0)),
            pl.BlockSpec((d, tn), lambda i, j: (0, j)),
            pl.BlockSpec((1, tn), lambda i, j: (0, j)),
        ]
        outs = [
            pl.BlockSpec((tm, tn), lambda i, j: (i + row0, j)),
            pl.BlockSpec((tm, SEG), lambda i, j: (i + row0, 0)),
        ]
        return ins, outs

    ins, outs = specs(0, dict(pipeline_mode=pl.Buffered(1)))
    proj, fb, w16 = pl.pallas_call(
        functools.partial(_in_proj_kernel, rows_per_step=128, first_tile=True),
        grid=(1, nj),
        in_specs=ins,
        out_specs=outs + [pl.BlockSpec((d, tn), lambda i, j: (0, j))],
        out_shape=out_shape + [jax.ShapeDtypeStruct((d, width), BF16)],
        scratch_shapes=scratch,
        compiler_params=_params("arbitrary", "arbitrary"),
        name="in_proj_first",
    )(x2d, gain, w_f32, cscale)

    ins, outs = specs(1, {})
    any_spec = pl.BlockSpec(memory_space=pl.ANY)
    cast_specs = _cast_specs(cast_weights, (n // tm - 1) * nj, lambda i, j: i * nj + j)
    res = pl.pallas_call(
        functools.partial(_in_proj_kernel, rows_per_step=128, first_tile=False, n_cast=len(cast_weights)),
        grid=(n // tm - 1, nj),
        in_specs=ins + [any_spec, any_spec] + cast_specs,
        out_specs=outs + cast_specs,
        out_shape=out_shape + [jax.ShapeDtypeStruct(w.shape, BF16) for w in cast_weights],
        input_output_aliases={4: 0, 5: 1},
        scratch_shapes=scratch,
        compiler_params=_params("arbitrary", "arbitrary"),
        name="in_proj",
    )(x2d, gain, w16, cscale, proj, fb, *cast_weights)
    return res[0], res[1], res[2:]


def _moba_kernel(rb_ref, qall_ref, q_ref, k_ref, v_ref, o_ref, bias_scr, vt_scr, sel_scr, s_scr, *, n_blocks, group,
                 q_per_step):
    hg = pl.program_id(0)
    b = pl.program_id(1)
    t = pl.program_id(2)
    blk = MOBA_BLOCK
    dh = HEAD_DIM
    sub = 8
    stride = n_blocks // q_per_step

    @pl.when((b == 0) & (t == 0))
    def _build_bias():
        r = lax.broadcasted_iota(jnp.int32, (blk, blk), 1)
        c = lax.broadcasted_iota(jnp.int32, (blk, blk), 0)
        for g in range(group):
            hd = hg * group + g
            for d in range(2):
                rel = d * blk + r - c
                bias = jnp.full((blk, blk), rb_ref[REL_BUCKETS - 1, hd] * LOG2E, F32)
                for kk in range(REL_BUCKETS - 2, -1, -1):
                    bias = jnp.where(rel < BUCKET_LO[kk + 1], rb_ref[kk, hd] * LOG2E, bias)
                if d == 0:
                    bias = jnp.where(rel < 0, MASK_VALUE, bias)
                bias_scr[g, d] = bias

    @pl.when(t == 0)
    def _per_sequence():
        for g in range(group):
            hs = slice(g * dh, (g + 1) * dh)
            means = []
            for jb in range(n_blocks):
                rs = slice(jb * blk, (jb + 1) * blk)
                means.append(jnp.mean(k_ref[0, rs, hs].astype(F32), axis=0, keepdims=True))
                vt_scr[g, :, rs] = v_ref[0, rs, hs].T
            km = jnp.concatenate(means + [jnp.zeros((16 - n_blocks, dh), F32)], axis=0)

            km_hi = km.astype(BF16)
            km_lo = (km - km_hi.astype(F32)).astype(BF16)
            qall = qall_ref[0, :, hs]
            sc = (lax.dot_general(km_hi, qall, NT_DIMS, preferred_element_type=F32)
                  + lax.dot_general(km_lo, qall, NT_DIMS, preferred_element_type=F32))
            rows = lax.broadcasted_iota(jnp.int32, sc.shape, 0)
            rows_f = rows.astype(F32)
            past = rows < lax.broadcasted_iota(jnp.int32, sc.shape, 1) // blk
            s = jnp.where(past, sc, -jnp.inf)
            picked = jnp.zeros(sc.shape, jnp.bool_)
            for _ in range(MOBA_TOPK):
                top = jnp.max(s, axis=0, keepdims=True)
                first = jnp.min(jnp.where(s == top, rows_f, 1e9), axis=0, keepdims=True)
                hit = rows_f == first
                picked = picked | hit
                s = jnp.where(hit, -jnp.inf, s)
            mask = jnp.where(picked & past, 0.0, MASK_VALUE)
            for qb in range(n_blocks):
                sel_scr[g, qb] = mask[:, qb * blk:(qb + 1) * blk]

    hss = [slice(g * dh, (g + 1) * dh) for g in range(group)]

    def logits_pass(slot, qb, row0, g):
        nt = qb + 1
        far = rb_ref[REL_BUCKETS - 1, hg * group + g] * LOG2E
        s = lax.dot_general(k_ref[0, :nt * blk, hss[g]], q_ref[0, slot, 0, :, hss[g]], NT_DIMS,
                            preferred_element_type=F32)
        m8 = None
        for j in range(nt):
            rs = slice(j * blk, (j + 1) * blk)
            d = nt - 1 - j
            if d == 0:
                sj = s[rs] + bias_scr[g, 0]
            elif d == 1:
                sj = s[rs] + bias_scr[g, 1] + sel_scr[g, qb, j:j + 1, :]
            else:
                sj = s[rs] + (sel_scr[g, qb, j:j + 1, :] + far)
            s_scr[g, row0 + j * blk:row0 + (j + 1) * blk, :] = sj
            mj = jnp.max(sj.reshape(blk // sub, sub, blk), axis=0)
            m8 = mj if m8 is None else jnp.maximum(m8, mj)
        return jnp.max(m8, axis=0, keepdims=True)

    def softmax_pv_pass(slot, qb, row0, g, m):
        l8 = jnp.zeros((sub, blk), F32)
        acc = None
        for j in range(qb + 1):
            rs = slice(j * blk, (j + 1) * blk)
            p = jnp.exp2(s_scr[g, row0 + j * blk:row0 + (j + 1) * blk, :] - m)
            l8 = l8 + jnp.sum(p.reshape(blk // sub, sub, blk), axis=0)
            pv = jnp.dot(vt_scr[g, :, rs], p.astype(BF16), preferred_element_type=F32)
            acc = pv if acc is None else acc + pv
        l = jnp.sum(l8, axis=0, keepdims=True)
        o_ref[0, slot, 0, :, hss[g]] = (acc / l).T.astype(o_ref.dtype)

    def attend_set(tt):
        work, row0 = [], 0
        for slot in range(q_per_step):
            qb = tt + slot * stride
            work.append((slot, qb, row0))
            row0 += (qb + 1) * blk
        ms = {(slot, g): logits_pass(slot, qb, r0, g) for g in range(group) for slot, qb, r0 in work}
        for g in range(group):
            for slot, qb, r0 in work:
                softmax_pv_pass(slot, qb, r0, g, ms[slot, g])

    for tt in range(stride):
        pl.when(t == tt)(functools.partial(attend_set, tt))


def _moba(proj3, rel_bias, *, heads, group=4, q_per_step=4):
    bsz, seq, width = proj3.shape
    nb = seq // MOBA_BLOCK
    stride = nb // q_per_step
    gw = group * HEAD_DIM
    cq, ck, cv = (c * (SEG // gw) for c in (COL_QA, COL_KA, COL_VA))
    proj5 = proj3.reshape(bsz, q_per_step, stride, MOBA_BLOCK, width)
    q_blk = (1, q_per_step, 1, MOBA_BLOCK, gw)
    s_rows = sum(nb - k * stride for k in range(q_per_step)) * MOBA_BLOCK
    out = pl.pallas_call(
        functools.partial(_moba_kernel, n_blocks=nb, group=group, q_per_step=q_per_step),
        grid=(heads // group, bsz, stride),
        in_specs=[
            pl.BlockSpec(memory_space=pltpu.SMEM),
            pl.BlockSpec((1, seq, gw), lambda h, b, i: (b, 0, cq + h)),
            pl.BlockSpec(q_blk, lambda h, b, i: (b, 0, i, 0, cq + h)),
            pl.BlockSpec((1, seq, gw), lambda h, b, i: (b, 0, ck + h)),
            pl.BlockSpec((1, seq, gw), lambda h, b, i: (b, 0, cv + h)),
        ],
        out_specs=pl.BlockSpec(q_blk, lambda h, b, i: (b, 0, i, 0, h)),
        out_shape=jax.ShapeDtypeStruct((bsz, q_per_step, stride, MOBA_BLOCK, heads * HEAD_DIM), BF16),
        scratch_shapes=[
            pltpu.VMEM((group, 2, MOBA_BLOCK, MOBA_BLOCK), F32),
            pltpu.VMEM((group, HEAD_DIM, seq), BF16),
            pltpu.VMEM((group, nb, 16, MOBA_BLOCK), F32),
            pltpu.VMEM((group, s_rows, MOBA_BLOCK), F32),
        ],
        compiler_params=_params("arbitrary", "arbitrary", "arbitrary"),
        name="moba",
    )(rel_bias, proj3, proj5, proj3, proj3)
    return out.reshape(bsz, seq, heads * HEAD_DIM)


TN_DIMS = (((0,), (0,)), ((), ()))


def _cumsum_rows(x, period):
    pos = lax.broadcasted_iota(jnp.int32, x.shape, 0) % period
    sh = 1
    while sh < period:
        x = x + jnp.where(pos >= sh, pltpu.roll(x, sh, axis=0), 0.0)
        sh *= 2
    return x


def _hgrn_kernel(lbl_ref, gain_ref, q_ref, f_ref, i_ref, g_ref, *rest, heads, layer, chunks, n_cast):
    w32_refs = rest[:n_cast]
    o_ref = rest[n_cast]
    w16_refs = rest[n_cast + 1:2 * n_cast + 1]
    st_scr, = rest[2 * n_cast + 1:]
    _cast_slabs(w32_refs, w16_refs)
    ch = HGRN_CHUNK
    dh = HEAD_DIM
    rows = chunks * ch

    @pl.when(pl.program_id(1) == 0)
    def _():
        st_scr[...] = jnp.zeros_like(st_scr)

    lg = lbl_ref[...]
    e = jnp.exp(lg - jnp.max(lg, axis=0, keepdims=True))
    lb = jnp.sum(e[:layer + 1], axis=0, keepdims=True) / jnp.sum(e, axis=0, keepdims=True)

    half = 0.5 * (1.0 - lb)
    f = (lb + half) + half * jnp.tanh(0.5 * f_ref[0])
    bcum = _cumsum_rows(jnp.log(f), ch)
    q_dec = (q_ref[0].astype(F32) * jnp.exp(bcum)).astype(BF16)
    k_dec32 = (1.0 - f) * jnp.exp(-bcum)
    k_dec = k_dec32.astype(BF16)
    decays = [jnp.exp(bcum[(c + 1) * ch - 1:(c + 1) * ch, :]) for c in range(chunks)]
    k_end = [(k_dec32[c * ch:(c + 1) * ch, :] * decays[c]).astype(BF16) for c in range(chunks)]
    v = i_ref[0]
    hg = 0.5 * g_ref[0].astype(F32)
    gate_gain = (hg + hg * jnp.tanh(hg)) * gain_ref[...]

    r = lax.broadcasted_iota(jnp.int32, (rows, rows), 0)
    c_ = lax.broadcasted_iota(jnp.int32, (rows, rows), 1)
    causal_in_chunk = (r >= c_) & (r // ch == c_ // ch)

    sls = [slice(hd * dh, (hd + 1) * dh) for hd in range(heads)]
    a_all = [jnp.where(causal_in_chunk,
                       lax.dot_general(q_dec[:, sl], k_dec[:, sl], NT_DIMS, preferred_element_type=F32), 0.0).astype(BF16)
             for sl in sls]
    o_all = [jnp.dot(a, v[:, sl], preferred_element_type=F32) for a, sl in zip(a_all, sls)]
    sts = [st_scr[hd] for hd in range(heads)]
    outs = [[] for _ in range(heads)]
    for c in range(chunks):
        rs = slice(c * ch, (c + 1) * ch)
        for hd, sl in enumerate(sls):
            st = sts[hd]
            outs[hd].append(o_all[hd][rs] + lax.dot_general(q_dec[rs, sl], st.astype(BF16), NT_DIMS,
                                                            preferred_element_type=F32))
            sts[hd] = st * decays[c][:, sl] + lax.dot_general(v[rs, sl], k_end[c][:, sl], TN_DIMS,
                                                              preferred_element_type=F32)
    for hd, sl in enumerate(sls):
        st_scr[hd] = sts[hd]
        o = jnp.concatenate(outs[hd], axis=0)
        ms = jnp.mean(o * o, axis=-1, keepdims=True)
        o_ref[0, :, sl] = (o * lax.rsqrt(ms + NORM_EPS) * gate_gain[:, sl]).astype(o_ref.dtype)


def _hgrn(proj3, fb3, lb_logits, gain, cast_weights, *, heads, layer, chunks=4):
    bsz, seq, _ = proj3.shape
    width = heads * HEAD_DIM
    rows = chunks * HGRN_CHUNK
    blk = (1, rows, width)
    nc = seq // rows
    cast_specs = _cast_specs(cast_weights, bsz * nc, lambda b, c: b * nc + c)
    outs = pl.pallas_call(
        functools.partial(_hgrn_kernel, heads=heads, layer=layer, chunks=chunks, n_cast=len(cast_weights)),
        grid=(bsz, nc),
        in_specs=[
            pl.BlockSpec(lb_logits.shape, lambda b, c: (0, 0)),
            pl.BlockSpec((1, width), lambda b, c: (0, 0)),
            pl.BlockSpec(blk, lambda b, c: (b, c, COL_QB)),
            pl.BlockSpec(blk, lambda b, c: (b, c, 0)),
            pl.BlockSpec(blk, lambda b, c: (b, c, COL_IB)),
            pl.BlockSpec(blk, lambda b, c: (b, c, COL_GB)),
            *cast_specs,
        ],
        out_specs=[pl.BlockSpec(blk, lambda b, c: (b, c, 0)), *cast_specs],
        out_shape=[jax.ShapeDtypeStruct((bsz, seq, width), BF16),
                   *(jax.ShapeDtypeStruct(w.shape, BF16) for w in cast_weights)],
        scratch_shapes=[pltpu.VMEM((heads, HEAD_DIM, HEAD_DIM), F32)],
        compiler_params=_params("arbitrary", "arbitrary"),
        name="hgrn2",
    )(lb_logits, gain, proj3, fb3, proj3, proj3, *cast_weights)
    return outs[0], outs[1:]


def _mix_kernel(ya_ref, yb_ref, ga0_ref, ga1_ref, gb0_ref, gb1_ref, x_ref, wa_ref, wb_ref, wo_ref, gain_ref,
                *rest, tc, n_cast):
    w32_refs = rest[:n_cast]
    x1_ref, h2_ref = rest[n_cast:n_cast + 2]
    w16_refs = rest[n_cast + 2:2 * n_cast + 2]
    mixed_scr, = rest[2 * n_cast + 2:]
    _cast_slabs(w32_refs, w16_refs)
    d = x_ref.shape[1]
    gate_refs = ((ga0_ref, gb0_ref), (ga1_ref, gb1_ref))
    ya = ya_ref[...]
    yb = yb_ref[...]
    for cb in range(d // tc):
        cs = slice(cb * tc, (cb + 1) * tc)
        ga_ref, gb_ref = gate_refs[(cb * tc) // SEG]
        gs = slice((cb * tc) % SEG, (cb * tc) % SEG + tc)
        pa = jnp.dot(ya, wa_ref[:, cs], preferred_element_type=F32)
        pb = jnp.dot(yb, wb_ref[:, cs], preferred_element_type=F32)
        mixed = (jax.nn.sigmoid(ga_ref[:, gs].astype(F32)) * pa
                 + jax.nn.sigmoid(gb_ref[:, gs].astype(F32)) * pb)
        mixed_scr[:, cs] = mixed.astype(BF16)
    ssq = jnp.zeros((x_ref.shape[0], 1), F32)
    for cb in range(d // tc):
        cs = slice(cb * tc, (cb + 1) * tc)
        x1 = x_ref[:, cs] + jnp.dot(mixed_scr[...], wo_ref[:, cs], preferred_element_type=F32)
        x1_ref[:, cs] = x1
        ssq = ssq + jnp.sum(x1 * x1, axis=-1, keepdims=True)
    inv = lax.rsqrt(ssq * (1.0 / d) + NORM_EPS)
    for cb in range(d // tc):
        cs = slice(cb * tc, (cb + 1) * tc)
        h2_ref[:, cs] = (x1_ref[:, cs] * inv * gain_ref[:, cs]).astype(BF16)


def _mix(ya, yb, proj, x2d, wa, wb, wo, gain, cast_weights, *, tm=512, tc=512):
    n, d = x2d.shape
    row = lambda i: (i, 0)
    seg = lambda c: pl.BlockSpec((tm, SEG), lambda i: (i, c))
    cast_specs = _cast_specs(cast_weights, n // tm, lambda i: i)
    outs = pl.pallas_call(
        functools.partial(_mix_kernel, tc=tc, n_cast=len(cast_weights)),
        grid=(n // tm,),
        in_specs=[
            pl.BlockSpec((tm, A_WIDTH), row),
            pl.BlockSpec((tm, B_WIDTH), row),
            seg(COL_GATE_A), seg(COL_GATE_A + 1), seg(COL_GATE_B), seg(COL_GATE_B + 1),
            pl.BlockSpec((tm, d), row),
            _resident(wa.shape), _resident(wb.shape), _resident(wo.shape), _resident(gain.shape),
            *cast_specs,
        ],
        out_specs=[pl.BlockSpec((tm, d), row), pl.BlockSpec((tm, d), row), *cast_specs],
        out_shape=[jax.ShapeDtypeStruct((n, d), F32), jax.ShapeDtypeStruct((n, d), BF16),
                   *(jax.ShapeDtypeStruct(w.shape, BF16) for w in cast_weights)],
        scratch_shapes=[pltpu.VMEM((tm, d), BF16)],
        compiler_params=_params("arbitrary"),
        name="mix",
    )(ya, yb, proj, proj, proj, proj, x2d, wa, wb, wo, gain, *cast_weights)
    return outs[0], outs[1], outs[2:]


def _ffn_kernel(h_ref, wg_ref, wu_ref, wd_ref, o_ref):
    @pl.when(pl.program_id(1) == 0)
    def _():
        o_ref[...] = jnp.zeros_like(o_ref)

    h = h_ref[...]
    g = jnp.dot(h, wg_ref[...], preferred_element_type=F32)
    u = jnp.dot(h, wu_ref[...], preferred_element_type=F32)
    act = (g * jax.nn.sigmoid(g) * u).astype(BF16)
    o_ref[...] += jnp.dot(act, wd_ref[...], preferred_element_type=F32)


def _ffn(h2, w_gu, w_down, *, tm=1024, tf=512):
    n, d = h2.shape
    dff = w_down.shape[0]
    nj = dff // tf
    return pl.pallas_call(
        _ffn_kernel,
        grid=(n // tm, nj),
        in_specs=[
            pl.BlockSpec((tm, d), lambda i, j: (i, 0)),
            pl.BlockSpec((d, tf), lambda i, j: (0, j)),
            pl.BlockSpec((d, tf), lambda i, j: (0, j + nj)),
            pl.BlockSpec((tf, d), lambda i, j: (j, 0)),
        ],
        out_specs=pl.BlockSpec((tm, d), lambda i, j: (i, 0)),
        out_shape=jax.ShapeDtypeStruct((n, d), F32),
        compiler_params=_params("arbitrary", "arbitrary"),
        name="ffn",
    )(h2, w_gu, w_gu, w_down)


def _ple_kernel(x1_ref, y_ref, p_ref, wg_ref, wp_ref, gain_ref, o_ref, x2_scr, *, tc, final_norm, row_groups):
    tm, d = x1_ref.shape
    tr = tm // row_groups
    for rg in range(row_groups):
        rs = slice(rg * tr, (rg + 1) * tr)
        x2_scr[rs, :] = x1_ref[rs, :] + y_ref[rs, :]
        xb = x2_scr[rs, :].astype(BF16)
        pb = p_ref[rs, :].astype(BF16)
        ssq = jnp.zeros((tr, 1), F32)
        for cb in range(d // tc):
            cs = slice(cb * tc, (cb + 1) * tc)
            gate = jax.nn.sigmoid(jnp.dot(xb, wg_ref[:, cs], preferred_element_type=F32))
            pe = jnp.dot(pb, wp_ref[:, cs], preferred_element_type=F32)
            x3 = x2_scr[rs, cs] + gate * pe
            o_ref[rs, cs] = x3
            ssq = ssq + jnp.sum(x3 * x3, axis=-1, keepdims=True)
        if final_norm:
            inv = lax.rsqrt(ssq * (1.0 / d) + NORM_EPS)
            for cb in range(d // tc):
                cs = slice(cb * tc, (cb + 1) * tc)
                o_ref[rs, cs] = o_ref[rs, cs] * inv * gain_ref[:, cs]


def _ple(x1, y, p2d, wg, wp, gain, *, final_norm, tm=512, tc=512, row_groups=2):
    n, d = x1.shape
    row = lambda i: (i, 0)
    return pl.pallas_call(
        functools.partial(_ple_kernel, tc=tc, final_norm=final_norm, row_groups=row_groups),
        grid=(n // tm,),
        in_specs=[
            pl.BlockSpec((tm, d), row),
            pl.BlockSpec((tm, d), row),
            pl.BlockSpec((tm, p2d.shape[1]), row),
            _resident(wg.shape), _resident(wp.shape), _resident(gain.shape),
        ],
        out_specs=pl.BlockSpec((tm, d), row),
        out_shape=jax.ShapeDtypeStruct((n, d), F32),
        scratch_shapes=[pltpu.VMEM((tm, d), F32)],
        compiler_params=_params("arbitrary"),
        name="ple",
    )(x1, y, p2d, wg, wp, gain)


def kernel(x, p, norm_mix, w_in, hgrn_norm, w_proj_a, w_proj_b, w_out, norm_ffn, w_gate_up, w_down, w_ple,
           w_ple_gate, rel_bias, hgrn_lb_logits, norm_final):
    bsz, seq, d = x.shape
    n = bsz * seq
    depth = w_in.shape[0]
    a_heads = A_WIDTH // HEAD_DIM
    b_heads = B_WIDTH // HEAD_DIM

    cscale = np.ones((1, IN_WIDTH), np.float32)
    cscale[:, COL_QA * SEG:(COL_QA + 1) * SEG] = LOG2E / math.sqrt(HEAD_DIM)
    cscale = jnp.asarray(cscale)

    xc = x.reshape(n, d)
    for i in range(depth):
        proj, fb, (wgu, wdn) = _in_proj(xc, norm_mix[i][None], w_in[i], cscale, (w_gate_up[i], w_down[i]))
        proj3 = proj.reshape(bsz, seq, IN_WIDTH)
        ya = _moba(proj3, rel_bias, heads=a_heads)
        yb, (wa, wb, wo) = _hgrn(proj3, fb.reshape(bsz, seq, SEG), hgrn_lb_logits, hgrn_norm[i][None],
                                 (w_proj_a[i], w_proj_b[i], w_out[i]), heads=b_heads, layer=i)
        x1, h2, (wpg, wpl) = _mix(ya.reshape(n, A_WIDTH), yb.reshape(n, B_WIDTH), proj, xc, wa, wb, wo,
                                  norm_ffn[i][None], (w_ple_gate[i], w_ple[i]))
        y = _ffn(h2, wgu, wdn)
        xc = _ple(x1, y, p[i].reshape(n, -1), wpg, wpl, norm_final[None], final_norm=(i == depth - 1))
    return xc.reshape(bsz, seq, d)
```

```python
import functools
import math

import jax
import jax.numpy as jnp
import numpy as np
from jax import lax
from jax.experimental import pallas as pl
from jax.experimental.pallas import tpu as pltpu

F32 = jnp.float32
BF16 = jnp.bfloat16

D_MODEL = 2048
HEAD_DIM = 128
A_WIDTH = 1024
B_WIDTH = 1024
MOBA_BLOCK = 256
MOBA_TOPK = 3
REL_BUCKETS = 32
REL_MAX_EXACT = 16
REL_MAX_DIST = 128
HGRN_CHUNK = 64
D_FF = 5632
NORM_EPS = 1e-6

COL_QA, COL_KA, COL_VA, COL_QB, COL_FB, COL_IB, COL_GB, COL_GATE_A, COL_GATE_B = 0, 1, 2, 3, 4, 5, 6, 7, 9
IN_WIDTH = 11 * 1024
SEG = 1024

BF16_SUBLANES = 16
MASK_VALUE = -1e30
VMEM_LIMIT = 56 * 1024 * 1024

NT_DIMS = (((1,), (1,)), ((), ()))
LOG2E = math.log2(math.e)


def _bucket_thresholds():
    n = np.arange(0, 4096, dtype=np.int64)
    nf = np.maximum(n, REL_MAX_EXACT).astype(np.float64)
    large = REL_MAX_EXACT + (np.log(nf / REL_MAX_EXACT) / math.log(REL_MAX_DIST / REL_MAX_EXACT)
                             * (REL_BUCKETS - REL_MAX_EXACT)).astype(np.int64)
    large = np.minimum(large, REL_BUCKETS - 1)
    bucket = np.where(n < REL_MAX_EXACT, n, large)
    return [int(np.argmax(bucket >= k)) for k in range(REL_BUCKETS)]


BUCKET_LO = _bucket_thresholds()


def _params(*sem):
    return pltpu.CompilerParams(dimension_semantics=sem, vmem_limit_bytes=VMEM_LIMIT)


def _resident(shape):
    return pl.BlockSpec(shape, lambda *_: (0,) * len(shape), pipeline_mode=pl.Buffered(1))


def _cast_specs(weights, n_steps, step_index):
    specs = []
    for w in weights:
        r = BF16_SUBLANES
        while w.shape[0] % r or w.shape[0] // r > n_steps:
            r += BF16_SUBLANES
        last = w.shape[0] // r - 1
        specs.append(pl.BlockSpec((r, w.shape[1]), lambda *ids, last=last: (jnp.minimum(step_index(*ids), last), 0)))
    return specs


def _cast_slabs(w32_refs, w16_refs):
    for w32, w16 in zip(w32_refs, w16_refs):
        w16[...] = w32[...].astype(BF16)


def _in_proj_kernel(x_ref, gain_ref, w_ref, cscale_ref, *rest, rows_per_step, first_tile, n_cast=0):
    if first_tile:
        o_ref, fb_ref, w16_ref, h_scr = rest
    else:
        rest = rest[2:]
        o_ref, fb_ref = rest[n_cast:n_cast + 2]
        h_scr, = rest[2 * n_cast + 2:]
        _cast_slabs(rest[:n_cast], rest[n_cast + 2:2 * n_cast + 2])
    j = pl.program_id(1)

    @pl.when(j == 0)
    def _():
        for r in range(x_ref.shape[0] // rows_per_step):
            rs = slice(r * rows_per_step, (r + 1) * rows_per_step)
            x = x_ref[rs, :]
            ms = jnp.mean(x * x, axis=-1, keepdims=True)
            h_scr[rs, :] = (x * lax.rsqrt(ms + NORM_EPS) * gain_ref[...]).astype(BF16)

    if first_tile:
        w = w_ref[...].astype(BF16)
        w16_ref[...] = w
    else:
        w = w_ref[...]
    acc = jnp.dot(h_scr[...], w, preferred_element_type=F32)
    o_ref[...] = (acc * cscale_ref[...]).astype(BF16)
    pltpu.store(fb_ref, acc, mask=jnp.broadcast_to(j == COL_FB, acc.shape))


def _in_proj(x2d, gain, w_f32, cscale, cast_weights, *, tm=1024, tn=SEG):
    n, d = x2d.shape
    width = w_f32.shape[1]
    nj = width // tn
    out_shape = [jax.ShapeDtypeStruct((n, width), BF16), jax.ShapeDtypeStruct((n, SEG), F32)]
    scratch = [pltpu.VMEM((tm, d), BF16)]

    def specs(row0, x_mode):
        ins = [
            pl.BlockSpec((tm, d), lambda i, j: (i + row0, 0), **x_mode),
            pl.BlockSpec((1, d), lambda i, j: (0, 0)),
            pl.BlockSpec((d, tn), lambda i, j: (0, j)),
            pl.BlockSpec((1, tn), lambda i, j: (0, j)),
        ]
        outs = [
            pl.BlockSpec((tm, tn), lambda i, j: (i + row0, j)),
            pl.BlockSpec((tm, SEG), lambda i, j: (i + row0, 0)),
        ]
        return ins, outs

    ins, outs = specs(0, dict(pipeline_mode=pl.Buffered(1)))
    proj, fb, w16 = pl.pallas_call(
        functools.partial(_in_proj_kernel, rows_per_step=128, first_tile=True),
        grid=(1, nj),
        in_specs=ins,
        out_specs=outs + [pl.BlockSpec((d, tn), lambda i, j: (0, j))],
        out_shape=out_shape + [jax.ShapeDtypeStruct((d, width), BF16)],
        scratch_shapes=scratch,
        compiler_params=_params("arbitrary", "arbitrary"),
        name="in_proj_first",
    )(x2d, gain, w_f32, cscale)

    ins, outs = specs(1, {})
    any_spec = pl.BlockSpec(memory_space=pl.ANY)
    cast_specs = _cast_specs(cast_weights, (n // tm - 1) * nj, lambda i, j: i * nj + j)
    res = pl.pallas_call(
        functools.partial(_in_proj_kernel, rows_per_step=128, first_tile=False, n_cast=len(cast_weights)),
        grid=(n // tm - 1, nj),
        in_specs=ins + [any_spec, any_spec] + cast_specs,
        out_specs=outs + cast_specs,
        out_shape=out_shape + [jax.ShapeDtypeStruct(w.shape, BF16) for w in cast_weights],
        input_output_aliases={4: 0, 5: 1},
        scratch_shapes=scratch,
        compiler_params=_params("arbitrary", "arbitrary"),
        name="in_proj",
    )(x2d, gain, w16, cscale, proj, fb, *cast_weights)
    return res[0], res[1], res[2:]


def _moba_kernel(rb_ref, qall_ref, q_ref, k_ref, v_ref, o_ref, bias_scr, vt_scr, sel_scr, s_scr, *, n_blocks, group,
                 q_per_step):
    hg = pl.program_id(0)
    b = pl.program_id(1)
    t = pl.program_id(2)
    blk = MOBA_BLOCK
    dh = HEAD_DIM
    sub = 8
    stride = n_blocks // q_per_step

    @pl.when((b == 0) & (t == 0))
    def _build_bias():
        r = lax.broadcasted_iota(jnp.int32, (blk, blk), 1)
        c = lax.broadcasted_iota(jnp.int32, (blk, blk), 0)
        for g in range(group):
            hd = hg * group + g
            for d in range(2):
                rel = d * blk + r - c
                bias = jnp.full((blk, blk), rb_ref[REL_BUCKETS - 1, hd] * LOG2E, F32)
                for kk in range(REL_BUCKETS - 2, -1, -1):
                    bias = jnp.where(rel < BUCKET_LO[kk + 1], rb_ref[kk, hd] * LOG2E, bias)
                if d == 0:
                    bias = jnp.where(rel < 0, MASK_VALUE, bias)
                bias_scr[g, d] = bias

    @pl.when(t == 0)
    def _per_sequence():
        for g in range(group):
            hs = slice(g * dh, (g + 1) * dh)
            means = []
            for jb in range(n_blocks):
                rs = slice(jb * blk, (jb + 1) * blk)
                means.append(jnp.mean(k_ref[0, rs, hs].astype(F32), axis=0, keepdims=True))
                vt_scr[g, :, rs] = v_ref[0, rs, hs].T
            km = jnp.concatenate(means + [jnp.zeros((16 - n_blocks, dh), F32)], axis=0)

            km_hi = km.astype(BF16)
            km_lo = (km - km_hi.astype(F32)).astype(BF16)
            qall = qall_ref[0, :, hs]
            sc = (lax.dot_general(km_hi, qall, NT_DIMS, preferred_element_type=F32)
                  + lax.dot_general(km_lo, qall, NT_DIMS, preferred_element_type=F32))
            rows = lax.broadcasted_iota(jnp.int32, sc.shape, 0)
            rows_f = rows.astype(F32)
            past = rows < lax.broadcasted_iota(jnp.int32, sc.shape, 1) // blk
            s = jnp.where(past, sc, -jnp.inf)
            picked = jnp.zeros(sc.shape, jnp.bool_)
            for _ in range(MOBA_TOPK):
                top = jnp.max(s, axis=0, keepdims=True)
                first = jnp.min(jnp.where(s == top, rows_f, 1e9), axis=0, keepdims=True)
                hit = rows_f == first
                picked = picked | hit
                s = jnp.where(hit, -jnp.inf, s)
            mask = jnp.where(picked & past, 0.0, MASK_VALUE)
            for qb in range(n_blocks):
                sel_scr[g, qb] = mask[:, qb * blk:(qb + 1) * blk]

    hss = [slice(g * dh, (g + 1) * dh) for g in range(group)]

    def logits_pass(slot, qb, row0, g):
        nt = qb + 1
        far = rb_ref[REL_BUCKETS - 1, hg * group + g] * LOG2E
        s = lax.dot_general(k_ref[0, :nt * blk, hss[g]], q_ref[0, slot, 0, :, hss[g]], NT_DIMS,
                            preferred_element_type=F32)
        m8 = None
        for j in range(nt):
            rs = slice(j * blk, (j + 1) * blk)
            d = nt - 1 - j
            if d == 0:
                sj = s[rs] + bias_scr[g, 0]
            elif d == 1:
                sj = s[rs] + bias_scr[g, 1] + sel_scr[g, qb, j:j + 1, :]
            else:
                sj = s[rs] + (sel_scr[g, qb, j:j + 1, :] + far)
            s_scr[g, row0 + j * blk:row0 + (j + 1) * blk, :] = sj
            mj = jnp.max(sj.reshape(blk // sub, sub, blk), axis=0)
            m8 = mj if m8 is None else jnp.maximum(m8, mj)
        return jnp.max(m8, axis=0, keepdims=True)

    def softmax_pv_pass(slot, qb, row0, g, m):
        l8 = jnp.zeros((sub, blk), F32)
        acc = None
        for j in range(qb + 1):
            rs = slice(j * blk, (j + 1) * blk)
            p = jnp.exp2(s_scr[g, row0 + j * blk:row0 + (j + 1) * blk, :] - m)
            l8 = l8 + jnp.sum(p.reshape(blk // sub, sub, blk), axis=0)
            pv = jnp.dot(vt_scr[g, :, rs], p.astype(BF16), preferred_element_type=F32)
            acc = pv if acc is None else acc + pv
        l = jnp.sum(l8, axis=0, keepdims=True)
        o_ref[0, slot, 0, :, hss[g]] = (acc / l).T.astype(o_ref.dtype)

    def attend_set(tt):
        work, row0 = [], 0
        for slot in range(q_per_step):
            qb = tt + slot * stride
            work.append((slot, qb, row0))
            row0 += (qb + 1) * blk
        ms = {(slot, g): logits_pass(slot, qb, r0, g) for g in range(group) for slot, qb, r0 in work}
        for g in range(group):
            for slot, qb, r0 in work:
                softmax_pv_pass(slot, qb, r0, g, ms[slot, g])

    for tt in range(stride):
        pl.when(t == tt)(functools.partial(attend_set, tt))


def _moba(proj3, rel_bias, *, heads, group=4, q_per_step=4):
    bsz, seq, width = proj3.shape
    nb = seq // MOBA_BLOCK
    stride = nb // q_per_step
    gw = group * HEAD_DIM
    cq, ck, cv = (c * (SEG // gw) for c in (COL_QA, COL_KA, COL_VA))
    proj5 = proj3.reshape(bsz, q_per_step, stride, MOBA_BLOCK, width)
    q_blk = (1, q_per_step, 1, MOBA_BLOCK, gw)
    s_rows = sum(nb - k * stride for k in range(q_per_step)) * MOBA_BLOCK
    out = pl.pallas_call(
        functools.partial(_moba_kernel, n_blocks=nb, group=group, q_per_step=q_per_step),
        grid=(heads // group, bsz, stride),
        in_specs=[
            pl.BlockSpec(memory_space=pltpu.SMEM),
            pl.BlockSpec((1, seq, gw), lambda h, b, i: (b, 0, cq + h)),
            pl.BlockSpec(q_blk, lambda h, b, i: (b, 0, i, 0, cq + h)),
            pl.BlockSpec((1, seq, gw), lambda h, b, i: (b, 0, ck + h)),
            pl.BlockSpec((1, seq, gw), lambda h, b, i: (b, 0, cv + h)),
        ],
        out_specs=pl.BlockSpec(q_blk, lambda h, b, i: (b, 0, i, 0, h)),
        out_shape=jax.ShapeDtypeStruct((bsz, q_per_step, stride, MOBA_BLOCK, heads * HEAD_DIM), BF16),
        scratch_shapes=[
            pltpu.VMEM((group, 2, MOBA_BLOCK, MOBA_BLOCK), F32),
            pltpu.VMEM((group, HEAD_DIM, seq), BF16),
            pltpu.VMEM((group, nb, 16, MOBA_BLOCK), F32),
            pltpu.VMEM((group, s_rows, MOBA_BLOCK), F32),
        ],
        compiler_params=_params("arbitrary", "arbitrary", "arbitrary"),
        name="moba",
    )(rel_bias, proj3, proj5, proj3, proj3)
    return out.reshape(bsz, seq, heads * HEAD_DIM)


TN_DIMS = (((0,), (0,)), ((), ()))


def _cumsum_rows(x, period):
    pos = lax.broadcasted_iota(jnp.int32, x.shape, 0) % period
    sh = 1
    while sh < period:
        x = x + jnp.where(pos >= sh, pltpu.roll(x, sh, axis=0), 0.0)
        sh *= 2
    return x


def _hgrn_kernel(lbl_ref, gain_ref, q_ref, f_ref, i_ref, g_ref, *rest, heads, layer, chunks, n_cast):
    w32_refs = rest[:n_cast]
    o_ref = rest[n_cast]
    w16_refs = rest[n_cast + 1:2 * n_cast + 1]
    st_scr, = rest[2 * n_cast + 1:]
    _cast_slabs(w32_refs, w16_refs)
    ch = HGRN_CHUNK
    dh = HEAD_DIM
    rows = chunks * ch

    @pl.when(pl.program_id(1) == 0)
    def _():
        st_scr[...] = jnp.zeros_like(st_scr)

    lg = lbl_ref[...]
    e = jnp.exp(lg - jnp.max(lg, axis=0, keepdims=True))
    lb = jnp.sum(e[:layer + 1], axis=0, keepdims=True) / jnp.sum(e, axis=0, keepdims=True)

    half = 0.5 * (1.0 - lb)
    f = (lb + half) + half * jnp.tanh(0.5 * f_ref[0])
    bcum = _cumsum_rows(jnp.log(f), ch)
    q_dec = (q_ref[0].astype(F32) * jnp.exp(bcum)).astype(BF16)
    k_dec32 = (1.0 - f) * jnp.exp(-bcum)
    k_dec = k_dec32.astype(BF16)
    decays = [jnp.exp(bcum[(c + 1) * ch - 1:(c + 1) * ch, :]) for c in range(chunks)]
    k_end = [(k_dec32[c * ch:(c + 1) * ch, :] * decays[c]).astype(BF16) for c in range(chunks)]
    v = i_ref[0]
    hg = 0.5 * g_ref[0].astype(F32)
    gate_gain = (hg + hg * jnp.tanh(hg)) * gain_ref[...]

    r = lax.broadcasted_iota(jnp.int32, (rows, rows), 0)
    c_ = lax.broadcasted_iota(jnp.int32, (rows, rows), 1)
    causal_in_chunk = (r >= c_) & (r // ch == c_ // ch)

    sls = [slice(hd * dh, (hd + 1) * dh) for hd in range(heads)]
    a_all = [jnp.where(causal_in_chunk,
                       lax.dot_general(q_dec[:, sl], k_dec[:, sl], NT_DIMS, preferred_element_type=F32), 0.0).astype(BF16)
             for sl in sls]
    o_all = [jnp.dot(a, v[:, sl], preferred_element_type=F32) for a, sl in zip(a_all, sls)]
    sts = [st_scr[hd] for hd in range(heads)]
    outs = [[] for _ in range(heads)]
    for c in range(chunks):
        rs = slice(c * ch, (c + 1) * ch)
        for hd, sl in enumerate(sls):
            st = sts[hd]
            outs[hd].append(o_all[hd][rs] + lax.dot_general(q_dec[rs, sl], st.astype(BF16), NT_DIMS,
                                                            preferred_element_type=F32))
            sts[hd] = st * decays[c][:, sl] + lax.dot_general(v[rs, sl], k_end[c][:, sl], TN_DIMS,
                                                              preferred_element_type=F32)
    for hd, sl in enumerate(sls):
        st_scr[hd] = sts[hd]
        o = jnp.concatenate(outs[hd], axis=0)
        ms = jnp.mean(o * o, axis=-1, keepdims=True)
        o_ref[0, :, sl] = (o * lax.rsqrt(ms + NORM_EPS) * gate_gain[:, sl]).astype(o_ref.dtype)


def _hgrn(proj3, fb3, lb_logits, gain, cast_weights, *, heads, layer, chunks=4):
    bsz, seq, _ = proj3.shape
    width = heads * HEAD_DIM
    rows = chunks * HGRN_CHUNK
    blk = (1, rows, width)
    nc = seq // rows
    cast_specs = _cast_specs(cast_weights, bsz * nc, lambda b, c: b * nc + c)
    outs = pl.pallas_call(
        functools.partial(_hgrn_kernel, heads=heads, layer=layer, chunks=chunks, n_cast=len(cast_weights)),
        grid=(bsz, nc),
        in_specs=[
            pl.BlockSpec(lb_logits.shape, lambda b, c: (0, 0)),
            pl.BlockSpec((1, width), lambda b, c: (0, 0)),
            pl.BlockSpec(blk, lambda b, c: (b, c, COL_QB)),
            pl.BlockSpec(blk, lambda b, c: (b, c, 0)),
            pl.BlockSpec(blk, lambda b, c: (b, c, COL_IB)),
            pl.BlockSpec(blk, lambda b, c: (b, c, COL_GB)),
            *cast_specs,
        ],
        out_specs=[pl.BlockSpec(blk, lambda b, c: (b, c, 0)), *cast_specs],
        out_shape=[jax.ShapeDtypeStruct((bsz, seq, width), BF16),
                   *(jax.ShapeDtypeStruct(w.shape, BF16) for w in cast_weights)],
        scratch_shapes=[pltpu.VMEM((heads, HEAD_DIM, HEAD_DIM), F32)],
        compiler_params=_params("arbitrary", "arbitrary"),
        name="hgrn2",
    )(lb_logits, gain, proj3, fb3, proj3, proj3, *cast_weights)
    return outs[0], outs[1:]


def _mix_kernel(ya_ref, yb_ref, ga0_ref, ga1_ref, gb0_ref, gb1_ref, x_ref, wa_ref, wb_ref, wo_ref, gain_ref,
                *rest, tc, n_cast):
    w32_refs = rest[:n_cast]
    x1_ref, h2_ref = rest[n_cast:n_cast + 2]
    w16_refs = rest[n_cast + 2:2 * n_cast + 2]
    mixed_scr, = rest[2 * n_cast + 2:]
    _cast_slabs(w32_refs, w16_refs)
    d = x_ref.shape[1]
    gate_refs = ((ga0_ref, gb0_ref), (ga1_ref, gb1_ref))
    ya = ya_ref[...]
    yb = yb_ref[...]
    for cb in range(d // tc):
        cs = slice(cb * tc, (cb + 1) * tc)
        ga_ref, gb_ref = gate_refs[(cb * tc) // SEG]
        gs = slice((cb * tc) % SEG, (cb * tc) % SEG + tc)
        pa = jnp.dot(ya, wa_ref[:, cs], preferred_element_type=F32)
        pb = jnp.dot(yb, wb_ref[:, cs], preferred_element_type=F32)
        mixed = (jax.nn.sigmoid(ga_ref[:, gs].astype(F32)) * pa
                 + jax.nn.sigmoid(gb_ref[:, gs].astype(F32)) * pb)
        mixed_scr[:, cs] = mixed.astype(BF16)
    ssq = jnp.zeros((x_ref.shape[0], 1), F32)
    for cb in range(d // tc):
        cs = slice(cb * tc, (cb + 1) * tc)
        x1 = x_ref[:, cs] + jnp.dot(mixed_scr[...], wo_ref[:, cs], preferred_element_type=F32)
        x1_ref[:, cs] = x1
        ssq = ssq + jnp.sum(x1 * x1, axis=-1, keepdims=True)
    inv = lax.rsqrt(ssq * (1.0 / d) + NORM_EPS)
    for cb in range(d // tc):
        cs = slice(cb * tc, (cb + 1) * tc)
        h2_ref[:, cs] = (x1_ref[:, cs] * inv * gain_ref[:, cs]).astype(BF16)


def _mix(ya, yb, proj, x2d, wa, wb, wo, gain, cast_weights, *, tm=512, tc=256):
    n, d = x2d.shape
    row = lambda i: (i, 0)
    seg = lambda c: pl.BlockSpec((tm, SEG), lambda i: (i, c))
    cast_specs = _cast_specs(cast_weights, n // tm, lambda i: i)
    outs = pl.pallas_call(
        functools.partial(_mix_kernel, tc=tc, n_cast=len(cast_weights)),
        grid=(n // tm,),
        in_specs=[
            pl.BlockSpec((tm, A_WIDTH), row),
            pl.BlockSpec((tm, B_WIDTH), row),
            seg(COL_GATE_A), seg(COL_GATE_A + 1), seg(COL_GATE_B), seg(COL_GATE_B + 1),
            pl.BlockSpec((tm, d), row),
            _resident(wa.shape), _resident(wb.shape), _resident(wo.shape), _resident(gain.shape),
            *cast_specs,
        ],
        out_specs=[pl.BlockSpec((tm, d), row), pl.BlockSpec((tm, d), row), *cast_specs],
        out_shape=[jax.ShapeDtypeStruct((n, d), F32), jax.ShapeDtypeStruct((n, d), BF16),
                   *(jax.ShapeDtypeStruct(w.shape, BF16) for w in cast_weights)],
        scratch_shapes=[pltpu.VMEM((tm, d), BF16)],
        compiler_params=_params("arbitrary"),
        name="mix",
    )(ya, yb, proj, proj, proj, proj, x2d, wa, wb, wo, gain, *cast_weights)
    return outs[0], outs[1], outs[2:]


def _ffn_kernel(h_ref, wg_ref, wu_ref, wd_ref, o_ref):
    @pl.when(pl.program_id(1) == 0)
    def _():
        o_ref[...] = jnp.zeros_like(o_ref)

    h = h_ref[...]
    hf = wg_ref.shape[1] // 2
    acts = []
    for c in range(2):
        cs = slice(c * hf, (c + 1) * hf)
        g = jnp.dot(h, wg_ref[:, cs], preferred_element_type=F32)
        u = jnp.dot(h, wu_ref[:, cs], preferred_element_type=F32)
        acts.append((g * jax.nn.sigmoid(g) * u).astype(BF16))
    o_ref[...] += (jnp.dot(acts[0], wd_ref[:hf, :], preferred_element_type=F32)
                   + jnp.dot(acts[1], wd_ref[hf:, :], preferred_element_type=F32))


def _ffn(h2, w_gu, w_down, *, tm=1024, tf=512):
    n, d = h2.shape
    dff = w_down.shape[0]
    nj = dff // tf
    return pl.pallas_call(
        _ffn_kernel,
        grid=(n // tm, nj),
        in_specs=[
            pl.BlockSpec((tm, d), lambda i, j: (i, 0)),
            pl.BlockSpec((d, tf), lambda i, j: (0, j)),
            pl.BlockSpec((d, tf), lambda i, j: (0, j + nj)),
            pl.BlockSpec((tf, d), lambda i, j: (j, 0)),
        ],
        out_specs=pl.BlockSpec((tm, d), lambda i, j: (i, 0)),
        out_shape=jax.ShapeDtypeStruct((n, d), F32),
        compiler_params=_params("arbitrary", "arbitrary"),
        name="ffn",
    )(h2, w_gu, w_gu, w_down)


def _ple_kernel(x1_ref, y_ref, p_ref, wg_ref, wp_ref, gain_ref, o_ref, x2_scr, *, tc, final_norm):
    d = x1_ref.shape[1]
    x2_scr[...] = x1_ref[...] + y_ref[...]
    xb = x2_scr[...].astype(BF16)
    pb = p_ref[...].astype(BF16)
    ssq = jnp.zeros((x1_ref.shape[0], 1), F32)
    for cb in range(d // tc):
        cs = slice(cb * tc, (cb + 1) * tc)
        gate = jax.nn.sigmoid(jnp.dot(xb, wg_ref[:, cs], preferred_element_type=F32))
        pe = jnp.dot(pb, wp_ref[:, cs], preferred_element_type=F32)
        x3 = x2_scr[:, cs] + gate * pe
        o_ref[:, cs] = x3
        ssq = ssq + jnp.sum(x3 * x3, axis=-1, keepdims=True)
    if not final_norm:
        return
    inv = lax.rsqrt(ssq * (1.0 / d) + NORM_EPS)
    for cb in range(d // tc):
        cs = slice(cb * tc, (cb + 1) * tc)
        o_ref[:, cs] = o_ref[:, cs] * inv * gain_ref[:, cs]


def _ple(x1, y, p2d, wg, wp, gain, *, final_norm, tm=512, tc=256):
    n, d = x1.shape
    row = lambda i: (i, 0)
    return pl.pallas_call(
        functools.partial(_ple_kernel, tc=tc, final_norm=final_norm),
        grid=(n // tm,),
        in_specs=[
            pl.BlockSpec((tm, d), row),
            pl.BlockSpec((tm, d), row),
            pl.BlockSpec((tm, p2d.shape[1]), row),
            _resident(wg.shape), _resident(wp.shape), _resident(gain.shape),
        ],
        out_specs=pl.BlockSpec((tm, d), row),
        out_shape=jax.ShapeDtypeStruct((n, d), F32),
        scratch_shapes=[pltpu.VMEM((tm, d), F32)],
        compiler_params=_params("arbitrary"),
        name="ple",
    )(x1, y, p2d, wg, wp, gain)


def kernel(x, p, norm_mix, w_in, hgrn_norm, w_proj_a, w_proj_b, w_out, norm_ffn, w_gate_up, w_down, w_ple,
           w_ple_gate, rel_bias, hgrn_lb_logits, norm_final):
    bsz, seq, d = x.shape
    n = bsz * seq
    depth = w_in.shape[0]
    a_heads = A_WIDTH // HEAD_DIM
    b_heads = B_WIDTH // HEAD_DIM

    cscale = np.ones((1, IN_WIDTH), np.float32)
    cscale[:, COL_QA * SEG:(COL_QA + 1) * SEG] = LOG2E / math.sqrt(HEAD_DIM)
    cscale = jnp.asarray(cscale)

    xc = x.reshape(n, d)
    for i in range(depth):
        proj, fb, (wgu,) = _in_proj(xc, norm_mix[i][None], w_in[i], cscale, (w_gate_up[i],))
        proj3 = proj.reshape(bsz, seq, IN_WIDTH)
        ya = _moba(proj3, rel_bias, heads=a_heads)
        yb, (wa, wb, wo, wdn) = _hgrn(proj3, fb.reshape(bsz, seq, SEG), hgrn_lb_logits, hgrn_norm[i][None],
                                      (w_proj_a[i], w_proj_b[i], w_out[i], w_down[i]), heads=b_heads, layer=i)
        x1, h2, (wpg, wpl) = _mix(ya.reshape(n, A_WIDTH), yb.reshape(n, B_WIDTH), proj, xc, wa, wb, wo,
                                  norm_ffn[i][None], (w_ple_gate[i], w_ple[i]))
        y = _ffn(h2, wgu, wdn)
        xc = _ple(x1, y, p[i].reshape(n, -1), wpg, wpl, norm_final[None], final_norm=(i == depth - 1))
    return xc.reshape(bsz, seq, d)
```

```python
import functools
import math

import jax
import jax.numpy as jnp
import numpy as np
from jax import lax
from jax.experimental import pallas as pl
from jax.experimental.pallas import tpu as pltpu

F32 = jnp.float32
BF16 = jnp.bfloat16

D_MODEL = 2048
HEAD_DIM = 128
A_WIDTH = 1024
B_WIDTH = 1024
MOBA_BLOCK = 256
MOBA_TOPK = 3
REL_BUCKETS = 32
REL_MAX_EXACT = 16
REL_MAX_DIST = 128
HGRN_CHUNK = 64
D_FF = 5632
NORM_EPS = 1e-6

COL_QA, COL_KA, COL_VA, COL_QB, COL_FB, COL_IB, COL_GB, COL_GATE_A, COL_GATE_B = 0, 1, 2, 3, 4, 5, 6, 7, 9
IN_WIDTH = 11 * 1024
SEG = 1024

BF16_SUBLANES = 16
MASK_VALUE = -1e30
VMEM_LIMIT = 56 * 1024 * 1024

NT_DIMS = (((1,), (1,)), ((), ()))
LOG2E = math.log2(math.e)


def _bucket_thresholds():
    n = np.arange(0, 4096, dtype=np.int64)
    nf = np.maximum(n, REL_MAX_EXACT).astype(np.float64)
    large = REL_MAX_EXACT + (np.log(nf / REL_MAX_EXACT) / math.log(REL_MAX_DIST / REL_MAX_EXACT)
                             * (REL_BUCKETS - REL_MAX_EXACT)).astype(np.int64)
    large = np.minimum(large, REL_BUCKETS - 1)
    bucket = np.where(n < REL_MAX_EXACT, n, large)
    return [int(np.argmax(bucket >= k)) for k in range(REL_BUCKETS)]


BUCKET_LO = _bucket_thresholds()


def _params(*sem):
    return pltpu.CompilerParams(dimension_semantics=sem, vmem_limit_bytes=VMEM_LIMIT)


def _resident(shape):
    return pl.BlockSpec(shape, lambda *_: (0,) * len(shape), pipeline_mode=pl.Buffered(1))


def _cast_specs(weights, n_steps, step_index):
    specs = []
    for w in weights:
        r = BF16_SUBLANES
        while w.shape[0] % r or w.shape[0] // r > n_steps:
            r += BF16_SUBLANES
        last = w.shape[0] // r - 1
        specs.append(pl.BlockSpec((r, w.shape[1]), lambda *ids, last=last: (jnp.minimum(step_index(*ids), last), 0)))
    return specs


def _cast_slabs(w32_refs, w16_refs):
    for w32, w16 in zip(w32_refs, w16_refs):
        w16[...] = w32[...].astype(BF16)


def _in_proj_kernel(x_ref, gain_ref, w_ref, cscale_ref, *rest, rows_per_step, first_tile, n_cast=0):
    i = pl.program_id(0)
    j = pl.program_id(1)
    if first_tile:
        o_ref, fb_ref, w16_ref, h_scr = rest
        x_tile = x_ref
    else:
        rest = rest[2:]
        o_ref, fb_ref = rest[n_cast:n_cast + 2]
        h_scr, x_buf, x_sem = rest[2 * n_cast + 2:]
        _cast_slabs(rest[:n_cast], rest[n_cast + 2:2 * n_cast + 2])
        tm = h_scr.shape[0]
        slot = i % 2
        x_tile = x_buf.at[slot]

        def x_copy(tile, to_slot):
            rows = pl.ds(pl.multiple_of((tile + 1) * tm, tm), tm)
            return pltpu.make_async_copy(x_ref.at[rows, :], x_buf.at[to_slot], x_sem.at[to_slot])

    @pl.when(j == 0)
    def _():
        if not first_tile:
            @pl.when(i == 0)
            def _():
                x_copy(0, 0).start()

            @pl.when(i + 1 < pl.num_programs(0))
            def _():
                x_copy(i + 1, 1 - slot).start()

            x_copy(i, slot).wait()
        for r in range(h_scr.shape[0] // rows_per_step):
            rs = slice(r * rows_per_step, (r + 1) * rows_per_step)
            x = x_tile[rs, :]
            ms = jnp.mean(x * x, axis=-1, keepdims=True)
            h_scr[rs, :] = (x * lax.rsqrt(ms + NORM_EPS) * gain_ref[...]).astype(BF16)

    if first_tile:
        w = w_ref[...].astype(BF16)
        w16_ref[...] = w
    else:
        w = w_ref[...]
    acc = jnp.dot(h_scr[...], w, preferred_element_type=F32)
    o_ref[...] = (acc * cscale_ref[...]).astype(BF16)
    pltpu.store(fb_ref, acc, mask=jnp.broadcast_to(j == COL_FB, acc.shape))


def _in_proj(x2d, gain, w_f32, cscale, cast_weights, *, tm=1024, tn=SEG):
    n, d = x2d.shape
    width = w_f32.shape[1]
    nj = width // tn
    out_shape = [jax.ShapeDtypeStruct((n, width), BF16), jax.ShapeDtypeStruct((n, SEG), F32)]
    scratch = [pltpu.VMEM((tm, d), BF16)]

    def specs(row0, x_mode):
        ins = [
            pl.BlockSpec((tm, d), lambda i, j: (i + row0, 0), **x_mode),
            pl.BlockSpec((1, d), lambda i, j: (0, 0)),
            pl.BlockSpec((d, tn), lambda i, j: (0, j)),
            pl.BlockSpec((1, tn), lambda i, j: (0, j)),
        ]
        outs = [
            pl.BlockSpec((tm, tn), lambda i, j: (i + row0, j)),
            pl.BlockSpec((tm, SEG), lambda i, j: (i + row0, 0)),
        ]
        return ins, outs

    ins, outs = specs(0, dict(pipeline_mode=pl.Buffered(1)))
    proj, fb, w16 = pl.pallas_call(
        functools.partial(_in_proj_kernel, rows_per_step=128, first_tile=True),
        grid=(1, nj),
        in_specs=ins,
        out_specs=outs + [pl.BlockSpec((d, tn), lambda i, j: (0, j))],
        out_shape=out_shape + [jax.ShapeDtypeStruct((d, width), BF16)],
        scratch_shapes=scratch,
        compiler_params=_params("arbitrary", "arbitrary"),
        name="in_proj_first",
    )(x2d, gain, w_f32, cscale)

    ins, outs = specs(1, {})
    any_spec = pl.BlockSpec(memory_space=pl.ANY)
    cast_specs = _cast_specs(cast_weights, (n // tm - 1) * nj, lambda i, j: i * nj + j)
    res = pl.pallas_call(
        functools.partial(_in_proj_kernel, rows_per_step=128, first_tile=False, n_cast=len(cast_weights)),
        grid=(n // tm - 1, nj),
        in_specs=[any_spec] + ins[1:] + [any_spec, any_spec] + cast_specs,
        out_specs=outs + cast_specs,
        out_shape=out_shape + [jax.ShapeDtypeStruct(w.shape, BF16) for w in cast_weights],
        input_output_aliases={4: 0, 5: 1},
        scratch_shapes=scratch + [pltpu.VMEM((2, tm, d), F32), pltpu.SemaphoreType.DMA((2,))],
        compiler_params=_params("arbitrary", "arbitrary"),
        name="in_proj",
    )(x2d, gain, w16, cscale, proj, fb, *cast_weights)
    return res[0], res[1], res[2:]


def _moba_kernel(rb_ref, qall_ref, q_ref, k_ref, v_ref, o_ref, bias_scr, vt_scr, sel_scr, s_scr, *, n_blocks, group,
                 q_per_step):
    hg = pl.program_id(0)
    b = pl.program_id(1)
    t = pl.program_id(2)
    blk = MOBA_BLOCK
    dh = HEAD_DIM
    sub = 8
    stride = n_blocks // q_per_step

    @pl.when((b == 0) & (t == 0))
    def _build_bias():
        r = lax.broadcasted_iota(jnp.int32, (blk, blk), 1)
        c = lax.broadcasted_iota(jnp.int32, (blk, blk), 0)
        for g in range(group):
            hd = hg * group + g
            for d in range(2):
                rel = d * blk + r - c
                bias = jnp.full((blk, blk), rb_ref[REL_BUCKETS - 1, hd] * LOG2E, F32)
                for kk in range(REL_BUCKETS - 2, -1, -1):
                    bias = jnp.where(rel < BUCKET_LO[kk + 1], rb_ref[kk, hd] * LOG2E, bias)
                if d == 0:
                    bias = jnp.where(rel < 0, MASK_VALUE, bias)
                bias_scr[g, d] = bias

    @pl.when(t == 0)
    def _per_sequence():
        for g in range(group):
            hs = slice(g * dh, (g + 1) * dh)
            means = []
            for jb in range(n_blocks):
                rs = slice(jb * blk, (jb + 1) * blk)
                means.append(jnp.mean(k_ref[0, rs, hs].astype(F32), axis=0, keepdims=True))
                vt_scr[g, :, rs] = v_ref[0, rs, hs].T
            km = jnp.concatenate(means + [jnp.zeros((16 - n_blocks, dh), F32)], axis=0)

            km_hi = km.astype(BF16)
            km_lo = (km - km_hi.astype(F32)).astype(BF16)
            qall = qall_ref[0, :, hs]
            sc = (lax.dot_general(km_hi, qall, NT_DIMS, preferred_element_type=F32)
                  + lax.dot_general(km_lo, qall, NT_DIMS, preferred_element_type=F32))
            rows = lax.broadcasted_iota(jnp.int32, sc.shape, 0)
            rows_f = rows.astype(F32)
            past = rows < lax.broadcasted_iota(jnp.int32, sc.shape, 1) // blk
            s = jnp.where(past, sc, -jnp.inf)
            picked = jnp.zeros(sc.shape, jnp.bool_)
            for _ in range(MOBA_TOPK):
                top = jnp.max(s, axis=0, keepdims=True)
                first = jnp.min(jnp.where(s == top, rows_f, 1e9), axis=0, keepdims=True)
                hit = rows_f == first
                picked = picked | hit
                s = jnp.where(hit, -jnp.inf, s)
            mask = jnp.where(picked & past, 0.0, MASK_VALUE)
            for qb in range(n_blocks):
                sel_scr[g, qb] = mask[:, qb * blk:(qb + 1) * blk]

    hss = [slice(g * dh, (g + 1) * dh) for g in range(group)]

    def logits_pass(slot, qb, row0, g):
        nt = qb + 1
        far = rb_ref[REL_BUCKETS - 1, hg * group + g] * LOG2E
        s = lax.dot_general(k_ref[0, :nt * blk, hss[g]], q_ref[0, slot, 0, :, hss[g]], NT_DIMS,
                            preferred_element_type=F32)
        m8 = None
        for j in range(nt):
            rs = slice(j * blk, (j + 1) * blk)
            d = nt - 1 - j
            if d == 0:
                sj = s[rs] + bias_scr[g, 0]
            elif d == 1:
                sj = s[rs] + bias_scr[g, 1] + sel_scr[g, qb, j:j + 1, :]
            else:
                sj = s[rs] + (sel_scr[g, qb, j:j + 1, :] + far)
            s_scr[g, row0 + j * blk:row0 + (j + 1) * blk, :] = sj
            mj = jnp.max(sj.reshape(blk // sub, sub, blk), axis=0)
            m8 = mj if m8 is None else jnp.maximum(m8, mj)
        return jnp.max(m8, axis=0, keepdims=True)

    def softmax_pv_pass(slot, qb, row0, g, m):
        l8 = jnp.zeros((sub, blk), F32)
        acc = None
        for j in range(qb + 1):
            rs = slice(j * blk, (j + 1) * blk)
            p = jnp.exp2(s_scr[g, row0 + j * blk:row0 + (j + 1) * blk, :] - m)
            l8 = l8 + jnp.sum(p.reshape(blk // sub, sub, blk), axis=0)
            pv = jnp.dot(vt_scr[g, :, rs], p.astype(BF16), preferred_element_type=F32)
            acc = pv if acc is None else acc + pv
        l = jnp.sum(l8, axis=0, keepdims=True)
        o_ref[0, slot, 0, :, hss[g]] = (acc / l).T.astype(o_ref.dtype)

    def attend_set(tt):
        work, row0 = [], 0
        for slot in range(q_per_step):
            qb = tt + slot * stride
            work.append((slot, qb, row0))
            row0 += (qb + 1) * blk
        ms = {(slot, g): logits_pass(slot, qb, r0, g) for g in range(group) for slot, qb, r0 in work}
        for g in range(group):
            for slot, qb, r0 in work:
                softmax_pv_pass(slot, qb, r0, g, ms[slot, g])

    for tt in range(stride):
        pl.when(t == tt)(functools.partial(attend_set, tt))


def _moba(proj3, rel_bias, *, heads, group=4, q_per_step=4):
    bsz, seq, width = proj3.shape
    nb = seq // MOBA_BLOCK
    stride = nb // q_per_step
    gw = group * HEAD_DIM
    cq, ck, cv = (c * (SEG // gw) for c in (COL_QA, COL_KA, COL_VA))
    proj5 = proj3.reshape(bsz, q_per_step, stride, MOBA_BLOCK, width)
    q_blk = (1, q_per_step, 1, MOBA_BLOCK, gw)
    s_rows = sum(nb - k * stride for k in range(q_per_step)) * MOBA_BLOCK
    out = pl.pallas_call(
        functools.partial(_moba_kernel, n_blocks=nb, group=group, q_per_step=q_per_step),
        grid=(heads // group, bsz, stride),
        in_specs=[
            pl.BlockSpec(memory_space=pltpu.SMEM),
            pl.BlockSpec((1, seq, gw), lambda h, b, i: (b, 0, cq + h)),
            pl.BlockSpec(q_blk, lambda h, b, i: (b, 0, i, 0, cq + h)),
            pl.BlockSpec((1, seq, gw), lambda h, b, i: (b, 0, ck + h)),
            pl.BlockSpec((1, seq, gw), lambda h, b, i: (b, 0, cv + h)),
        ],
        out_specs=pl.BlockSpec(q_blk, lambda h, b, i: (b, 0, i, 0, h)),
        out_shape=jax.ShapeDtypeStruct((bsz, q_per_step, stride, MOBA_BLOCK, heads * HEAD_DIM), BF16),
        scratch_shapes=[
            pltpu.VMEM((group, 2, MOBA_BLOCK, MOBA_BLOCK), F32),
            pltpu.VMEM((group, HEAD_DIM, seq), BF16),
            pltpu.VMEM((group, nb, 16, MOBA_BLOCK), F32),
            pltpu.VMEM((group, s_rows, MOBA_BLOCK), F32),
        ],
        compiler_params=_params("arbitrary", "arbitrary", "arbitrary"),
        name="moba",
    )(rel_bias, proj3, proj5, proj3, proj3)
    return out.reshape(bsz, seq, heads * HEAD_DIM)


TN_DIMS = (((0,), (0,)), ((), ()))


def _cumsum_rows(x, period):
    pos = lax.broadcasted_iota(jnp.int32, x.shape, 0) % period
    sh = 1
    while sh < period:
        x = x + jnp.where(pos >= sh, pltpu.roll(x, sh, axis=0), 0.0)
        sh *= 2
    return x


def _hgrn_kernel(lbl_ref, gain_ref, q_ref, f_ref, i_ref, g_ref, *rest, heads, layer, chunks, n_cast):
    w32_refs = rest[:n_cast]
    o_ref = rest[n_cast]
    w16_refs = rest[n_cast + 1:2 * n_cast + 1]
    st_scr, = rest[2 * n_cast + 1:]
    _cast_slabs(w32_refs, w16_refs)
    ch = HGRN_CHUNK
    dh = HEAD_DIM
    rows = chunks * ch

    @pl.when(pl.program_id(1) == 0)
    def _():
        st_scr[...] = jnp.zeros_like(st_scr)

    lg = lbl_ref[...]
    e = jnp.exp(lg - jnp.max(lg, axis=0, keepdims=True))
    lb = jnp.sum(e[:layer + 1], axis=0, keepdims=True) / jnp.sum(e, axis=0, keepdims=True)

    half = 0.5 * (1.0 - lb)
    f = (lb + half) + half * jnp.tanh(0.5 * f_ref[0])
    bcum = _cumsum_rows(jnp.log(f), ch)
    q_dec = (q_ref[0].astype(F32) * jnp.exp(bcum)).astype(BF16)
    k_dec32 = (1.0 - f) * jnp.exp(-bcum)
    k_dec = k_dec32.astype(BF16)
    decays = [jnp.exp(bcum[(c + 1) * ch - 1:(c + 1) * ch, :]) for c in range(chunks)]
    k_end = [(k_dec32[c * ch:(c + 1) * ch, :] * decays[c]).astype(BF16) for c in range(chunks)]
    v = i_ref[0]
    hg = 0.5 * g_ref[0].astype(F32)
    gate_gain = (hg + hg * jnp.tanh(hg)) * gain_ref[...]

    r = lax.broadcasted_iota(jnp.int32, (rows, rows), 0)
    c_ = lax.broadcasted_iota(jnp.int32, (rows, rows), 1)
    causal_in_chunk = (r >= c_) & (r // ch == c_ // ch)

    sls = [slice(hd * dh, (hd + 1) * dh) for hd in range(heads)]
    a_all = [jnp.where(causal_in_chunk,
                       lax.dot_general(q_dec[:, sl], k_dec[:, sl], NT_DIMS, preferred_element_type=F32), 0.0).astype(BF16)
             for sl in sls]
    o_all = [jnp.dot(a, v[:, sl], preferred_element_type=F32) for a, sl in zip(a_all, sls)]
    sts = [st_scr[hd] for hd in range(heads)]
    outs = [[] for _ in range(heads)]
    for c in range(chunks):
        rs = slice(c * ch, (c + 1) * ch)
        for hd, sl in enumerate(sls):
            st = sts[hd]
            outs[hd].append(o_all[hd][rs] + lax.dot_general(q_dec[rs, sl], st.astype(BF16), NT_DIMS,
                                                            preferred_element_type=F32))
            sts[hd] = st * decays[c][:, sl] + lax.dot_general(v[rs, sl], k_end[c][:, sl], TN_DIMS,
                                                              preferred_element_type=F32)
    for hd, sl in enumerate(sls):
        st_scr[hd] = sts[hd]
        o = jnp.concatenate(outs[hd], axis=0)
        ms = jnp.mean(o * o, axis=-1, keepdims=True)
        o_ref[0, :, sl] = (o * lax.rsqrt(ms + NORM_EPS) * gate_gain[:, sl]).astype(o_ref.dtype)


def _hgrn(proj3, fb3, lb_logits, gain, cast_weights, *, heads, layer, chunks=4):
    bsz, seq, _ = proj3.shape
    width = heads * HEAD_DIM
    rows = chunks * HGRN_CHUNK
    blk = (1, rows, width)
    nc = seq // rows
    cast_specs = _cast_specs(cast_weights, bsz * nc, lambda b, c: b * nc + c)
    outs = pl.pallas_call(
        functools.partial(_hgrn_kernel, heads=heads, layer=layer, chunks=chunks, n_cast=len(cast_weights)),
        grid=(bsz, nc),
        in_specs=[
            pl.BlockSpec(lb_logits.shape, lambda b, c: (0, 0)),
            pl.BlockSpec((1, width), lambda b, c: (0, 0)),
            pl.BlockSpec(blk, lambda b, c: (b, c, COL_QB)),
            pl.BlockSpec(blk, lambda b, c: (b, c, 0)),
            pl.BlockSpec(blk, lambda b, c: (b, c, COL_IB)),
            pl.BlockSpec(blk, lambda b, c: (b, c, COL_GB)),
            *cast_specs,
        ],
        out_specs=[pl.BlockSpec(blk, lambda b, c: (b, c, 0)), *cast_specs],
        out_shape=[jax.ShapeDtypeStruct((bsz, seq, width), BF16),
                   *(jax.ShapeDtypeStruct(w.shape, BF16) for w in cast_weights)],
        scratch_shapes=[pltpu.VMEM((heads, HEAD_DIM, HEAD_DIM), F32)],
        compiler_params=_params("arbitrary", "arbitrary"),
        name="hgrn2",
    )(lb_logits, gain, proj3, fb3, proj3, proj3, *cast_weights)
    return outs[0], outs[1:]


def _mix_kernel(ya_ref, yb_ref, ga0_ref, ga1_ref, gb0_ref, gb1_ref, x_ref, wa_ref, wb_ref, wo_ref, gain_ref,
                *rest, tc, n_cast):
    w32_refs = rest[:n_cast]
    x1_ref, h2_ref = rest[n_cast:n_cast + 2]
    w16_refs = rest[n_cast + 2:2 * n_cast + 2]
    mixed_scr, = rest[2 * n_cast + 2:]
    _cast_slabs(w32_refs, w16_refs)
    d = x_ref.shape[1]
    gate_refs = ((ga0_ref, gb0_ref), (ga1_ref, gb1_ref))
    ya = ya_ref[...]
    yb = yb_ref[...]
    for cb in range(d // tc):
        cs = slice(cb * tc, (cb + 1) * tc)
        ga_ref, gb_ref = gate_refs[(cb * tc) // SEG]
        gs = slice((cb * tc) % SEG, (cb * tc) % SEG + tc)
        pa = jnp.dot(ya, wa_ref[:, cs], preferred_element_type=F32)
        pb = jnp.dot(yb, wb_ref[:, cs], preferred_element_type=F32)
        mixed = (jax.nn.sigmoid(ga_ref[:, gs].astype(F32)) * pa
                 + jax.nn.sigmoid(gb_ref[:, gs].astype(F32)) * pb)
        mixed_scr[:, cs] = mixed.astype(BF16)
    ssq = jnp.zeros((x_ref.shape[0], 1), F32)
    for cb in range(d // tc):
        cs = slice(cb * tc, (cb + 1) * tc)
        x1 = x_ref[:, cs] + jnp.dot(mixed_scr[...], wo_ref[:, cs], preferred_element_type=F32)
        x1_ref[:, cs] = x1
        ssq = ssq + jnp.sum(x1 * x1, axis=-1, keepdims=True)
    inv = lax.rsqrt(ssq * (1.0 / d) + NORM_EPS)
    for cb in range(d // tc):
        cs = slice(cb * tc, (cb + 1) * tc)
        h2_ref[:, cs] = (x1_ref[:, cs] * inv * gain_ref[:, cs]).astype(BF16)


def _mix(ya, yb, proj, x2d, wa, wb, wo, gain, cast_weights, *, tm=512, tc=256):
    n, d = x2d.shape
    row = lambda i: (i, 0)
    seg = lambda c: pl.BlockSpec((tm, SEG), lambda i: (i, c))
    cast_specs = _cast_specs(cast_weights, n // tm, lambda i: i)
    outs = pl.pallas_call(
        functools.partial(_mix_kernel, tc=tc, n_cast=len(cast_weights)),
        grid=(n // tm,),
        in_specs=[
            pl.BlockSpec((tm, A_WIDTH), row),
            pl.BlockSpec((tm, B_WIDTH), row),
            seg(COL_GATE_A), seg(COL_GATE_A + 1), seg(COL_GATE_B), seg(COL_GATE_B + 1),
            pl.BlockSpec((tm, d), row),
            _resident(wa.shape), _resident(wb.shape), _resident(wo.shape), _resident(gain.shape),
            *cast_specs,
        ],
        out_specs=[pl.BlockSpec((tm, d), row), pl.BlockSpec((tm, d), row), *cast_specs],
        out_shape=[jax.ShapeDtypeStruct((n, d), F32), jax.ShapeDtypeStruct((n, d), BF16),
                   *(jax.ShapeDtypeStruct(w.shape, BF16) for w in cast_weights)],
        scratch_shapes=[pltpu.VMEM((tm, d), BF16)],
        compiler_params=_params("arbitrary"),
        name="mix",
    )(ya, yb, proj, proj, proj, proj, x2d, wa, wb, wo, gain, *cast_weights)
    return outs[0], outs[1], outs[2:]


def _ffn_kernel(h_hbm, wg_ref, wu_ref, wd_ref, o_ref, h_buf, h_sem):
    i = pl.program_id(0)
    tm = o_ref.shape[0]
    slot = i % 2

    def h_copy(tile, to_slot):
        rows = pl.ds(pl.multiple_of(tile * tm, tm), tm)
        return pltpu.make_async_copy(h_hbm.at[rows, :], h_buf.at[to_slot], h_sem.at[to_slot])

    @pl.when(pl.program_id(1) == 0)
    def _():
        @pl.when(i == 0)
        def _():
            h_copy(0, 0).start()

        @pl.when(i + 1 < pl.num_programs(0))
        def _():
            h_copy(i + 1, 1 - slot).start()

        h_copy(i, slot).wait()
        o_ref[...] = jnp.zeros_like(o_ref)

    h = h_buf[slot]
    hf = wg_ref.shape[1] // 2
    acts = []
    for c in range(2):
        cs = slice(c * hf, (c + 1) * hf)
        g = jnp.dot(h, wg_ref[:, cs], preferred_element_type=F32)
        u = jnp.dot(h, wu_ref[:, cs], preferred_element_type=F32)
        acts.append((g * jax.nn.sigmoid(g) * u).astype(BF16))
    o_ref[...] += (jnp.dot(acts[0], wd_ref[:hf, :], preferred_element_type=F32)
                   + jnp.dot(acts[1], wd_ref[hf:, :], preferred_element_type=F32))


def _ffn(h2, w_gu, w_down, *, tm=1024, tf=512):
    n, d = h2.shape
    dff = w_down.shape[0]
    nj = dff // tf
    return pl.pallas_call(
        _ffn_kernel,
        grid=(n // tm, nj),
        in_specs=[
            pl.BlockSpec(memory_space=pl.ANY),
            pl.BlockSpec((d, tf), lambda i, j: (0, j)),
            pl.BlockSpec((d, tf), lambda i, j: (0, j + nj)),
            pl.BlockSpec((tf, d), lambda i, j: (j, 0)),
        ],
        out_specs=pl.BlockSpec((tm, d), lambda i, j: (i, 0)),
        out_shape=jax.ShapeDtypeStruct((n, d), F32),
        scratch_shapes=[pltpu.VMEM((2, tm, d), BF16), pltpu.SemaphoreType.DMA((2,))],
        compiler_params=_params("arbitrary", "arbitrary"),
        name="ffn",
    )(h2, w_gu, w_gu, w_down)


def _ple_kernel(x1_ref, y_ref, p_ref, wg_ref, wp_ref, gain_ref, o_ref, x2_scr, *, tc, final_norm):
    d = x1_ref.shape[1]
    x2_scr[...] = x1_ref[...] + y_ref[...]
    xb = x2_scr[...].astype(BF16)
    pb = p_ref[...].astype(BF16)
    ssq = jnp.zeros((x1_ref.shape[0], 1), F32)
    for cb in range(d // tc):
        cs = slice(cb * tc, (cb + 1) * tc)
        gate = jax.nn.sigmoid(jnp.dot(xb, wg_ref[:, cs], preferred_element_type=F32))
        pe = jnp.dot(pb, wp_ref[:, cs], preferred_element_type=F32)
        x3 = x2_scr[:, cs] + gate * pe
        o_ref[:, cs] = x3
        ssq = ssq + jnp.sum(x3 * x3, axis=-1, keepdims=True)
    if not final_norm:
        return
    inv = lax.rsqrt(ssq * (1.0 / d) + NORM_EPS)
    for cb in range(d // tc):
        cs = slice(cb * tc, (cb + 1) * tc)
        o_ref[:, cs] = o_ref[:, cs] * inv * gain_ref[:, cs]


def _ple(x1, y, p2d, wg, wp, gain, *, final_norm, tm=512, tc=256):
    n, d = x1.shape
    row = lambda i: (i, 0)
    return pl.pallas_call(
        functools.partial(_ple_kernel, tc=tc, final_norm=final_norm),
        grid=(n // tm,),
        in_specs=[
            pl.BlockSpec((tm, d), row),
            pl.BlockSpec((tm, d), row),
            pl.BlockSpec((tm, p2d.shape[1]), row),
            _resident(wg.shape), _resident(wp.shape), _resident(gain.shape),
        ],
        out_specs=pl.BlockSpec((tm, d), row),
        out_shape=jax.ShapeDtypeStruct((n, d), F32),
        scratch_shapes=[pltpu.VMEM((tm, d), F32)],
        compiler_params=_params("arbitrary"),
        name="ple",
    )(x1, y, p2d, wg, wp, gain)


def kernel(x, p, norm_mix, w_in, hgrn_norm, w_proj_a, w_proj_b, w_out, norm_ffn, w_gate_up, w_down, w_ple,
           w_ple_gate, rel_bias, hgrn_lb_logits, norm_final):
    bsz, seq, d = x.shape
    n = bsz * seq
    depth = w_in.shape[0]
    a_heads = A_WIDTH // HEAD_DIM
    b_heads = B_WIDTH // HEAD_DIM

    cscale = np.ones((1, IN_WIDTH), np.float32)
    cscale[:, COL_QA * SEG:(COL_QA + 1) * SEG] = LOG2E / math.sqrt(HEAD_DIM)
    cscale = jnp.asarray(cscale)

    xc = x.reshape(n, d)
    for i in range(depth):
        proj, fb, (wgu,) = _in_proj(xc, norm_mix[i][None], w_in[i], cscale, (w_gate_up[i],))
        proj3 = proj.reshape(bsz, seq, IN_WIDTH)
        ya = _moba(proj3, rel_bias, heads=a_heads)
        yb, (wa, wb, wo, wdn) = _hgrn(proj3, fb.reshape(bsz, seq, SEG), hgrn_lb_logits, hgrn_norm[i][None],
                                      (w_proj_a[i], w_proj_b[i], w_out[i], w_down[i]), heads=b_heads, layer=i)
        x1, h2, (wpg, wpl) = _mix(ya.reshape(n, A_WIDTH), yb.reshape(n, B_WIDTH), proj, xc, wa, wb, wo,
                                  norm_ffn[i][None], (w_ple_gate[i], w_ple[i]))
        y = _ffn(h2, wgu, wdn)
        xc = _ple(x1, y, p[i].reshape(n, -1), wpg, wpl, norm_final[None], final_norm=(i == depth - 1))
    return xc.reshape(bsz, seq, d)
```
